```python
import math
import jax
import jax.numpy as jnp
from jax import lax
import numpy as np

D_MODEL = 2048
BATCH = 1
SEQ = 16384
DEPTH = 4
DEC_BATCH = 16
DEC_SEQ = 32
PAST_LEN = 4096

CHUNK = 64
N_HEADS = 16
HEAD_DIM = D_MODEL // N_HEADS
SCALE = 1.0 / math.sqrt(HEAD_DIM)
BAND_CHUNKS = 8
BAND_PAST = BAND_CHUNKS * CHUNK
MAX_REL = 2 * CHUNK
D_FF = 5504
CONV_W = 3
Q_BLOCK = 128
EPS = 1e-6
N_MIXERS = 2
N_FOX = (DEPTH + 1) // 2
N_BAND = DEPTH // 2

kernel_name = 'fox_chunkband_convffn_stream_step'


def rmsnorm(x, g):
    x32 = x.astype(jnp.float32)
    y = x32 * lax.rsqrt(jnp.mean(x32 * x32, axis=-1, keepdims=True) + EPS)
    return (y * g.astype(jnp.float32)).astype(x.dtype)


def split_heads(t):
    return t.reshape(t.shape[0], t.shape[1], N_HEADS, HEAD_DIM)


def project_qkv(h, w_qkv):
    q, k, v = jnp.split(h @ w_qkv, 3, axis=-1)
    return split_heads(q), split_heads(k), split_heads(v)


def forget_log(h, w_f, b_f):
    z = (h @ w_f).astype(jnp.float32) + b_f.astype(jnp.float32)
    return jax.nn.log_sigmoid(z)


def fox_attend(q, k, v, cum_q, cum_k, q_pos, k_pos):
    s = jnp.einsum('bqhd,bkhd->bhqk', q, k, preferred_element_type=jnp.float32) * SCALE
    s = s + jnp.transpose(cum_q, (0, 2, 1))[:, :, :, None] - jnp.transpose(cum_k, (0, 2, 1))[:, :, None, :]
    s = jnp.where(k_pos[None, :] <= q_pos[:, None], s, -jnp.inf)
    p = jax.nn.softmax(s, axis=-1).astype(v.dtype)
    return jnp.einsum('bhqk,bkhd->bqhd', p, v)


def fox_prompt(q, k, v, logf):
    b, s_len = q.shape[0], q.shape[1]
    cum = jnp.cumsum(logf, axis=1)
    k_pos = jnp.arange(s_len)

    def block(i):
        start = i * Q_BLOCK
        qb = lax.dynamic_slice_in_dim(q, start, Q_BLOCK, axis=1)
        cq = lax.dynamic_slice_in_dim(cum, start, Q_BLOCK, axis=1)
        return fox_attend(qb, k, v, cq, cum, start + jnp.arange(Q_BLOCK), k_pos)

    out = lax.map(block, jnp.arange(s_len // Q_BLOCK))
    return jnp.swapaxes(out, 0, 1).reshape(b, s_len, N_HEADS, HEAD_DIM)


def fox_sample(q, k, v, logf, ck, cv, clogf):
    past, t = ck.shape[1], q.shape[1]
    k_all = jnp.concatenate([ck, k.astype(ck.dtype)], axis=1)
    v_all = jnp.concatenate([cv, v.astype(cv.dtype)], axis=1)
    cum = jnp.cumsum(jnp.concatenate([clogf.astype(jnp.float32), logf], axis=1), axis=1)
    return fox_attend(q, k_all, v_all, cum[:, past:], cum, past + jnp.arange(t), jnp.arange(past + t))


def band_attend(q, k, v, q_pos, k_pos, rel_table):
    s = jnp.einsum('bqhd,bkhd->bhqk', q, k, preferred_element_type=jnp.float32) * SCALE
    rel = jnp.clip(q_pos[:, None] - k_pos[None, :], -MAX_REL, MAX_REL) + MAX_REL
    s = s + rel_table.astype(jnp.float32)[:, rel][None]
    qc = q_pos // CHUNK
    kc = k_pos // CHUNK
    ok = (k_pos[None, :] >= 0) & (kc[None, :] <= qc[:, None]) & (kc[None, :] >= qc[:, None] - BAND_CHUNKS)
    s = jnp.where(ok, s, -jnp.inf)
    p = jax.nn.softmax(s, axis=-1).astype(v.dtype)
    return jnp.einsum('bhqk,bkhd->bqhd', p, v)


def band_prompt(q, k, v, rel_table):
    b, s_len = q.shape[0], q.shape[1]
    pad = ((0, 0), (BAND_PAST, 0), (0, 0), (0, 0))
    kp = jnp.pad(k, pad)
    vp = jnp.pad(v, pad)
    span = BAND_PAST + CHUNK

    def chunk(c):
        start = c * CHUNK
        qb = lax.dynamic_slice_in_dim(q, start, CHUNK, axis=1)
        kb = lax.dynamic_slice_in_dim(kp, start, span, axis=1)
        vb = lax.dynamic_slice_in_dim(vp, start, span, axis=1)
        q_pos = start + jnp.arange(CHUNK)
        k_pos = start - BAND_PAST + jnp.arange(span)
        return band_attend(qb, kb, vb, q_pos, k_pos, rel_table)

    out = lax.map(chunk, jnp.arange(s_len // CHUNK))
    return jnp.swapaxes(out, 0, 1).reshape(b, s_len, N_HEADS, HEAD_DIM)


def band_sample(q, k, v, ck, cv, rel_table, past):
    nb, t = ck.shape[1], q.shape[1]
    k_all = jnp.concatenate([ck, k.astype(ck.dtype)], axis=1)
    v_all = jnp.concatenate([cv, v.astype(cv.dtype)], axis=1)
    q_pos = past + jnp.arange(t)
    k_pos = past - nb + jnp.arange(nb + t)
    return band_attend(q, k_all, v_all, q_pos, k_pos, rel_table)


def conv_ffn(h, hist, w_in, conv_w, conv_b, w_out):
    t = h.shape[1]
    u = h @ w_in
    up = jnp.concatenate([hist.astype(u.dtype), u], axis=1)
    c = conv_b + conv_w[0] * up[:, 0:t]
    for j in range(1, CONV_W):
        c = c + conv_w[j] * up[:, j:j + t]
    gate, val = jnp.split(c, 2, axis=-1)
    y = (jax.nn.silu(gate) * val) @ w_out
    return y, up[:, -(CONV_W - 1):]


def setup_inputs(seed: int = 0) -> dict:
    key = jax.random.key(seed)
    ks = jax.random.split(key, 24)
    f32 = jnp.float32
    nb_past = min(BAND_PAST, PAST_LEN)
    inp = {}
    inp['x_prompt'] = jax.random.normal(ks[0], (BATCH, SEQ, D_MODEL), f32)
    inp['x_sample'] = jax.random.normal(ks[1], (DEC_BATCH, DEC_SEQ, D_MODEL), f32)
    inp['cache_fox_k'] = jax.random.normal(ks[2], (N_FOX, DEC_BATCH, PAST_LEN, N_HEADS, HEAD_DIM), f32)
    inp['cache_fox_v'] = jax.random.normal(ks[3], (N_FOX, DEC_BATCH, PAST_LEN, N_HEADS, HEAD_DIM), f32)
    inp['cache_fox_logf'] = jax.nn.log_sigmoid(3.5 + 1.5 * jax.random.normal(ks[4], (N_FOX, DEC_BATCH, PAST_LEN, N_HEADS), f32))
    inp['cache_band_k'] = jax.random.normal(ks[5], (N_BAND, DEC_BATCH, nb_past, N_HEADS, HEAD_DIM), f32)
    inp['cache_band_v'] = jax.random.normal(ks[6], (N_BAND, DEC_BATCH, nb_past, N_HEADS, HEAD_DIM), f32)
    inp['state_conv'] = jax.random.normal(ks[7], (DEPTH, DEC_BATCH, CONV_W - 1, 2 * D_FF), f32)
    inp['g_attn'] = 1.0 + 0.02 * jax.random.normal(ks[8], (DEPTH, D_MODEL), f32)
    inp['w_qkv'] = jax.random.normal(ks[9], (DEPTH, D_MODEL, 3 * D_MODEL), f32) * D_MODEL ** -0.5
    inp['w_fgate'] = jax.random.normal(ks[10], (N_FOX, D_MODEL, N_HEADS), f32) * D_MODEL ** -0.5
    inp['b_fgate'] = jax.random.uniform(ks[11], (N_FOX, N_HEADS), f32, 1.0, 6.0)
    inp['rel_bias'] = 0.5 * jax.random.normal(ks[12], (N_BAND, N_HEADS, 2 * MAX_REL + 1), f32)
    inp['w_o'] = jax.random.normal(ks[13], (DEPTH, D_MODEL, D_MODEL), f32) * D_MODEL ** -0.5
    inp['g_ffn'] = 1.0 + 0.02 * jax.random.normal(ks[14], (DEPTH, D_MODEL), f32)
    inp['w_in'] = jax.random.normal(ks[15], (DEPTH, D_MODEL, 2 * D_FF), f32) * D_MODEL ** -0.5
    inp['conv_w'] = jax.random.normal(ks[16], (DEPTH, CONV_W, 2 * D_FF), f32) * CONV_W ** -0.5
    inp['conv_b'] = 0.01 * jax.random.normal(ks[17], (DEPTH, 2 * D_FF), f32)
    inp['w_out'] = jax.random.normal(ks[18], (DEPTH, D_FF, D_MODEL), f32) * D_FF ** -0.5
    inp['g_final'] = 1.0 + 0.02 * jax.random.normal(ks[19], (D_MODEL,), f32)
    return inp


def reference(x_prompt, x_sample, cache_fox_k, cache_fox_v, cache_fox_logf, cache_band_k, cache_band_v,
              state_conv, g_attn, w_qkv, w_fgate, b_fgate, rel_bias, w_o, g_ffn, w_in, conv_w, conv_b,
              w_out, g_final):
    past = cache_fox_k.shape[2]
    bp, sp = x_prompt.shape[0], x_prompt.shape[1]
    bs, ss = x_sample.shape[0], x_sample.shape[1]
    n_keep = min(BAND_PAST, sp)
    xp, xs = x_prompt, x_sample
    fox_kp, fox_vp, fox_lp, fox_ks, fox_vs, fox_ls = [], [], [], [], [], []
    band_kp, band_vp, band_ks, band_vs = [], [], [], []
    conv_p, conv_s = [], []
    for i in range(DEPTH):
        j = i // N_MIXERS
        hp = rmsnorm(xp, g_attn[i])
        hs = rmsnorm(xs, g_attn[i])
        qp, kp, vp = project_qkv(hp, w_qkv[i])
        qs, ks, vs = project_qkv(hs, w_qkv[i])
        if i % N_MIXERS == 0:
            lfp = forget_log(hp, w_fgate[j], b_fgate[j])
            lfs = forget_log(hs, w_fgate[j], b_fgate[j])
            op = fox_prompt(qp, kp, vp, lfp)
            osm = fox_sample(qs, ks, vs, lfs, cache_fox_k[j], cache_fox_v[j], cache_fox_logf[j])
            fox_kp.append(kp)
            fox_vp.append(vp)
            fox_lp.append(lfp)
            fox_ks.append(ks)
            fox_vs.append(vs)
            fox_ls.append(lfs)
        else:
            op = band_prompt(qp, kp, vp, rel_bias[j])
            osm = band_sample(qs, ks, vs, cache_band_k[j], cache_band_v[j], rel_bias[j], past)
            band_kp.append(kp[:, sp - n_keep:])
            band_vp.append(vp[:, sp - n_keep:])
            band_ks.append(ks)
            band_vs.append(vs)
        xp = xp + op.reshape(bp, sp, D_MODEL) @ w_o[i]
        xs = xs + osm.reshape(bs, ss, D_MODEL) @ w_o[i]
        hist0 = jnp.zeros((bp, CONV_W - 1, 2 * D_FF), xp.dtype)
        fp, hp_new = conv_ffn(rmsnorm(xp, g_ffn[i]), hist0, w_in[i], conv_w[i], conv_b[i], w_out[i])
        fs, hs_new = conv_ffn(rmsnorm(xs, g_ffn[i]), state_conv[i], w_in[i], conv_w[i], conv_b[i], w_out[i])
        xp = xp + fp
        xs = xs + fs
        conv_p.append(hp_new)
        conv_s.append(hs_new)
    y_prompt = rmsnorm(xp, g_final)
    y_sample = rmsnorm(xs, g_final)
    return (y_prompt, y_sample,
            jnp.stack(fox_kp), jnp.stack(fox_vp), jnp.stack(fox_lp),
            jnp.stack(fox_ks), jnp.stack(fox_vs), jnp.stack(fox_ls),
            jnp.stack(band_kp), jnp.stack(band_vp), jnp.stack(band_ks), jnp.stack(band_vs),
            jnp.stack(conv_p), jnp.stack(conv_s))
```

```python
import functools
import math

import numpy as np
import jax
import jax.numpy as jnp
from jax import lax
from jax.experimental import pallas as pl
from jax.experimental.pallas import tpu as pltpu

F32 = jnp.float32
BF16 = jnp.bfloat16

D_MODEL = 2048
N_HEADS = 16
HEAD_DIM = D_MODEL // N_HEADS
SCALE = 1.0 / math.sqrt(HEAD_DIM)
CHUNK = 64
BAND_CHUNKS = 8
BAND_PAST = BAND_CHUNKS * CHUNK
MAX_REL = 2 * CHUNK
D_FF = 5504
CONV_W = 3
EPS = 1e-6
N_MIXERS = 2

LANES = 128
SUBLANES = 8
VMEM_LIMIT = 56 * 1024 * 1024

TM = 512
TN_QKV = 512
TF = 512
D_FF_PAD = ((D_FF + TF - 1) // TF) * TF
TQ_FOX = 512
TK_FOX_S = 512
TQ_BAND = 256
CUM_BLK = 128


def _cparams(sem):
    return pltpu.CompilerParams(dimension_semantics=sem, vmem_limit_bytes=VMEM_LIMIT)


def _rmsnorm_f32(x, g):
    ms = jnp.mean(x * x, axis=-1, keepdims=True)
    return (x * lax.rsqrt(ms + EPS)) * g


def _qkv_kernel(*refs, with_fgate):
    if with_fgate:
        (x_ref, g_ref, wq_ref, wk_ref, wv_ref, wf_ref, bf_ref,
         q_ref, k32_ref, v32_ref, kb_ref, vb_ref, lf_ref, h_ref) = refs
    else:
        (x_ref, g_ref, wq_ref, wk_ref, wv_ref,
         q_ref, k32_ref, v32_ref, kb_ref, vb_ref, h_ref) = refs

    @pl.when(pl.program_id(1) == 0)
    def _():
        hb = _rmsnorm_f32(x_ref[...], g_ref[...]).astype(BF16)
        h_ref[...] = hb
        if with_fgate:
            z = jnp.dot(hb, wf_ref[...], preferred_element_type=F32) + bf_ref[...]
            lf_ref[...] = jnp.minimum(z, 0.0) - jnp.log1p(jnp.exp(-jnp.abs(z)))

    h = h_ref[...]
    q = jnp.dot(h, wq_ref[...], preferred_element_type=F32)
    q_ref[...] = (q * SCALE).astype(BF16)
    k = jnp.dot(h, wk_ref[...], preferred_element_type=F32)
    k32_ref[...] = k
    kb_ref[...] = k.astype(BF16)
    v = jnp.dot(h, wv_ref[...], preferred_element_type=F32)
    v32_ref[...] = v
    vb_ref[...] = v.astype(BF16)


def _qkv_proj(x, g, w_qkv_b, wf_b=None, bf=None):
    m = x.shape[0]
    tm = min(TM, m)
    nn = D_MODEL // TN_QKV
    with_fgate = wf_b is not None
    in_specs = [
        pl.BlockSpec((tm, D_MODEL), lambda i, n: (i, 0)),
        pl.BlockSpec((1, D_MODEL), lambda i, n: (0, 0)),
        pl.BlockSpec((D_MODEL, TN_QKV), lambda i, n: (0, n)),
        pl.BlockSpec((D_MODEL, TN_QKV), lambda i, n: (0, n + nn)),
        pl.BlockSpec((D_MODEL, TN_QKV), lambda i, n: (0, n + 2 * nn)),
    ]
    args = [x, g.reshape(1, D_MODEL), w_qkv_b, w_qkv_b, w_qkv_b]
    tile = pl.BlockSpec((tm, TN_QKV), lambda i, n: (i, n))
    out_specs = [tile] * 5
    out_shape = [
        jax.ShapeDtypeStruct((m, D_MODEL), BF16),
        jax.ShapeDtypeStruct((m, D_MODEL), F32),
        jax.ShapeDtypeStruct((m, D_MODEL), F32),
        jax.ShapeDtypeStruct((m, D_MODEL), BF16),
        jax.ShapeDtypeStruct((m, D_MODEL), BF16),
    ]
    if with_fgate:
        in_specs += [
            pl.BlockSpec((D_MODEL, LANES), lambda i, n: (0, 0)),
            pl.BlockSpec((1, LANES), lambda i, n: (0, 0)),
        ]
        args += [wf_b, bf]
        out_specs.append(pl.BlockSpec((tm, LANES), lambda i, n: (i, 0)))
        out_shape.append(jax.ShapeDtypeStruct((m, LANES), F32))
    return pl.pallas_call(
        functools.partial(_qkv_kernel, with_fgate=with_fgate),
        grid=(m // tm, nn),
        in_specs=in_specs,
        out_specs=out_specs,
        out_shape=out_shape,
        scratch_shapes=[pltpu.VMEM((tm, D_MODEL), BF16)],
        compiler_params=_cparams(("arbitrary", "arbitrary")),
        name="qkv_proj",
    )(*args)


def _cumsum_kernel(x_ref, o_ref, carry_ref):
    @pl.when(pl.program_id(0) == 0)
    def _():
        carry_ref[...] = jnp.zeros_like(carry_ref)

    x = x_ref[...]
    hi = x.astype(BF16)
    r1 = x - hi.astype(F32)
    mid = r1.astype(BF16)
    lo = (r1 - mid.astype(F32)).astype(BF16)
    row = lax.broadcasted_iota(jnp.int32, (CUM_BLK, CUM_BLK), 0)
    col = lax.broadcasted_iota(jnp.int32, (CUM_BLK, CUM_BLK), 1)
    tri = (row <= col).astype(BF16)
    s = jnp.dot(lo, tri, preferred_element_type=F32)
    s = s + jnp.dot(mid, tri, preferred_element_type=F32)
    s = s + jnp.dot(hi, tri, preferred_element_type=F32)
    s = s + carry_ref[...]
    o_ref[...] = s
    carry_ref[...] = jnp.broadcast_to(s[:, CUM_BLK - 1:CUM_BLK], s.shape)


def _cumsum_lanes(x):
    r, t = x.shape
    return pl.pallas_call(
        _cumsum_kernel,
        grid=(t // CUM_BLK,),
        in_specs=[pl.BlockSpec((r, CUM_BLK), lambda i: (0, i))],
        out_specs=pl.BlockSpec((r, CUM_BLK), lambda i: (0, i)),
        out_shape=jax.ShapeDtypeStruct((r, t), F32),
        scratch_shapes=[pltpu.VMEM((r, CUM_BLK), F32)],
        compiler_params=_cparams(("arbitrary",)),
        name="cumsum_time",
    )(x)


def _online_softmax_step(s, v, m_ref, l_ref, acc_ref):
    m_prev = m_ref[...]
    m_new = jnp.maximum(m_prev, jnp.max(s, axis=-1, keepdims=True))
    alpha = jnp.exp(m_prev - m_new)
    p = jnp.exp(s - m_new)
    l_ref[...] = alpha * l_ref[...] + jnp.sum(p, axis=-1, keepdims=True)
    acc_ref[...] = alpha * acc_ref[...] + jnp.dot(p.astype(BF16), v, preferred_element_type=F32)
    m_ref[...] = m_new


def _fox_prompt_kernel(q_ref, k_ref, v_ref, cq_ref, ck_ref, o_ref, m_ref, l_ref, acc_ref):
    h = pl.program_id(0)
    qi = pl.program_id(1)
    t = TQ_FOX
    lane = lax.broadcasted_iota(jnp.int32, (t, N_HEADS), 1)
    cq = jnp.sum(jnp.where(lane == h, cq_ref[...], 0.0), axis=-1, keepdims=True)
    m_ref[...] = jnp.full_like(m_ref, -jnp.inf)
    l_ref[...] = jnp.zeros_like(l_ref)
    acc_ref[...] = jnp.zeros_like(acc_ref)
    q = q_ref[...]

    def scores(ki, ks):
        k = k_ref[pl.ds(ks, t), :]
        s = lax.dot_general(q, k, (((1,), (1,)), ((), ())), preferred_element_type=F32)
        return (s + cq) - ck_ref[0, ki]

    def body(ki, carry):
        ks = pl.multiple_of(ki * t, t)
        _online_softmax_step(scores(ki, ks), v_ref[pl.ds(ks, t), :], m_ref, l_ref, acc_ref)
        return carry

    lax.fori_loop(0, qi, body, 0)

    ks = pl.multiple_of(qi * t, t)
    row = lax.broadcasted_iota(jnp.int32, (t, t), 0)
    col = lax.broadcasted_iota(jnp.int32, (t, t), 1)
    s = jnp.where(col <= row, scores(qi, ks), -jnp.inf)
    _online_softmax_step(s, v_ref[pl.ds(ks, t), :], m_ref, l_ref, acc_ref)
    o_ref[...] = (acc_ref[...] / l_ref[...]).astype(BF16)


def _fox_prompt_attn(q, k, v, cum_th, cum_ht):
    t_len = q.shape[0]
    t = TQ_FOX
    return pl.pallas_call(
        _fox_prompt_kernel,
        grid=(N_HEADS, t_len // t),
        in_specs=[
            pl.BlockSpec((t, HEAD_DIM), lambda h, i: (i, h)),
            pl.BlockSpec((t_len, HEAD_DIM), lambda h, i: (0, h)),
            pl.BlockSpec((t_len, HEAD_DIM), lambda h, i: (0, h)),
            pl.BlockSpec((t, N_HEADS), lambda h, i: (i, 0)),
            pl.BlockSpec((1, t_len // t, 1, t), lambda h, i: (h, 0, 0, 0)),
        ],
        out_specs=pl.BlockSpec((t, HEAD_DIM), lambda h, i: (i, h)),
        out_shape=jax.ShapeDtypeStruct((t_len, D_MODEL), BF16),
        scratch_shapes=[
            pltpu.VMEM((t, 1), F32),
            pltpu.VMEM((t, 1), F32),
            pltpu.VMEM((t, HEAD_DIM), F32),
        ],
        compiler_params=_cparams(("arbitrary", "arbitrary")),
        name="fox_prompt_attn",
    )(q, k, v, cum_th, cum_ht)


def _fox_sample_kernel(q_ref, kn_ref, vn_ref, kc_ref, vc_ref, cq_ref, ckc_ref, ckn_ref,
                       o_ref, m_ref, l_ref, acc_ref, *, n_new):
    ki = pl.program_id(1)
    nk = pl.num_programs(1)

    @pl.when(ki == 0)
    def _():
        m_ref[...] = jnp.full_like(m_ref, -jnp.inf)
        l_ref[...] = jnp.zeros_like(l_ref)
        acc_ref[...] = jnp.zeros_like(acc_ref)

    for h in range(N_HEADS):
        hs = slice(h * HEAD_DIM, (h + 1) * HEAD_DIM)
        q = q_ref[:, hs]
        cq = cq_ref[0, :, h:h + 1]
        kc = kc_ref[0, :, hs].astype(BF16)
        vc = vc_ref[0, :, hs].astype(BF16)
        s = lax.dot_general(q, kc, (((1,), (1,)), ((), ())), preferred_element_type=F32)
        s = (s + cq) - ckc_ref[0, h:h + 1, :]
        _online_softmax_step(s, vc, m_ref.at[h], l_ref.at[h], acc_ref.at[h])

    @pl.when(ki == nk - 1)
    def _():
        row = lax.broadcasted_iota(jnp.int32, (n_new, n_new), 0)
        col = lax.broadcasted_iota(jnp.int32, (n_new, n_new), 1)
        for h in range(N_HEADS):
            hs = slice(h * HEAD_DIM, (h + 1) * HEAD_DIM)
            q = q_ref[:, hs]
            cq = cq_ref[0, :, h:h + 1]
            s = lax.dot_general(q, kn_ref[:, hs], (((1,), (1,)), ((), ())),
                                preferred_element_type=F32)
            s = (s + cq) - ckn_ref[0, h:h + 1, 0:n_new]
            s = jnp.where(col <= row, s, -jnp.inf)
            _online_softmax_step(s, vn_ref[:, hs], m_ref.at[h], l_ref.at[h], acc_ref.at[h])
            o_ref[:, hs] = (acc_ref[h] / l_ref[h]).astype(BF16)


def _fox_sample_attn(q, kn, vn, cache_k, cache_v, cq_bth, cum_bht, n_streams, n_new):
    past = cache_k.shape[1]
    tk = TK_FOX_S
    return pl.pallas_call(
        functools.partial(_fox_sample_kernel, n_new=n_new),
        grid=(n_streams, past // tk),
        in_specs=[
            pl.BlockSpec((n_new, D_MODEL), lambda b, i: (b, 0)),
            pl.BlockSpec((n_new, D_MODEL), lambda b, i: (b, 0)),
            pl.BlockSpec((n_new, D_MODEL), lambda b, i: (b, 0)),
            pl.BlockSpec((1, tk, D_MODEL), lambda b, i: (b, i, 0)),
            pl.BlockSpec((1, tk, D_MODEL), lambda b, i: (b, i, 0)),
            pl.BlockSpec((1, n_new, N_HEADS), lambda b, i: (b, 0, 0)),
            pl.BlockSpec((1, N_HEADS, tk), lambda b, i: (b, 0, i)),
            pl.BlockSpec((1, N_HEADS, LANES), lambda b, i: (b, 0, past // LANES)),
        ],
        out_specs=pl.BlockSpec((n_new, D_MODEL), lambda b, i: (b, 0)),
        out_shape=jax.ShapeDtypeStruct((n_streams * n_new, D_MODEL), BF16),
        scratch_shapes=[
            pltpu.VMEM((N_HEADS, n_new, 1), F32),
            pltpu.VMEM((N_HEADS, n_new, 1), F32),
            pltpu.VMEM((N_HEADS, n_new, HEAD_DIM), F32),
        ],
        compiler_params=_cparams(("arbitrary", "arbitrary")),
        name="fox_sample_attn",
    )(q, kn, vn, cache_k, cache_v, cq_bth, cum_bht, cum_bht)


def _band_bias(rel_table, q_pos, k_pos):
    rel = np.clip(q_pos[:, None] - k_pos[None, :], -MAX_REL, MAX_REL) + MAX_REL
    qc = q_pos // CHUNK
    kc = k_pos // CHUNK
    ok = (k_pos[None, :] >= 0) & (kc[None, :] <= qc[:, None]) & (kc[None, :] >= qc[:, None] - BAND_CHUNKS)
    bias = rel_table.astype(F32)[:, rel]
    return jnp.where(jnp.asarray(ok)[None], bias, -jnp.inf)


def _softmax_rows(s):
    m = jnp.max(s, axis=-1, keepdims=True)
    p = jnp.exp(s - m)
    return p, jnp.sum(p, axis=-1, keepdims=True)


def _band_prompt_kernel(q_ref, k0_ref, k1_ref, k2_ref, v0_ref, v1_ref, v2_ref, b_ref, o_ref):
    qi = pl.program_id(1)
    t = TQ_BAND
    q = q_ref[...]
    k = jnp.concatenate([k0_ref[...], k1_ref[...], k2_ref[...]], axis=0)
    v = jnp.concatenate([v0_ref[...], v1_ref[...], v2_ref[...]], axis=0)
    s = lax.dot_general(q, k, (((1,), (1,)), ((), ())), preferred_element_type=F32) + b_ref[0]
    col = lax.broadcasted_iota(jnp.int32, s.shape, 1)
    first_valid = (2 - jnp.minimum(qi, 2)) * t
    s = jnp.where(col >= first_valid, s, -jnp.inf)
    p, l = _softmax_rows(s)
    o = jnp.dot(p.astype(BF16), v, preferred_element_type=F32)
    o_ref[...] = (o / l).astype(BF16)


def _band_prompt_attn(q, k, v, bias):
    t_len = q.shape[0]
    t = TQ_BAND
    assert BAND_PAST == 2 * t

    def kv_spec(back):
        return pl.BlockSpec((t, HEAD_DIM), lambda h, i: (jnp.maximum(i - back, 0), h))

    return pl.pallas_call(
        _band_prompt_kernel,
        grid=(N_HEADS, t_len // t),
        in_specs=[
            pl.BlockSpec((t, HEAD_DIM), lambda h, i: (i, h)),
            kv_spec(2), kv_spec(1), kv_spec(0),
            kv_spec(2), kv_spec(1), kv_spec(0),
            pl.BlockSpec((1, t, 3 * t), lambda h, i: (h, 0, 0)),
        ],
        out_specs=pl.BlockSpec((t, HEAD_DIM), lambda h, i: (i, h)),
        out_shape=jax.ShapeDtypeStruct((t_len, D_MODEL), BF16),
        compiler_params=_cparams(("arbitrary", "arbitrary")),
        name="band_prompt_attn",
    )(q, k, k, k, v, v, v, bias)


def _band_sample_kernel(q_ref, kn_ref, vn_ref, kc_ref, vc_ref, b_ref, o_ref, *, n_cache):
    for h in range(N_HEADS):
        hs = slice(h * HEAD_DIM, (h + 1) * HEAD_DIM)
        q = q_ref[:, hs]
        kc = kc_ref[0, :, hs].astype(BF16)
        vc = vc_ref[0, :, hs].astype(BF16)
        dn = (((1,), (1,)), ((), ()))
        s1 = lax.dot_general(q, kc, dn, preferred_element_type=F32) + b_ref[h, :, 0:n_cache]
        s2 = lax.dot_general(q, kn_ref[:, hs], dn, preferred_element_type=F32) + b_ref[h, :, n_cache:]
        m = jnp.maximum(jnp.max(s1, axis=-1, keepdims=True), jnp.max(s2, axis=-1, keepdims=True))
        p1 = jnp.exp(s1 - m)
        p2 = jnp.exp(s2 - m)
        l = jnp.sum(p1, axis=-1, keepdims=True) + jnp.sum(p2, axis=-1, keepdims=True)
        o = jnp.dot(p1.astype(BF16), vc, preferred_element_type=F32)
        o = o + jnp.dot(p2.astype(BF16), vn_ref[:, hs], preferred_element_type=F32)
        o_ref[:, hs] = (o / l).astype(BF16)


def _band_sample_attn(q, kn, vn, cache_k, cache_v, bias, n_streams, n_new):
    n_cache = cache_k.shape[1]
    return pl.pallas_call(
        functools.partial(_band_sample_kernel, n_cache=n_cache),
        grid=(n_streams,),
        in_specs=[
            pl.BlockSpec((n_new, D_MODEL), lambda b: (b, 0)),
            pl.BlockSpec((n_new, D_MODEL), lambda b: (b, 0)),
            pl.BlockSpec((n_new, D_MODEL), lambda b: (b, 0)),
            pl.BlockSpec((1, n_cache, D_MODEL), lambda b: (b, 0, 0)),
            pl.BlockSpec((1, n_cache, D_MODEL), lambda b: (b, 0, 0)),
            pl.BlockSpec((N_HEADS, n_new, n_cache + n_new), lambda b: (0, 0, 0)),
        ],
        out_specs=pl.BlockSpec((n_new, D_MODEL), lambda b: (b, 0)),
        out_shape=jax.ShapeDtypeStruct((n_streams * n_new, D_MODEL), BF16),
        compiler_params=_cparams(("arbitrary",)),
        name="band_sample_attn",
    )(q, kn, vn, cache_k, cache_v, bias)


def _oproj_kernel(x_ref, a_ref, w_ref, o_ref):
    o_ref[...] = x_ref[...] + jnp.dot(a_ref[...], w_ref[...], preferred_element_type=F32)


def _out_proj(x, a, w_o_b):
    m = x.shape[0]
    tm = min(TM, m)
    return pl.pallas_call(
        _oproj_kernel,
        grid=(m // tm,),
        in_specs=[
            pl.BlockSpec((tm, D_MODEL), lambda i: (i, 0)),
            pl.BlockSpec((tm, D_MODEL), lambda i: (i, 0)),
            pl.BlockSpec((D_MODEL, D_MODEL), lambda i: (0, 0)),
        ],
        out_specs=pl.BlockSpec((tm, D_MODEL), lambda i: (i, 0)),
        out_shape=jax.ShapeDtypeStruct((m, D_MODEL), F32),
        compiler_params=_cparams(("arbitrary",)),
        name="out_proj",
    )(x, a, w_o_b)


def _ffn_kernel(*refs, stream_len, tm, final_norm):
    if final_norm:
        (x_ref, g_ref, wg_ref, wv_ref, hg_ref, hv_ref, cwg_ref, cwv_ref, cbg_ref, cbv_ref,
         wo_ref, gf_ref, o_ref, tg_ref, tv_ref, h_ref, pg_ref, pv_ref) = refs
    else:
        (x_ref, g_ref, wg_ref, wv_ref, hg_ref, hv_ref, cwg_ref, cwv_ref, cbg_ref, cbv_ref,
         wo_ref, o_ref, tg_ref, tv_ref, h_ref, pg_ref, pv_ref) = refs
        gf_ref = None
    i = pl.program_id(0)
    f = pl.program_id(1)
    nf = pl.num_programs(1)
    streams_per_tile = max(tm // stream_len, 1)
    tiles_per_stream = max(stream_len // tm, 1)

    @pl.when(f == 0)
    def _():
        h_ref[...] = _rmsnorm_f32(x_ref[...], g_ref[...]).astype(BF16)
        o_ref[...] = x_ref[...]

    h = h_ref[...]
    row = lax.broadcasted_iota(jnp.int32, (tm, TF), 0)
    if streams_per_tile > 1:
        row = row % stream_len

    def conv_half(w_ref, hist_ref, cw_ref, cb_ref, prev_ref, tail_ref):
        u = jnp.dot(h, w_ref[...], preferred_element_type=F32)
        if streams_per_tile > 1:
            def per_row(r):
                hr = hist_ref[:, r:r + 1, :]
                return jnp.broadcast_to(hr, (streams_per_tile, stream_len, TF)).reshape(tm, TF)
            p0, p1 = per_row(0), per_row(1)
            u3 = u.reshape(streams_per_tile, stream_len, TF)
            tail_ref[...] = u3[:, stream_len - SUBLANES:, :]
        else:
            @pl.when((i % tiles_per_stream) == 0)
            def _():
                prev_ref[f, SUBLANES - (CONV_W - 1):SUBLANES, :] = hist_ref[0]
            p0 = prev_ref[f, SUBLANES - 2:SUBLANES - 1, :]
            p1 = prev_ref[f, SUBLANES - 1:SUBLANES, :]
            prev_ref[f] = u[tm - SUBLANES:, :]
            tail_ref[0] = u[tm - SUBLANES:, :]
        u1 = jnp.where(row == 0, p1, pltpu.roll(u, 1, axis=0))
        u2 = jnp.where(row == 0, p0, jnp.where(row == 1, p1, pltpu.roll(u, 2, axis=0)))
        c = cb_ref[...] + cw_ref[0:1, :] * u2
        c = c + cw_ref[1:2, :] * u1
        return c + cw_ref[2:3, :] * u

    cg = conv_half(wg_ref, hg_ref, cwg_ref, cbg_ref, pg_ref, tg_ref)
    cv = conv_half(wv_ref, hv_ref, cwv_ref, cbv_ref, pv_ref, tv_ref)
    a = (cg * jax.nn.sigmoid(cg)) * cv
    o_ref[...] += jnp.dot(a.astype(BF16), wo_ref[...], preferred_element_type=F32)

    if final_norm:
        @pl.when(f == nf - 1)
        def _():
            o_ref[...] = _rmsnorm_f32(o_ref[...], gf_ref[...])


def _conv_ffn(x, g, w_in_g, w_in_v, hist_g, hist_v, cw_g, cw_v, cb_g, cb_v, w_out_b, stream_len,
              g_final=None):
    m = x.shape[0]
    tm = min(TM, m)
    nf = D_FF_PAD // TF
    n_tiles = m // tm
    final_norm = g_final is not None
    if stream_len >= tm:
        assert stream_len % tm == 0
        tiles_per_stream = stream_len // tm
        hist_spec = pl.BlockSpec((1, CONV_W - 1, TF), lambda i, f: (i // tiles_per_stream, 0, f))
        tail_rows = n_tiles
        tail_spec = pl.BlockSpec((1, SUBLANES, TF), lambda i, f: (i, 0, f))
    else:
        assert tm % stream_len == 0 and stream_len % SUBLANES == 0 and n_tiles == 1
        spt = tm // stream_len
        hist_spec = pl.BlockSpec((spt, CONV_W - 1, TF), lambda i, f: (i, 0, f))
        tail_rows = m // stream_len
        tail_spec = pl.BlockSpec((spt, SUBLANES, TF), lambda i, f: (i, 0, f))
    row_tile = pl.BlockSpec((tm, D_MODEL), lambda i, f: (i, 0))
    gain = pl.BlockSpec((1, D_MODEL), lambda i, f: (0, 0))
    w_in_spec = pl.BlockSpec((D_MODEL, TF), lambda i, f: (0, f))
    cw_spec = pl.BlockSpec((CONV_W, TF), lambda i, f: (0, f))
    cb_spec = pl.BlockSpec((1, TF), lambda i, f: (0, f))
    in_specs = [row_tile, gain, w_in_spec, w_in_spec, hist_spec, hist_spec, cw_spec, cw_spec,
                cb_spec, cb_spec, pl.BlockSpec((TF, D_MODEL), lambda i, f: (f, 0))]
    args = [x, g.reshape(1, D_MODEL), w_in_g, w_in_v, hist_g, hist_v, cw_g, cw_v, cb_g, cb_v, w_out_b]
    if final_norm:
        in_specs.append(gain)
        args.append(g_final.reshape(1, D_MODEL))
    tail_shape = jax.ShapeDtypeStruct((tail_rows, SUBLANES, D_FF_PAD), F32)
    return pl.pallas_call(
        functools.partial(_ffn_kernel, stream_len=stream_len, tm=tm, final_norm=final_norm),
        grid=(n_tiles, nf),
        in_specs=in_specs,
        out_specs=[row_tile, tail_spec, tail_spec],
        out_shape=[jax.ShapeDtypeStruct((m, D_MODEL), F32), tail_shape, tail_shape],
        scratch_shapes=[
            pltpu.VMEM((tm, D_MODEL), BF16),
            pltpu.VMEM((nf, SUBLANES, TF), F32),
            pltpu.VMEM((nf, SUBLANES, TF), F32),
        ],
        compiler_params=_cparams(("arbitrary", "arbitrary")),
        name="conv_ffn",
    )(*args)


def _pad_ff(a):
    pad = [(0, 0)] * (a.ndim - 1) + [(0, D_FF_PAD - D_FF)]
    return jnp.pad(a, pad)


def _split_ff(a):
    return _pad_ff(a[..., :D_FF]), _pad_ff(a[..., D_FF:])


def kernel(x_prompt, x_sample, cache_fox_k, cache_fox_v, cache_fox_logf, cache_band_k, cache_band_v,
           state_conv, g_attn, w_qkv, w_fgate, b_fgate, rel_bias, w_o, g_ffn, w_in, conv_w, conv_b,
           w_out, g_final):
    depth = g_attn.shape[0]
    bp, sp, _ = x_prompt.shape
    bs, ss, _ = x_sample.shape
    past = cache_fox_k.shape[2]
    n_band_cache = cache_band_k.shape[2]
    n_keep = min(BAND_PAST, sp)
    assert bp == 1

    xp = x_prompt.reshape(bp * sp, D_MODEL)
    xs = x_sample.reshape(bs * ss, D_MODEL)

    tq = TQ_BAND
    band_q_pos = BAND_PAST + np.arange(tq)
    band_k_pos = np.arange(3 * tq)
    samp_q_pos = past + np.arange(ss)
    samp_k_pos = past - n_band_cache + np.arange(n_band_cache + ss)

    hist0 = jnp.zeros((bp, CONV_W - 1, D_FF_PAD), F32)

    fox_kp, fox_vp, fox_lp, fox_ks, fox_vs, fox_ls = [], [], [], [], [], []
    band_kp, band_vp, band_ks, band_vs = [], [], [], []
    conv_p, conv_s = [], []

    for i in range(depth):
        j = i // N_MIXERS
        is_fox = (i % N_MIXERS == 0)
        wqkv_b = w_qkv[i].astype(BF16)
        wo_b = w_o[i].astype(BF16)
        w_in_g, w_in_v = _split_ff(w_in[i].astype(BF16))
        cw_g, cw_v = _split_ff(conv_w[i])
        cb_g, cb_v = _split_ff(conv_b[i].reshape(1, 2 * D_FF))
        w_out_b = jnp.pad(w_out[i].astype(BF16), ((0, D_FF_PAD - D_FF), (0, 0)))
        hs_g, hs_v = _split_ff(state_conv[i])

        if is_fox:
            wf_b = jnp.pad(w_fgate[j].astype(BF16), ((0, 0), (0, LANES - N_HEADS)))
            bf = jnp.pad(b_fgate[j].reshape(1, N_HEADS), ((0, 0), (0, LANES - N_HEADS)))
            qp, kp32, vp32, kpb, vpb, lfp = _qkv_proj(xp, g_attn[i], wqkv_b, wf_b, bf)
            qs, ks32, vs32, ksb, vsb, lfs = _qkv_proj(xs, g_attn[i], wqkv_b, wf_b, bf)
            lfp = lfp[:, :N_HEADS]
            lfs = lfs[:, :N_HEADS]

            cum_ht = _cumsum_lanes(lfp.T)
            op = _fox_prompt_attn(qp, kpb, vpb, cum_ht.T, cum_ht.reshape(N_HEADS, sp // TQ_FOX, 1, TQ_FOX))

            lf_all = jnp.concatenate([
                jnp.transpose(cache_fox_logf[j], (0, 2, 1)),
                jnp.transpose(lfs.reshape(bs, ss, N_HEADS), (0, 2, 1)),
                jnp.zeros((bs, N_HEADS, LANES - ss), F32),
            ], axis=-1).reshape(bs * N_HEADS, past + LANES)
            cum_s = _cumsum_lanes(lf_all).reshape(bs, N_HEADS, past + LANES)
            cq_s = jnp.transpose(cum_s[:, :, past:past + ss], (0, 2, 1))
            osm = _fox_sample_attn(qs, ksb, vsb,
                                   cache_fox_k[j].reshape(bs, past, D_MODEL),
                                   cache_fox_v[j].reshape(bs, past, D_MODEL),
                                   cq_s, cum_s, bs, ss)

            fox_kp.append(kp32.reshape(bp, sp, N_HEADS, HEAD_DIM))
            fox_vp.append(vp32.reshape(bp, sp, N_HEADS, HEAD_DIM))
            fox_lp.append(lfp.reshape(bp, sp, N_HEADS))
            fox_ks.append(ks32.reshape(bs, ss, N_HEADS, HEAD_DIM))
            fox_vs.append(vs32.reshape(bs, ss, N_HEADS, HEAD_DIM))
            fox_ls.append(lfs.reshape(bs, ss, N_HEADS))
        else:
            qp, kp32, vp32, kpb, vpb = _qkv_proj(xp, g_attn[i], wqkv_b)
            qs, ks32, vs32, ksb, vsb = _qkv_proj(xs, g_attn[i], wqkv_b)
            op = _band_prompt_attn(qp, kpb, vpb, _band_bias(rel_bias[j], band_q_pos, band_k_pos))
            osm = _band_sample_attn(qs, ksb, vsb,
                                    cache_band_k[j].reshape(bs, n_band_cache, D_MODEL),
                                    cache_band_v[j].reshape(bs, n_band_cache, D_MODEL),
                                    _band_bias(rel_bias[j], samp_q_pos, samp_k_pos), bs, ss)
            band_kp.append(kp32[sp - n_keep:].reshape(bp, n_keep, N_HEADS, HEAD_DIM))
            band_vp.append(vp32[sp - n_keep:].reshape(bp, n_keep, N_HEADS, HEAD_DIM))
            band_ks.append(ks32.reshape(bs, ss, N_HEADS, HEAD_DIM))
            band_vs.append(vs32.reshape(bs, ss, N_HEADS, HEAD_DIM))

        xp = _out_proj(xp, op, wo_b)
        xs = _out_proj(xs, osm, wo_b)

        gf = g_final if i == depth - 1 else None
        xp, tgp, tvp = _conv_ffn(xp, g_ffn[i], w_in_g, w_in_v, hist0, hist0, cw_g, cw_v, cb_g, cb_v,
                                 w_out_b, sp, gf)
        xs, tgs, tvs = _conv_ffn(xs, g_ffn[i], w_in_g, w_in_v, hs_g, hs_v, cw_g, cw_v, cb_g, cb_v,
                                 w_out_b, ss, gf)
        keep = slice(SUBLANES - (CONV_W - 1), SUBLANES)
        conv_p.append(jnp.concatenate([tgp[-1:, keep, :D_FF], tvp[-1:, keep, :D_FF]], axis=-1))
        conv_s.append(jnp.concatenate([tgs[:, keep, :D_FF], tvs[:, keep, :D_FF]], axis=-1))

    y_prompt = xp.reshape(bp, sp, D_MODEL)
    y_sample = xs.reshape(bs, ss, D_MODEL)
    return (y_prompt, y_sample,
            jnp.stack(fox_kp), jnp.stack(fox_vp), jnp.stack(fox_lp),
            jnp.stack(fox_ks), jnp.stack(fox_vs), jnp.stack(fox_ls),
            jnp.stack(band_kp), jnp.stack(band_vp), jnp.stack(band_ks), jnp.stack(band_vs),
            jnp.stack(conv_p), jnp.stack(conv_s))
```

```python
import functools
import math

import numpy as np
import jax
import jax.numpy as jnp
from jax import lax
from jax.experimental import pallas as pl
from jax.experimental.pallas import tpu as pltpu

F32 = jnp.float32
BF16 = jnp.bfloat16

D_MODEL = 2048
N_HEADS = 16
HEAD_DIM = D_MODEL // N_HEADS
SCALE = 1.0 / math.sqrt(HEAD_DIM)
LOG2E = math.log2(math.e)
CHUNK = 64
BAND_CHUNKS = 8
BAND_PAST = BAND_CHUNKS * CHUNK
MAX_REL = 2 * CHUNK
D_FF = 5504
CONV_W = 3
EPS = 1e-6
N_MIXERS = 2

LANES = 128
SUBLANES = 8
VMEM_LIMIT = 56 * 1024 * 1024

TM = 512
TN_QKV = 512
TF = 512
FFN_ROW_GROUPS = 2
D_FF_PAD = ((D_FF + TF - 1) // TF) * TF
TK_FOX = 512
TQ_FOX = 2 * TK_FOX
TK_FOX_S = 512
TQ_BAND = 256
CUM_BLK = 128


def _cparams(sem):
    return pltpu.CompilerParams(dimension_semantics=sem, vmem_limit_bytes=VMEM_LIMIT)


def _rmsnorm_f32(x, g):
    ms = jnp.mean(x * x, axis=-1, keepdims=True)
    return (x * lax.rsqrt(ms + EPS)) * g


def _qkv_kernel(*refs, with_fgate, n_aliased, q_scale):
    refs = refs[:5 + 2 * with_fgate] + refs[5 + 2 * with_fgate + n_aliased:]
    if with_fgate:
        (x_ref, g_ref, wq_ref, wk_ref, wv_ref, wf_ref, bf_ref,
         q_ref, k32_ref, v32_ref, kb_ref, vb_ref, lf_ref, h_ref) = refs
    else:
        (x_ref, g_ref, wq_ref, wk_ref, wv_ref,
         q_ref, k32_ref, v32_ref, kb_ref, vb_ref, h_ref) = refs

    @pl.when(pl.program_id(1) == 0)
    def _():
        hb = _rmsnorm_f32(x_ref[...], g_ref[...]).astype(BF16)
        h_ref[...] = hb
        if with_fgate:
            z = jnp.dot(hb, wf_ref[...], preferred_element_type=F32) + bf_ref[...]
            lf_ref[...] = jnp.minimum(z, 0.0) - jnp.log1p(jnp.exp(-jnp.abs(z)))

    h = h_ref[...]
    q = jnp.dot(h, wq_ref[...], preferred_element_type=F32)
    q_ref[...] = (q * q_scale).astype(BF16)
    k = jnp.dot(h, wk_ref[...], preferred_element_type=F32)
    k32_ref[...] = k
    kb_ref[...] = k.astype(BF16)
    v = jnp.dot(h, wv_ref[...], preferred_element_type=F32)
    v32_ref[...] = v
    vb_ref[...] = v.astype(BF16)


def _qkv_proj(x, g, w_qkv_b, wf_b=None, bf=None, slot=0, n_slots=1, kv_stack=None, q_scale=SCALE):
    m = x.shape[0]
    tm = min(TM, m)
    nn = D_MODEL // TN_QKV
    with_fgate = wf_b is not None
    stack_tile = pl.BlockSpec((None, tm, TN_QKV), lambda i, n: (slot, i, n))
    stack_shape = jax.ShapeDtypeStruct((n_slots, m, D_MODEL), F32)
    in_specs = [
        pl.BlockSpec((tm, D_MODEL), lambda i, n: (i, 0)),
        pl.BlockSpec((1, D_MODEL), lambda i, n: (0, 0)),
        pl.BlockSpec((D_MODEL, TN_QKV), lambda i, n: (0, n)),
        pl.BlockSpec((D_MODEL, TN_QKV), lambda i, n: (0, n + nn)),
        pl.BlockSpec((D_MODEL, TN_QKV), lambda i, n: (0, n + 2 * nn)),
    ]
    args = [x, g.reshape(1, D_MODEL), w_qkv_b, w_qkv_b, w_qkv_b]
    tile = pl.BlockSpec((tm, TN_QKV), lambda i, n: (i, n))
    out_specs = [tile, stack_tile, stack_tile, tile, tile]
    out_shape = [
        jax.ShapeDtypeStruct((m, D_MODEL), BF16),
        stack_shape,
        stack_shape,
        jax.ShapeDtypeStruct((m, D_MODEL), BF16),
        jax.ShapeDtypeStruct((m, D_MODEL), BF16),
    ]
    if with_fgate:
        in_specs += [
            pl.BlockSpec((D_MODEL, LANES), lambda i, n: (0, 0)),
            pl.BlockSpec((1, LANES), lambda i, n: (0, 0)),
        ]
        args += [wf_b, bf]
        out_specs.append(pl.BlockSpec((tm, LANES), lambda i, n: (i, 0)))
        out_shape.append(jax.ShapeDtypeStruct((m, LANES), F32))
    aliases = {}
    if kv_stack is not None:
        aliases = {len(args): 1, len(args) + 1: 2}
        in_specs += [pl.BlockSpec(memory_space=pl.ANY)] * 2
        args += list(kv_stack)
    return pl.pallas_call(
        functools.partial(_qkv_kernel, with_fgate=with_fgate, n_aliased=len(aliases),
                          q_scale=q_scale),
        grid=(m // tm, nn),
        in_specs=in_specs,
        out_specs=out_specs,
        out_shape=out_shape,
        scratch_shapes=[pltpu.VMEM((tm, D_MODEL), BF16)],
        input_output_aliases=aliases,
        compiler_params=_cparams(("arbitrary", "arbitrary")),
        name="qkv_proj",
    )(*args)


def _cumsum_kernel(x_ref, o_ref, carry_ref):
    @pl.when(pl.program_id(0) == 0)
    def _():
        carry_ref[...] = jnp.zeros_like(carry_ref)

    x = x_ref[...]
    hi = x.astype(BF16)
    r1 = x - hi.astype(F32)
    mid = r1.astype(BF16)
    lo = (r1 - mid.astype(F32)).astype(BF16)
    row = lax.broadcasted_iota(jnp.int32, (CUM_BLK, CUM_BLK), 0)
    col = lax.broadcasted_iota(jnp.int32, (CUM_BLK, CUM_BLK), 1)
    tri = (row <= col).astype(BF16)
    s = jnp.dot(lo, tri, preferred_element_type=F32)
    s = s + jnp.dot(mid, tri, preferred_element_type=F32)
    s = s + jnp.dot(hi, tri, preferred_element_type=F32)
    s = s + carry_ref[...]
    o_ref[...] = s
    carry_ref[...] = jnp.broadcast_to(s[:, CUM_BLK - 1:CUM_BLK], s.shape)


def _cumsum_lanes(x):
    r, t = x.shape
    return pl.pallas_call(
        _cumsum_kernel,
        grid=(t // CUM_BLK,),
        in_specs=[pl.BlockSpec((r, CUM_BLK), lambda i: (0, i))],
        out_specs=pl.BlockSpec((r, CUM_BLK), lambda i: (0, i)),
        out_shape=jax.ShapeDtypeStruct((r, t), F32),
        scratch_shapes=[pltpu.VMEM((r, CUM_BLK), F32)],
        compiler_params=_cparams(("arbitrary",)),
        name="cumsum_time",
    )(x)


def _online_softmax_step(s, v, m_ref, l_ref, acc_ref):
    m_prev = m_ref[...]
    m_new = jnp.maximum(m_prev, jnp.max(s, axis=-1, keepdims=True))
    alpha = jnp.exp(m_prev - m_new)
    p = jnp.exp(s - m_new)
    l_ref[...] = alpha * l_ref[...] + jnp.sum(p, axis=-1, keepdims=True)
    acc_ref[...] = alpha * acc_ref[...] + jnp.dot(p.astype(BF16), v, preferred_element_type=F32)
    m_ref[...] = m_new


def _split3_bf16(x):
    hi = x.astype(BF16)
    r1 = x - hi.astype(F32)
    mid = r1.astype(BF16)
    lo = (r1 - mid.astype(F32)).astype(BF16)
    return jnp.concatenate([hi, mid, lo], axis=-1)


def _fox_aug_placement():
    pq = np.zeros((3 * LANES, D_MODEL), np.float32)
    pk = np.zeros((3 * LANES, D_MODEL), np.float32)
    cq = np.zeros((1, D_MODEL), np.float32)
    ck = np.zeros((1, D_MODEL), np.float32)
    for h in range(N_HEADS):
        for piece in range(3):
            pq[piece * LANES + h, h * HEAD_DIM + piece] = 1.0
            pk[piece * LANES + h, h * HEAD_DIM + 3 + piece] = -1.0
            cq[0, h * HEAD_DIM + 3 + piece] = 1.0
            ck[0, h * HEAD_DIM + piece] = 1.0
    return (jnp.asarray(pq, BF16), jnp.asarray(pk, BF16), jnp.asarray(cq), jnp.asarray(ck))


def _fox_aug_kernel(x_ref, pq_ref, pk_ref, cq_ref, ck_ref, aq_ref, ak_ref, carry_ref):
    @pl.when(pl.program_id(0) == 0)
    def _():
        carry_ref[...] = jnp.zeros_like(carry_ref)

    tb = x_ref.shape[0]
    row = lax.broadcasted_iota(jnp.int32, (tb, tb), 0)
    col = lax.broadcasted_iota(jnp.int32, (tb, tb), 1)
    tri = (col <= row).astype(BF16)
    x3 = _split3_bf16(x_ref[...])
    c3 = jnp.dot(tri, x3, preferred_element_type=F32)
    cum = (c3[:, 2 * LANES:] + c3[:, LANES:2 * LANES]) + c3[:, :LANES] + carry_ref[...]
    carry_ref[...] = cum[tb - 1:tb, :]
    cum3 = _split3_bf16(cum)
    aq = jnp.dot(cum3, pq_ref[...], preferred_element_type=F32) + cq_ref[...]
    ak = jnp.dot(cum3, pk_ref[...], preferred_element_type=F32) + ck_ref[...]
    aq_ref[...] = aq.astype(BF16)
    ak_ref[...] = ak.astype(BF16)


def _fox_aug(logf_pad):
    t_len = logf_pad.shape[0]
    tb = 2 * LANES
    pq, pk, cq, ck = _fox_aug_placement()
    const = lambda shape: pl.BlockSpec(shape, lambda i: (0, 0))
    return pl.pallas_call(
        _fox_aug_kernel,
        grid=(t_len // tb,),
        in_specs=[
            pl.BlockSpec((tb, LANES), lambda i: (i, 0)),
            const((3 * LANES, D_MODEL)), const((3 * LANES, D_MODEL)),
            const((1, D_MODEL)), const((1, D_MODEL)),
        ],
        out_specs=[pl.BlockSpec((tb, D_MODEL), lambda i: (i, 0))] * 2,
        out_shape=[jax.ShapeDtypeStruct((t_len, D_MODEL), BF16)] * 2,
        scratch_shapes=[pltpu.VMEM((1, LANES), F32)],
        compiler_params=_cparams(("arbitrary",)),
        name="fox_aug",
    )(logf_pad, pq, pk, cq, ck)


def _fox_prompt_kernel(q_ref, aq_ref, k_ref, ak_ref, v_ref, o_ref,
                       s0_ref, s1_ref, p0_ref, p1_ref, a0_ref, a1_ref, m_ref, acc_ref):
    qi = pl.program_id(1)
    tq, tk = TQ_FOX, TK_FOX
    dn = (((1,), (1,)), ((), ()))
    q = jnp.concatenate([q_ref[...], aq_ref[...]], axis=1)
    ones_col = (lax.broadcasted_iota(jnp.int32, (tk, HEAD_DIM), 1) == 0).astype(BF16)

    def scores(ki, s_ref):
        ks = pl.multiple_of(ki * tk, tk)
        k = jnp.concatenate([k_ref[pl.ds(ks, tk), :], ak_ref[pl.ds(ks, tk), :]], axis=1)
        s_ref[...] = lax.dot_general(q, k, dn, preferred_element_type=F32)

    def accumulate(ki, p_ref, a_ref):
        ks = pl.multiple_of(ki * tk, tk)
        v = jnp.concatenate([v_ref[pl.ds(ks, tk), :], ones_col], axis=1)
        pv = jnp.dot(p_ref[...], v, preferred_element_type=F32)
        acc_ref[...] = a_ref[...] * acc_ref[...] + pv

    def softmax(s, p_ref, a_ref):
        m_prev = m_ref[...]
        m_new = jnp.maximum(m_prev, jnp.max(s, axis=-1, keepdims=True))
        a_ref[...] = jnp.exp(m_prev - m_new)
        p_ref[...] = jnp.exp(s - m_new).astype(BF16)
        m_ref[...] = m_new

    m_ref[...] = jnp.full_like(m_ref, -jnp.inf)
    acc_ref[...] = jnp.zeros_like(acc_ref)
    p1_ref[...] = jnp.zeros_like(p1_ref)
    a1_ref[...] = jnp.ones_like(a1_ref)
    scores(0, s0_ref)

    def body(j, carry):
        scores(2 * j + 1, s1_ref)
        accumulate(jnp.maximum(2 * j - 1, 0), p1_ref, a1_ref)
        softmax(s0_ref[...], p0_ref, a0_ref)
        scores(2 * j + 2, s0_ref)
        accumulate(2 * j, p0_ref, a0_ref)
        softmax(s1_ref[...], p1_ref, a1_ref)
        return carry

    lax.fori_loop(0, qi, body, 0)

    row = lax.broadcasted_iota(jnp.int32, (tq, tk), 0)
    col = lax.broadcasted_iota(jnp.int32, (tq, tk), 1)
    scores(2 * qi + 1, s1_ref)
    accumulate(jnp.maximum(2 * qi - 1, 0), p1_ref, a1_ref)
    softmax(jnp.where(col <= row, s0_ref[...], -jnp.inf), p0_ref, a0_ref)
    accumulate(2 * qi, p0_ref, a0_ref)
    softmax(jnp.where(col + tk <= row, s1_ref[...], -jnp.inf), p1_ref, a1_ref)
    accumulate(2 * qi + 1, p1_ref, a1_ref)
    acc = acc_ref[...]
    o_ref[...] = (acc[:, :HEAD_DIM] / acc[:, HEAD_DIM:HEAD_DIM + 1]).astype(BF16)


def _fox_prompt_attn(q, aq, k, ak, v):
    t_len = q.shape[0]
    tq, tk = TQ_FOX, TK_FOX
    assert tq == 2 * tk and t_len % tq == 0
    tile = pl.BlockSpec((tq, HEAD_DIM), lambda h, i: (i, h))
    whole = pl.BlockSpec((t_len, HEAD_DIM), lambda h, i: (0, h))
    return pl.pallas_call(
        _fox_prompt_kernel,
        grid=(N_HEADS, t_len // tq),
        in_specs=[tile, tile, whole, whole, whole],
        out_specs=tile,
        out_shape=jax.ShapeDtypeStruct((t_len, D_MODEL), BF16),
        scratch_shapes=[
            pltpu.VMEM((tq, tk), F32), pltpu.VMEM((tq, tk), F32),
            pltpu.VMEM((tq, tk), BF16), pltpu.VMEM((tq, tk), BF16),
            pltpu.VMEM((tq, 1), F32), pltpu.VMEM((tq, 1), F32),
            pltpu.VMEM((tq, 1), F32),
            pltpu.VMEM((tq, 2 * HEAD_DIM), F32),
        ],
        compiler_params=_cparams(("arbitrary", "arbitrary")),
        name="fox_prompt_attn",
    )(q, aq, k, ak, v)


def _fox_sample_kernel(q_ref, kn_ref, vn_ref, kc_ref, vc_ref, cq_ref, ckc_ref, ckn_ref,
                       o_ref, s_ref, sn_ref, m_ref, acc_ref, *, n_new, tk):
    ki = pl.program_id(1)
    nk = pl.num_programs(1)
    dn = (((1,), (1,)), ((), ()))

    @pl.when(ki == 0)
    def _():
        m_ref[...] = jnp.full_like(m_ref, -jnp.inf)
        acc_ref[...] = jnp.zeros_like(acc_ref)

    def head_rows(h):
        return slice(h * n_new, (h + 1) * n_new)

    def head_cols(h):
        return slice(h * HEAD_DIM, (h + 1) * HEAD_DIM)

    def softmax_update(s):
        m_prev = m_ref[...]
        m_new = jnp.maximum(m_prev, jnp.max(s, axis=-1, keepdims=True))
        m_ref[...] = m_new
        return jnp.exp(m_prev - m_new), jnp.exp(s - m_new).astype(BF16)

    def accumulate(alpha, p, values):
        n_keys = p.shape[1]
        ones_col = (lax.broadcasted_iota(jnp.int32, (n_keys, HEAD_DIM), 1) == 0).astype(BF16)
        for h in range(N_HEADS):
            r = head_rows(h)
            pv = jnp.dot(p[r], jnp.concatenate([values(h), ones_col], axis=1),
                         preferred_element_type=F32)
            acc_ref[r, :] = alpha[r] * acc_ref[r, :] + pv

    for h in range(N_HEADS):
        kc = kc_ref[0, pl.ds(h, tk, stride=N_HEADS), :].astype(BF16)
        s = lax.dot_general(q_ref[:, head_cols(h)], kc, dn, preferred_element_type=F32)
        s_ref[head_rows(h), :] = (s + cq_ref[0, :, h:h + 1]) - ckc_ref[0, h:h + 1, :]
    alpha, p = softmax_update(s_ref[...])
    accumulate(alpha, p, lambda h: vc_ref[0, pl.ds(h, tk, stride=N_HEADS), :].astype(BF16))

    @pl.when(ki == nk - 1)
    def _():
        row = lax.broadcasted_iota(jnp.int32, (n_new, n_new), 0)
        col = lax.broadcasted_iota(jnp.int32, (n_new, n_new), 1)
        for h in range(N_HEADS):
            s = lax.dot_general(q_ref[:, head_cols(h)], kn_ref[:, head_cols(h)], dn,
                                preferred_element_type=F32)
            s = (s + cq_ref[0, :, h:h + 1]) - ckn_ref[0, h:h + 1, 0:n_new]
            sn_ref[head_rows(h), :] = jnp.where(col <= row, s, -jnp.inf)
        alpha, p = softmax_update(sn_ref[...])
        accumulate(alpha, p, lambda h: vn_ref[:, head_cols(h)])
        for h in range(N_HEADS):
            acc = acc_ref[head_rows(h), :]
            o_ref[:, head_cols(h)] = (acc[:, :HEAD_DIM] / acc[:, HEAD_DIM:HEAD_DIM + 1]).astype(BF16)


def _fox_sample_attn(q, kn, vn, cache_k, cache_v, layer, cq_bth, cum_bht, n_streams, n_new):
    past = cache_k.shape[1] // N_HEADS
    tk = TK_FOX_S
    first = layer * n_streams
    rows = N_HEADS * n_new
    return pl.pallas_call(
        functools.partial(_fox_sample_kernel, n_new=n_new, tk=tk),
        grid=(n_streams, past // tk),
        in_specs=[
            pl.BlockSpec((n_new, D_MODEL), lambda b, i: (b, 0)),
            pl.BlockSpec((n_new, D_MODEL), lambda b, i: (b, 0)),
            pl.BlockSpec((n_new, D_MODEL), lambda b, i: (b, 0)),
            pl.BlockSpec((1, tk * N_HEADS, HEAD_DIM), lambda b, i: (first + b, i, 0)),
            pl.BlockSpec((1, tk * N_HEADS, HEAD_DIM), lambda b, i: (first + b, i, 0)),
            pl.BlockSpec((1, n_new, N_HEADS), lambda b, i: (b, 0, 0)),
            pl.BlockSpec((1, N_HEADS, tk), lambda b, i: (b, 0, i)),
            pl.BlockSpec((1, N_HEADS, LANES), lambda b, i: (b, 0, past // LANES)),
        ],
        out_specs=pl.BlockSpec((n_new, D_MODEL), lambda b, i: (b, 0)),
        out_shape=jax.ShapeDtypeStruct((n_streams * n_new, D_MODEL), BF16),
        scratch_shapes=[
            pltpu.VMEM((rows, tk), F32),
            pltpu.VMEM((rows, n_new), F32),
            pltpu.VMEM((rows, 1), F32),
            pltpu.VMEM((rows, 2 * HEAD_DIM), F32),
        ],
        compiler_params=_cparams(("arbitrary", "arbitrary")),
        name="fox_sample_attn",
    )(q, kn, vn, cache_k, cache_v, cq_bth, cum_bht, cum_bht)


def _band_bias(rel_table, q_pos, k_pos):
    rel = np.clip(q_pos[:, None] - k_pos[None, :], -MAX_REL, MAX_REL) + MAX_REL
    qc = q_pos // CHUNK
    kc = k_pos // CHUNK
    ok = (k_pos[None, :] >= 0) & (kc[None, :] <= qc[:, None]) & (kc[None, :] >= qc[:, None] - BAND_CHUNKS)
    bias = rel_table.astype(F32)[:, rel]
    return jnp.where(jnp.asarray(ok)[None], bias, -jnp.inf)


def _softmax_rows(s):
    m = jnp.max(s, axis=-1, keepdims=True)
    p = jnp.exp(s - m)
    return p, jnp.sum(p, axis=-1, keepdims=True)


def _band_prompt_kernel(q_ref, kp_ref, kc_ref, vp_ref, vc_ref, g_ref, o_ref, b_ref):
    qi = pl.program_id(1)
    t = TQ_BAND

    @pl.when(qi == 0)
    def _():
        n = g_ref.shape[-1]
        g = jnp.broadcast_to(g_ref[0], (t, n))
        r = pltpu.roll(g, n - (t - 1), axis=1, stride=1, stride_axis=0)
        qc = lax.broadcasted_iota(jnp.int32, (t, 3 * t), 0) // CHUNK
        kc = lax.broadcasted_iota(jnp.int32, (t, 3 * t), 1) // CHUNK
        ok = (kc >= qc) & (kc <= qc + BAND_CHUNKS)
        b_ref[...] = jnp.where(ok, r[:, :3 * t], -jnp.inf)

    k = jnp.concatenate([kp_ref[...], kc_ref[...]], axis=0)
    v = jnp.concatenate([vp_ref[...], vc_ref[...]], axis=0)
    col = lax.broadcasted_iota(jnp.int32, (t, 3 * t), 1)
    for sub in range(BAND_PAST // t):
        lo = sub * t
        q = q_ref[lo:lo + t, :]
        s = lax.dot_general(q, k[lo:lo + 3 * t], (((1,), (1,)), ((), ())),
                            preferred_element_type=F32) + b_ref[...]
        first_valid = jnp.where(qi == 0, BAND_PAST - lo, 0)
        s = jnp.where(col >= first_valid, s, -jnp.inf)
        p, l = _softmax_rows(s)
        o = jnp.dot(p.astype(BF16), v[lo:lo + 3 * t], preferred_element_type=F32)
        o_ref[lo:lo + t, :] = (o / l).astype(BF16)


def _band_profile(rel_table):
    t = TQ_BAND
    dist = BAND_PAST + (t - 1) - np.arange(4 * t)
    idx = np.clip(dist, -MAX_REL, MAX_REL) + MAX_REL
    return rel_table.astype(F32)[:, idx].reshape(N_HEADS, 1, 4 * t)


def _band_prompt_attn(q, k, v, profile):
    t_len = q.shape[0]
    t = TQ_BAND
    tb = BAND_PAST
    assert tb == 2 * t and t % CHUNK == 0 and t_len % tb == 0
    cur = pl.BlockSpec((tb, HEAD_DIM), lambda h, i: (i, h))
    prev = pl.BlockSpec((tb, HEAD_DIM), lambda h, i: (jnp.maximum(i - 1, 0), h))
    return pl.pallas_call(
        _band_prompt_kernel,
        grid=(N_HEADS, t_len // tb),
        in_specs=[cur, prev, cur, prev, cur,
                  pl.BlockSpec((1, 1, 4 * t), lambda h, i: (h, 0, 0))],
        out_specs=cur,
        out_shape=jax.ShapeDtypeStruct((t_len, D_MODEL), BF16),
        scratch_shapes=[pltpu.VMEM((t, 3 * t), F32)],
        compiler_params=_cparams(("arbitrary", "arbitrary")),
        name="band_prompt_attn",
    )(q, k, k, v, v, profile)


def _band_sample_kernel(q_ref, kn_ref, vn_ref, kc_ref, vc_ref, b_ref, o_ref, *, n_cache):
    for h in range(N_HEADS):
        hs = slice(h * HEAD_DIM, (h + 1) * HEAD_DIM)
        q = q_ref[:, hs]
        kc = kc_ref[0, :, hs].astype(BF16)
        vc = vc_ref[0, :, hs].astype(BF16)
        dn = (((1,), (1,)), ((), ()))
        s1 = lax.dot_general(q, kc, dn, preferred_element_type=F32) + b_ref[h, :, 0:n_cache]
        s2 = lax.dot_general(q, kn_ref[:, hs], dn, preferred_element_type=F32) + b_ref[h, :, n_cache:]
        m = jnp.maximum(jnp.max(s1, axis=-1, keepdims=True), jnp.max(s2, axis=-1, keepdims=True))
        p1 = jnp.exp(s1 - m)
        p2 = jnp.exp(s2 - m)
        l = jnp.sum(p1, axis=-1, keepdims=True) + jnp.sum(p2, axis=-1, keepdims=True)
        o = jnp.dot(p1.astype(BF16), vc, preferred_element_type=F32)
        o = o + jnp.dot(p2.astype(BF16), vn_ref[:, hs], preferred_element_type=F32)
        o_ref[:, hs] = (o / l).astype(BF16)


def _band_sample_attn(q, kn, vn, cache_k, cache_v, layer, bias, n_streams, n_new):
    n_cache = cache_k.shape[1]
    first = layer * n_streams
    return pl.pallas_call(
        functools.partial(_band_sample_kernel, n_cache=n_cache),
        grid=(n_streams,),
        in_specs=[
            pl.BlockSpec((n_new, D_MODEL), lambda b: (b, 0)),
            pl.BlockSpec((n_new, D_MODEL), lambda b: (b, 0)),
            pl.BlockSpec((n_new, D_MODEL), lambda b: (b, 0)),
            pl.BlockSpec((1, n_cache, D_MODEL), lambda b: (first + b, 0, 0)),
            pl.BlockSpec((1, n_cache, D_MODEL), lambda b: (first + b, 0, 0)),
            pl.BlockSpec((N_HEADS, n_new, n_cache + n_new), lambda b: (0, 0, 0)),
        ],
        out_specs=pl.BlockSpec((n_new, D_MODEL), lambda b: (b, 0)),
        out_shape=jax.ShapeDtypeStruct((n_streams * n_new, D_MODEL), BF16),
        compiler_params=_cparams(("arbitrary",)),
        name="band_sample_attn",
    )(q, kn, vn, cache_k, cache_v, bias)


def _oproj_kernel(x_ref, a_ref, w_ref, o_ref):
    o_ref[...] = x_ref[...] + jnp.dot(a_ref[...], w_ref[...], preferred_element_type=F32)


def _out_proj(x, a, w_o_b):
    m = x.shape[0]
    tm = min(TM, m)
    return pl.pallas_call(
        _oproj_kernel,
        grid=(m // tm,),
        in_specs=[
            pl.BlockSpec((tm, D_MODEL), lambda i: (i, 0)),
            pl.BlockSpec((tm, D_MODEL), lambda i: (i, 0)),
            pl.BlockSpec((D_MODEL, D_MODEL), lambda i: (0, 0)),
        ],
        out_specs=pl.BlockSpec((tm, D_MODEL), lambda i: (i, 0)),
        out_shape=jax.ShapeDtypeStruct((m, D_MODEL), F32),
        compiler_params=_cparams(("arbitrary",)),
        name="out_proj",
    )(x, a, w_o_b)


def _ffn_kernel(*refs, stream_len, tm, final_norm):
    if final_norm:
        (x_ref, g_ref, wg_ref, wv_ref, hg_ref, hv_ref, cwg_ref, cwv_ref, cbg_ref, cbv_ref,
         wo_ref, gf_ref, o_ref, tg_ref, tv_ref, h_ref, pg_ref, pv_ref, u_ref) = refs
    else:
        (x_ref, g_ref, wg_ref, wv_ref, hg_ref, hv_ref, cwg_ref, cwv_ref, cbg_ref, cbv_ref,
         wo_ref, o_ref, tg_ref, tv_ref, h_ref, pg_ref, pv_ref, u_ref) = refs
        gf_ref = None
    i = pl.program_id(0)
    f = pl.program_id(1)
    nf = pl.num_programs(1)
    streams_per_tile = max(tm // stream_len, 1)
    tiles_per_stream = max(stream_len // tm, 1)

    @pl.when(f == 0)
    def _():
        h_ref[...] = _rmsnorm_f32(x_ref[...], g_ref[...]).astype(BF16)
        o_ref[...] = x_ref[...]

    n_groups = FFN_ROW_GROUPS
    rg = tm // n_groups
    assert rg % SUBLANES == 0 and (streams_per_tile == 1 or streams_per_tile % n_groups == 0)
    row = lax.broadcasted_iota(jnp.int32, (rg, TF), 0)
    if streams_per_tile > 1:
        row = row % stream_len
    halves = ((wg_ref, hg_ref, cwg_ref, cbg_ref, pg_ref, tg_ref),
              (wv_ref, hv_ref, cwv_ref, cbv_ref, pv_ref, tv_ref))

    class _Up:
        def __init__(self, g, half):
            self.ref = u_ref.at[half * n_groups + g]
            self.ref[...] = jnp.dot(h_ref[g * rg:(g + 1) * rg, :], halves[half][0][...],
                                    preferred_element_type=F32)

        def __getitem__(self, idx):
            return self.ref[idx]

        def reshape(self, *shape):
            return self.ref[...].reshape(*shape)

    if streams_per_tile == 1:
        @pl.when((i % tiles_per_stream) == 0)
        def _():
            for (_, hist_ref, _, _, prev_ref, _) in halves:
                prev_ref[f, SUBLANES - (CONV_W - 1):SUBLANES, :] = hist_ref[0]

    ups = [[_Up(0, half) for half in range(2)]]

    def conv(g, half):
        _, hist_ref, cw_ref, cb_ref, prev_ref, _ = halves[half]
        u = ups[g][half][...]
        if streams_per_tile > 1:
            spg = streams_per_tile // n_groups
            def per_row(r):
                hr = hist_ref[g * spg:(g + 1) * spg, r:r + 1, :]
                return jnp.broadcast_to(hr, (spg, stream_len, TF)).reshape(rg, TF)
            p0, p1 = per_row(0), per_row(1)
        elif g == 0:
            p0 = prev_ref[f, SUBLANES - 2:SUBLANES - 1, :]
            p1 = prev_ref[f, SUBLANES - 1:SUBLANES, :]
        else:
            p0 = ups[g - 1][half][rg - 2:rg - 1, :]
            p1 = ups[g - 1][half][rg - 1:rg, :]
        u1 = jnp.where(row == 0, p1, pltpu.roll(u, 1, axis=0))
        u2 = jnp.where(row == 0, p0, jnp.where(row == 1, p1, pltpu.roll(u, 2, axis=0)))
        c = cb_ref[...] + cw_ref[0:1, :] * u2
        c = c + cw_ref[1:2, :] * u1
        return c + cw_ref[2:3, :] * u

    for g in range(n_groups):
        cg = conv(g, 0)
        cv = conv(g, 1)
        a = ((cg * jax.nn.sigmoid(cg)) * cv).astype(BF16)
        if g + 1 < n_groups:
            ups.append([_Up(g + 1, half) for half in range(2)])
        o_ref[g * rg:(g + 1) * rg, :] += jnp.dot(a, wo_ref[...], preferred_element_type=F32)

    for half, (_, _, _, _, prev_ref, tail_ref) in enumerate(halves):
        if streams_per_tile > 1:
            spg = streams_per_tile // n_groups
            for g in range(n_groups):
                u3 = ups[g][half].reshape(spg, stream_len, TF)
                tail_ref[g * spg:(g + 1) * spg] = u3[:, stream_len - SUBLANES:, :]
        else:
            last = ups[n_groups - 1][half][rg - SUBLANES:, :]
            prev_ref[f] = last
            tail_ref[0] = last

    if final_norm:
        @pl.when(f == nf - 1)
        def _():
            o_ref[...] = _rmsnorm_f32(o_ref[...], gf_ref[...])


def _conv_ffn(x, g, w_in_g, w_in_v, hist_g, hist_v, cw_g, cw_v, cb_g, cb_v, w_out_b, stream_len,
              g_final=None):
    m = x.shape[0]
    tm = min(TM, m)
    nf = D_FF_PAD // TF
    n_tiles = m // tm
    final_norm = g_final is not None
    if stream_len >= tm:
        assert stream_len % tm == 0
        tiles_per_stream = stream_len // tm
        hist_spec = pl.BlockSpec((1, CONV_W - 1, TF), lambda i, f: (i // tiles_per_stream, 0, f))
        tail_rows = n_tiles
        tail_spec = pl.BlockSpec((1, SUBLANES, TF), lambda i, f: (i, 0, f))
    else:
        assert tm % stream_len == 0 and stream_len % SUBLANES == 0 and n_tiles == 1
        spt = tm // stream_len
        hist_spec = pl.BlockSpec((spt, CONV_W - 1, TF), lambda i, f: (i, 0, f))
        tail_rows = m // stream_len
        tail_spec = pl.BlockSpec((spt, SUBLANES, TF), lambda i, f: (i, 0, f))
    row_tile = pl.BlockSpec((tm, D_MODEL), lambda i, f: (i, 0))
    gain = pl.BlockSpec((1, D_MODEL), lambda i, f: (0, 0))
    w_in_spec = pl.BlockSpec((D_MODEL, TF), lambda i, f: (0, f))
    cw_spec = pl.BlockSpec((CONV_W, TF), lambda i, f: (0, f))
    cb_spec = pl.BlockSpec((1, TF), lambda i, f: (0, f))
    in_specs = [row_tile, gain, w_in_spec, w_in_spec, hist_spec, hist_spec, cw_spec, cw_spec,
                cb_spec, cb_spec, pl.BlockSpec((TF, D_MODEL), lambda i, f: (f, 0))]
    args = [x, g.reshape(1, D_MODEL), w_in_g, w_in_v, hist_g, hist_v, cw_g, cw_v, cb_g, cb_v, w_out_b]
    if final_norm:
        in_specs.append(gain)
        args.append(g_final.reshape(1, D_MODEL))
    tail_shape = jax.ShapeDtypeStruct((tail_rows, SUBLANES, D_FF_PAD), F32)
    return pl.pallas_call(
        functools.partial(_ffn_kernel, stream_len=stream_len, tm=tm, final_norm=final_norm),
        grid=(n_tiles, nf),
        in_specs=in_specs,
        out_specs=[row_tile, tail_spec, tail_spec],
        out_shape=[jax.ShapeDtypeStruct((m, D_MODEL), F32), tail_shape, tail_shape],
        scratch_shapes=[
            pltpu.VMEM((tm, D_MODEL), BF16),
            pltpu.VMEM((nf, SUBLANES, TF), F32),
            pltpu.VMEM((nf, SUBLANES, TF), F32),
            pltpu.VMEM((2 * FFN_ROW_GROUPS, tm // FFN_ROW_GROUPS, TF), F32),
        ],
        compiler_params=_cparams(("arbitrary", "arbitrary")),
        name="conv_ffn",
    )(*args)


def _pad_ff(a):
    pad = [(0, 0)] * (a.ndim - 1) + [(0, D_FF_PAD - D_FF)]
    return jnp.pad(a, pad)


def _split_ff(a):
    return _pad_ff(a[..., :D_FF]), _pad_ff(a[..., D_FF:])


def kernel(x_prompt, x_sample, cache_fox_k, cache_fox_v, cache_fox_logf, cache_band_k, cache_band_v,
           state_conv, g_attn, w_qkv, w_fgate, b_fgate, rel_bias, w_o, g_ffn, w_in, conv_w, conv_b,
           w_out, g_final):
    depth = g_attn.shape[0]
    bp, sp, _ = x_prompt.shape
    bs, ss, _ = x_sample.shape
    past = cache_fox_k.shape[2]
    n_band_cache = cache_band_k.shape[2]
    n_keep = min(BAND_PAST, sp)
    assert bp == 1

    xp = x_prompt.reshape(bp * sp, D_MODEL)
    xs = x_sample.reshape(bs * ss, D_MODEL)

    samp_q_pos = past + np.arange(ss)
    samp_k_pos = past - n_band_cache + np.arange(n_band_cache + ss)
    n_fox = cache_fox_k.shape[0]
    n_band = cache_band_k.shape[0]
    cfk = cache_fox_k.reshape(n_fox * bs, past * N_HEADS, HEAD_DIM)
    cfv = cache_fox_v.reshape(n_fox * bs, past * N_HEADS, HEAD_DIM)
    cbk = cache_band_k.reshape(n_band * bs, n_band_cache, D_MODEL)
    cbv = cache_band_v.reshape(n_band * bs, n_band_cache, D_MODEL)

    hist0 = jnp.zeros((bp, CONV_W - 1, D_FF_PAD), F32)

    fox_lp, fox_ks, fox_vs, fox_ls = [], [], [], []
    band_kp, band_vp, band_ks, band_vs = [], [], [], []
    conv_p, conv_s = [], []
    fox_kv_p = None

    for i in range(depth):
        j = i // N_MIXERS
        is_fox = (i % N_MIXERS == 0)
        wqkv_b = w_qkv[i].astype(BF16)
        wo_b = w_o[i].astype(BF16)
        w_in_g, w_in_v = _split_ff(w_in[i].astype(BF16))
        cw_g, cw_v = _split_ff(conv_w[i])
        cb_g, cb_v = _split_ff(conv_b[i].reshape(1, 2 * D_FF))
        w_out_b = jnp.pad(w_out[i].astype(BF16), ((0, D_FF_PAD - D_FF), (0, 0)))
        hs_g, hs_v = _split_ff(state_conv[i])

        if is_fox:
            wf_b = jnp.pad(w_fgate[j].astype(BF16), ((0, 0), (0, LANES - N_HEADS)))
            bf = jnp.pad(b_fgate[j].reshape(1, N_HEADS), ((0, 0), (0, LANES - N_HEADS)))
            qp, kst, vst, kpb, vpb, lfp_pad = _qkv_proj(xp, g_attn[i], wqkv_b, wf_b, bf,
                                                        slot=j, n_slots=n_fox, kv_stack=fox_kv_p)
            fox_kv_p = (kst, vst)
            qs, ks32, vs32, ksb, vsb, lfs = _qkv_proj(xs, g_attn[i], wqkv_b, wf_b, bf)
            lfp = lfp_pad[:, :N_HEADS]
            lfs = lfs[:, :N_HEADS]

            aq, ak = _fox_aug(lfp_pad)
            op = _fox_prompt_attn(qp, aq, kpb, ak, vpb)

            lf_all = jnp.concatenate([
                jnp.transpose(cache_fox_logf[j], (0, 2, 1)),
                jnp.transpose(lfs.reshape(bs, ss, N_HEADS), (0, 2, 1)),
                jnp.zeros((bs, N_HEADS, LANES - ss), F32),
            ], axis=-1).reshape(bs * N_HEADS, past + LANES)
            cum_s = _cumsum_lanes(lf_all).reshape(bs, N_HEADS, past + LANES)
            cq_s = jnp.transpose(cum_s[:, :, past:past + ss], (0, 2, 1))
            osm = _fox_sample_attn(qs, ksb, vsb, cfk, cfv, j, cq_s, cum_s, bs, ss)

            fox_lp.append(lfp.reshape(bp, sp, N_HEADS))
            fox_ks.append(ks32.reshape(bs, ss, N_HEADS, HEAD_DIM))
            fox_vs.append(vs32.reshape(bs, ss, N_HEADS, HEAD_DIM))
            fox_ls.append(lfs.reshape(bs, ss, N_HEADS))
        else:
            qp, kp32, vp32, kpb, vpb = _qkv_proj(xp, g_attn[i], wqkv_b)
            qs, ks32, vs32, ksb, vsb = _qkv_proj(xs, g_attn[i], wqkv_b)
            op = _band_prompt_attn(qp, kpb, vpb, _band_profile(rel_bias[j]))
            osm = _band_sample_attn(qs, ksb, vsb, cbk, cbv, j,
                                    _band_bias(rel_bias[j], samp_q_pos, samp_k_pos), bs, ss)
            band_kp.append(kp32[0, sp - n_keep:].reshape(bp, n_keep, N_HEADS, HEAD_DIM))
            band_vp.append(vp32[0, sp - n_keep:].reshape(bp, n_keep, N_HEADS, HEAD_DIM))
            band_ks.append(ks32.reshape(bs, ss, N_HEADS, HEAD_DIM))
            band_vs.append(vs32.reshape(bs, ss, N_HEADS, HEAD_DIM))

        xp = _out_proj(xp, op, wo_b)
        xs = _out_proj(xs, osm, wo_b)

        gf = g_final if i == depth - 1 else None
        xp, tgp, tvp = _conv_ffn(xp, g_ffn[i], w_in_g, w_in_v, hist0, hist0, cw_g, cw_v, cb_g, cb_v,
                                 w_out_b, sp, gf)
        xs, tgs, tvs = _conv_ffn(xs, g_ffn[i], w_in_g, w_in_v, hs_g, hs_v, cw_g, cw_v, cb_g, cb_v,
                                 w_out_b, ss, gf)
        keep = slice(SUBLANES - (CONV_W - 1), SUBLANES)
        conv_p.append(jnp.concatenate([tgp[-1:, keep, :D_FF], tvp[-1:, keep, :D_FF]], axis=-1))
        conv_s.append(jnp.concatenate([tgs[:, keep, :D_FF], tvs[:, keep, :D_FF]], axis=-1))

    y_prompt = xp.reshape(bp, sp, D_MODEL)
    y_sample = xs.reshape(bs, ss, D_MODEL)
    fox_kp, fox_vp = (a.reshape(n_fox, bp, sp, N_HEADS, HEAD_DIM) for a in fox_kv_p)
    return (y_prompt, y_sample,
            fox_kp, fox_vp, jnp.stack(fox_lp),
            jnp.stack(fox_ks), jnp.stack(fox_vs), jnp.stack(fox_ls),
            jnp.stack(band_kp), jnp.stack(band_vp), jnp.stack(band_ks), jnp.stack(band_vs),
            jnp.stack(conv_p), jnp.stack(conv_s))
```

```python
import functools
import math

import numpy as np
import jax
import jax.numpy as jnp
from jax import lax
from jax.experimental import pallas as pl
from jax.experimental.pallas import tpu as pltpu

F32 = jnp.float32
BF16 = jnp.bfloat16

D_MODEL = 2048
N_HEADS = 16
HEAD_DIM = D_MODEL // N_HEADS
SCALE = 1.0 / math.sqrt(HEAD_DIM)
FOX_DEAD_LOGIT = 128.0
CHUNK = 64
BAND_CHUNKS = 8
BAND_PAST = BAND_CHUNKS * CHUNK
MAX_REL = 2 * CHUNK
D_FF = 5504
CONV_W = 3
EPS = 1e-6
N_MIXERS = 2

LANES = 128
SUBLANES = 8
VMEM_LIMIT = 56 * 1024 * 1024

TM = 512
TN_QKV = 512
TF = 512
FFN_ROW_GROUPS = 2
D_FF_PAD = ((D_FF + TF - 1) // TF) * TF
TK_FOX = 512
TQ_FOX = 2 * TK_FOX
TK_FOX_S = 512
TQ_BAND = 256
CUM_BLK = 128


def _cparams(sem):
    return pltpu.CompilerParams(dimension_semantics=sem, vmem_limit_bytes=VMEM_LIMIT)


def _rmsnorm_f32(x, g):
    ms = jnp.mean(x * x, axis=-1, keepdims=True)
    return (x * lax.rsqrt(ms + EPS)) * g


def _qkv_kernel(*refs, with_fgate, n_aliased, q_scale):
    refs = refs[:5 + 2 * with_fgate] + refs[5 + 2 * with_fgate + n_aliased:]
    if with_fgate:
        (x_ref, g_ref, wq_ref, wk_ref, wv_ref, wf_ref, bf_ref,
         q_ref, k32_ref, v32_ref, kb_ref, vb_ref, lf_ref, h_ref) = refs
    else:
        (x_ref, g_ref, wq_ref, wk_ref, wv_ref,
         q_ref, k32_ref, v32_ref, kb_ref, vb_ref, h_ref) = refs

    @pl.when(pl.program_id(1) == 0)
    def _():
        hb = _rmsnorm_f32(x_ref[...], g_ref[...]).astype(BF16)
        h_ref[...] = hb
        if with_fgate:
            z = jnp.dot(hb, wf_ref[...], preferred_element_type=F32) + bf_ref[...]
            lf_ref[...] = jnp.minimum(z, 0.0) - jnp.log1p(jnp.exp(-jnp.abs(z)))

    h = h_ref[...]
    q = jnp.dot(h, wq_ref[...], preferred_element_type=F32)
    q_ref[...] = (q * q_scale).astype(BF16)
    k = jnp.dot(h, wk_ref[...], preferred_element_type=F32)
    k32_ref[...] = k
    kb_ref[...] = k.astype(BF16)
    v = jnp.dot(h, wv_ref[...], preferred_element_type=F32)
    v32_ref[...] = v
    vb_ref[...] = v.astype(BF16)


def _qkv_proj(x, g, w_qkv_b, wf_b=None, bf=None, slot=0, n_slots=1, kv_stack=None, q_scale=SCALE):
    m = x.shape[0]
    tm = min(TM, m)
    nn = D_MODEL // TN_QKV
    with_fgate = wf_b is not None
    stack_tile = pl.BlockSpec((None, tm, TN_QKV), lambda i, n: (slot, i, n))
    stack_shape = jax.ShapeDtypeStruct((n_slots, m, D_MODEL), F32)
    in_specs = [
        pl.BlockSpec((tm, D_MODEL), lambda i, n: (i, 0)),
        pl.BlockSpec((1, D_MODEL), lambda i, n: (0, 0)),
        pl.BlockSpec((D_MODEL, TN_QKV), lambda i, n: (0, n)),
        pl.BlockSpec((D_MODEL, TN_QKV), lambda i, n: (0, n + nn)),
        pl.BlockSpec((D_MODEL, TN_QKV), lambda i, n: (0, n + 2 * nn)),
    ]
    args = [x, g.reshape(1, D_MODEL), w_qkv_b, w_qkv_b, w_qkv_b]
    tile = pl.BlockSpec((tm, TN_QKV), lambda i, n: (i, n))
    out_specs = [tile, stack_tile, stack_tile, tile, tile]
    out_shape = [
        jax.ShapeDtypeStruct((m, D_MODEL), BF16),
        stack_shape,
        stack_shape,
        jax.ShapeDtypeStruct((m, D_MODEL), BF16),
        jax.ShapeDtypeStruct((m, D_MODEL), BF16),
    ]
    if with_fgate:
        in_specs += [
            pl.BlockSpec((D_MODEL, LANES), lambda i, n: (0, 0)),
            pl.BlockSpec((1, LANES), lambda i, n: (0, 0)),
        ]
        args += [wf_b, bf]
        out_specs.append(pl.BlockSpec((tm, LANES), lambda i, n: (i, 0)))
        out_shape.append(jax.ShapeDtypeStruct((m, LANES), F32))
    aliases = {}
    if kv_stack is not None:
        aliases = {len(args): 1, len(args) + 1: 2}
        in_specs += [pl.BlockSpec(memory_space=pl.ANY)] * 2
        args += list(kv_stack)
    return pl.pallas_call(
        functools.partial(_qkv_kernel, with_fgate=with_fgate, n_aliased=len(aliases),
                          q_scale=q_scale),
        grid=(m // tm, nn),
        in_specs=in_specs,
        out_specs=out_specs,
        out_shape=out_shape,
        scratch_shapes=[pltpu.VMEM((tm, D_MODEL), BF16)],
        input_output_aliases=aliases,
        compiler_params=_cparams(("arbitrary", "arbitrary")),
        name="qkv_proj",
    )(*args)


def _cumsum_kernel(x_ref, o_ref, carry_ref):
    @pl.when(pl.program_id(0) == 0)
    def _():
        carry_ref[...] = jnp.zeros_like(carry_ref)

    x = x_ref[...]
    hi = x.astype(BF16)
    r1 = x - hi.astype(F32)
    mid = r1.astype(BF16)
    lo = (r1 - mid.astype(F32)).astype(BF16)
    row = lax.broadcasted_iota(jnp.int32, (CUM_BLK, CUM_BLK), 0)
    col = lax.broadcasted_iota(jnp.int32, (CUM_BLK, CUM_BLK), 1)
    tri = (row <= col).astype(BF16)
    s = jnp.dot(lo, tri, preferred_element_type=F32)
    s = s + jnp.dot(mid, tri, preferred_element_type=F32)
    s = s + jnp.dot(hi, tri, preferred_element_type=F32)
    s = s + carry_ref[...]
    o_ref[...] = s
    carry_ref[...] = jnp.broadcast_to(s[:, CUM_BLK - 1:CUM_BLK], s.shape)


def _cumsum_lanes(x):
    r, t = x.shape
    return pl.pallas_call(
        _cumsum_kernel,
        grid=(t // CUM_BLK,),
        in_specs=[pl.BlockSpec((r, CUM_BLK), lambda i: (0, i))],
        out_specs=pl.BlockSpec((r, CUM_BLK), lambda i: (0, i)),
        out_shape=jax.ShapeDtypeStruct((r, t), F32),
        scratch_shapes=[pltpu.VMEM((r, CUM_BLK), F32)],
        compiler_params=_cparams(("arbitrary",)),
        name="cumsum_time",
    )(x)


def _online_softmax_step(s, v, m_ref, l_ref, acc_ref):
    m_prev = m_ref[...]
    m_new = jnp.maximum(m_prev, jnp.max(s, axis=-1, keepdims=True))
    alpha = jnp.exp(m_prev - m_new)
    p = jnp.exp(s - m_new)
    l_ref[...] = alpha * l_ref[...] + jnp.sum(p, axis=-1, keepdims=True)
    acc_ref[...] = alpha * acc_ref[...] + jnp.dot(p.astype(BF16), v, preferred_element_type=F32)
    m_ref[...] = m_new


def _split3_bf16(x):
    hi = x.astype(BF16)
    r1 = x - hi.astype(F32)
    mid = r1.astype(BF16)
    lo = (r1 - mid.astype(F32)).astype(BF16)
    return jnp.concatenate([hi, mid, lo], axis=-1)


def _fox_aug_placement():
    pq = np.zeros((3 * LANES, D_MODEL), np.float32)
    pk = np.zeros((3 * LANES, D_MODEL), np.float32)
    cq = np.zeros((1, D_MODEL), np.float32)
    ck = np.zeros((1, D_MODEL), np.float32)
    for h in range(N_HEADS):
        for piece in range(3):
            pq[piece * LANES + h, h * HEAD_DIM + piece] = 1.0
            pk[piece * LANES + h, h * HEAD_DIM + 3 + piece] = -1.0
            cq[0, h * HEAD_DIM + 3 + piece] = 1.0
            ck[0, h * HEAD_DIM + piece] = 1.0
    return (jnp.asarray(pq, BF16), jnp.asarray(pk, BF16), jnp.asarray(cq), jnp.asarray(ck))


def _fox_aug_kernel(x_ref, pq_ref, pk_ref, cq_ref, ck_ref, aq_ref, ak_ref, cum_ref, carry_ref):
    @pl.when(pl.program_id(0) == 0)
    def _():
        carry_ref[...] = jnp.zeros_like(carry_ref)

    tb = x_ref.shape[0]
    row = lax.broadcasted_iota(jnp.int32, (tb, tb), 0)
    col = lax.broadcasted_iota(jnp.int32, (tb, tb), 1)
    tri = (col <= row).astype(BF16)
    x3 = _split3_bf16(x_ref[...])
    c3 = jnp.dot(tri, x3, preferred_element_type=F32)
    cum = (c3[:, 2 * LANES:] + c3[:, LANES:2 * LANES]) + c3[:, :LANES] + carry_ref[...]
    carry_ref[...] = cum[tb - 1:tb, :]
    cum_ref[...] = cum
    cum3 = _split3_bf16(cum)
    aq = jnp.dot(cum3, pq_ref[...], preferred_element_type=F32) + cq_ref[...]
    ak = jnp.dot(cum3, pk_ref[...], preferred_element_type=F32) + ck_ref[...]
    aq_ref[...] = aq.astype(BF16)
    ak_ref[...] = ak.astype(BF16)


def _fox_aug(logf_pad):
    t_len = logf_pad.shape[0]
    tb = 2 * LANES
    pq, pk, cq, ck = _fox_aug_placement()
    const = lambda shape: pl.BlockSpec(shape, lambda i: (0, 0))
    return pl.pallas_call(
        _fox_aug_kernel,
        grid=(t_len // tb,),
        in_specs=[
            pl.BlockSpec((tb, LANES), lambda i: (i, 0)),
            const((3 * LANES, D_MODEL)), const((3 * LANES, D_MODEL)),
            const((1, D_MODEL)), const((1, D_MODEL)),
        ],
        out_specs=[pl.BlockSpec((tb, D_MODEL), lambda i: (i, 0))] * 2
        + [pl.BlockSpec((tb, LANES), lambda i: (i, 0))],
        out_shape=[jax.ShapeDtypeStruct((t_len, D_MODEL), BF16)] * 2
        + [jax.ShapeDtypeStruct((t_len, LANES), F32)],
        scratch_shapes=[pltpu.VMEM((1, LANES), F32)],
        compiler_params=_cparams(("arbitrary",)),
        name="fox_aug",
    )(logf_pad, pq, pk, cq, ck)


def _fox_prompt_kernel(start_ref, q_ref, aq_ref, k_ref, ak_ref, v_ref, o_ref,
                       s0_ref, s1_ref, p0_ref, p1_ref, a0_ref, a1_ref, m_ref, acc_ref):
    qi = pl.program_id(1)
    j0 = start_ref[pl.program_id(0), qi]
    tq, tk = TQ_FOX, TK_FOX
    dn = (((1,), (1,)), ((), ()))
    q = jnp.concatenate([q_ref[...], aq_ref[...]], axis=1)
    ones_col = (lax.broadcasted_iota(jnp.int32, (tk, HEAD_DIM), 1) == 0).astype(BF16)

    def scores(ki, s_ref):
        ks = pl.multiple_of(ki * tk, tk)
        k = jnp.concatenate([k_ref[pl.ds(ks, tk), :], ak_ref[pl.ds(ks, tk), :]], axis=1)
        s_ref[...] = lax.dot_general(q, k, dn, preferred_element_type=F32)

    def accumulate(ki, p_ref, a_ref):
        ks = pl.multiple_of(ki * tk, tk)
        v = jnp.concatenate([v_ref[pl.ds(ks, tk), :], ones_col], axis=1)
        pv = jnp.dot(p_ref[...], v, preferred_element_type=F32)
        acc_ref[...] = a_ref[...] * acc_ref[...] + pv

    def softmax(s, p_ref, a_ref):
        m_prev = m_ref[...]
        m_new = jnp.maximum(m_prev, jnp.max(s, axis=-1, keepdims=True))
        a_ref[...] = jnp.exp(m_prev - m_new)
        p_ref[...] = jnp.exp(s - m_new).astype(BF16)
        m_ref[...] = m_new

    m_ref[...] = jnp.full_like(m_ref, -jnp.inf)
    acc_ref[...] = jnp.zeros_like(acc_ref)
    p1_ref[...] = jnp.zeros_like(p1_ref)
    a1_ref[...] = jnp.ones_like(a1_ref)
    scores(2 * j0, s0_ref)

    def body(j, carry):
        scores(2 * j + 1, s1_ref)
        accumulate(jnp.maximum(2 * j - 1, 0), p1_ref, a1_ref)
        softmax(s0_ref[...], p0_ref, a0_ref)
        scores(2 * j + 2, s0_ref)
        accumulate(2 * j, p0_ref, a0_ref)
        softmax(s1_ref[...], p1_ref, a1_ref)
        return carry

    lax.fori_loop(j0, qi, body, 0)

    row = lax.broadcasted_iota(jnp.int32, (tq, tk), 0)
    col = lax.broadcasted_iota(jnp.int32, (tq, tk), 1)
    scores(2 * qi + 1, s1_ref)
    accumulate(jnp.maximum(2 * qi - 1, 0), p1_ref, a1_ref)
    softmax(jnp.where(col <= row, s0_ref[...], -jnp.inf), p0_ref, a0_ref)
    accumulate(2 * qi, p0_ref, a0_ref)
    softmax(jnp.where(col + tk <= row, s1_ref[...], -jnp.inf), p1_ref, a1_ref)
    accumulate(2 * qi + 1, p1_ref, a1_ref)
    acc = acc_ref[...]
    o_ref[...] = (acc[:, :HEAD_DIM] / acc[:, HEAD_DIM:HEAD_DIM + 1]).astype(BF16)


def _fox_first_live_pair(cum_pad, q, k):
    t_len = q.shape[0]
    nq = t_len // TQ_FOX

    def max_row_norm(a):
        a = a.astype(F32).reshape(t_len, N_HEADS, HEAD_DIM)
        return jnp.sqrt(jnp.max(jnp.sum(a * a, axis=-1), axis=0))

    slack = 2.0 * max_row_norm(q) * max_row_norm(k)
    cum = cum_pad[:, :N_HEADS]
    c_tile_start = cum[0::TQ_FOX]
    c_pair_end = cum[TQ_FOX - 1::TQ_FOX]
    bound = c_tile_start[:, None, :] - c_pair_end[None, :, :] + slack[None, None, :]
    before = np.arange(nq)[None, :, None] < np.arange(nq)[:, None, None]
    dead = jnp.logical_and(bound < -FOX_DEAD_LOGIT, before)
    lead = jnp.cumprod(dead.astype(jnp.int32), axis=1)
    return jnp.transpose(jnp.sum(lead, axis=1)).astype(jnp.int32)


def _fox_prompt_attn(q, aq, k, ak, v, first_pair):
    t_len = q.shape[0]
    tq, tk = TQ_FOX, TK_FOX
    assert tq == 2 * tk and t_len % tq == 0
    tile = pl.BlockSpec((tq, HEAD_DIM), lambda h, i, start: (i, h))
    whole = pl.BlockSpec((t_len, HEAD_DIM), lambda h, i, start: (0, h))
    return pl.pallas_call(
        _fox_prompt_kernel,
        grid_spec=pltpu.PrefetchScalarGridSpec(
            num_scalar_prefetch=1,
            grid=(N_HEADS, t_len // tq),
            in_specs=[tile, tile, whole, whole, whole],
            out_specs=tile,
            scratch_shapes=[
                pltpu.VMEM((tq, tk), F32), pltpu.VMEM((tq, tk), F32),
                pltpu.VMEM((tq, tk), BF16), pltpu.VMEM((tq, tk), BF16),
                pltpu.VMEM((tq, 1), F32), pltpu.VMEM((tq, 1), F32),
                pltpu.VMEM((tq, 1), F32),
                pltpu.VMEM((tq, 2 * HEAD_DIM), F32),
            ],
        ),
        out_shape=jax.ShapeDtypeStruct((t_len, D_MODEL), BF16),
        compiler_params=_cparams(("arbitrary", "arbitrary")),
        name="fox_prompt_attn",
    )(first_pair, q, aq, k, ak, v)


def _fox_sample_kernel(q_ref, kn_ref, vn_ref, kc_ref, vc_ref, cq_ref, ckc_ref, ckn_ref,
                       o_ref, s_ref, sn_ref, m_ref, acc_ref, *, n_new, tk):
    ki = pl.program_id(1)
    nk = pl.num_programs(1)
    dn = (((1,), (1,)), ((), ()))

    @pl.when(ki == 0)
    def _():
        m_ref[...] = jnp.full_like(m_ref, -jnp.inf)
        acc_ref[...] = jnp.zeros_like(acc_ref)

    def head_rows(h):
        return slice(h * n_new, (h + 1) * n_new)

    def head_cols(h):
        return slice(h * HEAD_DIM, (h + 1) * HEAD_DIM)

    def softmax_update(s):
        m_prev = m_ref[...]
        m_new = jnp.maximum(m_prev, jnp.max(s, axis=-1, keepdims=True))
        m_ref[...] = m_new
        return jnp.exp(m_prev - m_new), jnp.exp(s - m_new).astype(BF16)

    def accumulate(alpha, p, values):
        n_keys = p.shape[1]
        ones_col = (lax.broadcasted_iota(jnp.int32, (n_keys, HEAD_DIM), 1) == 0).astype(BF16)
        for h in range(N_HEADS):
            r = head_rows(h)
            pv = jnp.dot(p[r], jnp.concatenate([values(h), ones_col], axis=1),
                         preferred_element_type=F32)
            acc_ref[r, :] = alpha[r] * acc_ref[r, :] + pv

    for h in range(N_HEADS):
        kc = kc_ref[0, pl.ds(h, tk, stride=N_HEADS), :].astype(BF16)
        s = lax.dot_general(q_ref[:, head_cols(h)], kc, dn, preferred_element_type=F32)
        s_ref[head_rows(h), :] = (s + cq_ref[0, :, h:h + 1]) - ckc_ref[0, h:h + 1, :]
    alpha, p = softmax_update(s_ref[...])
    accumulate(alpha, p, lambda h: vc_ref[0, pl.ds(h, tk, stride=N_HEADS), :].astype(BF16))

    @pl.when(ki == nk - 1)
    def _():
        row = lax.broadcasted_iota(jnp.int32, (n_new, n_new), 0)
        col = lax.broadcasted_iota(jnp.int32, (n_new, n_new), 1)
        for h in range(N_HEADS):
            s = lax.dot_general(q_ref[:, head_cols(h)], kn_ref[:, head_cols(h)], dn,
                                preferred_element_type=F32)
            s = (s + cq_ref[0, :, h:h + 1]) - ckn_ref[0, h:h + 1, 0:n_new]
            sn_ref[head_rows(h), :] = jnp.where(col <= row, s, -jnp.inf)
        alpha, p = softmax_update(sn_ref[...])
        accumulate(alpha, p, lambda h: vn_ref[:, head_cols(h)])
        for h in range(N_HEADS):
            acc = acc_ref[head_rows(h), :]
            o_ref[:, head_cols(h)] = (acc[:, :HEAD_DIM] / acc[:, HEAD_DIM:HEAD_DIM + 1]).astype(BF16)


def _fox_sample_attn(q, kn, vn, cache_k, cache_v, layer, cq_bth, cum_bht, n_streams, n_new):
    past = cache_k.shape[1] // N_HEADS
    tk = TK_FOX_S
    first = layer * n_streams
    rows = N_HEADS * n_new
    return pl.pallas_call(
        functools.partial(_fox_sample_kernel, n_new=n_new, tk=tk),
        grid=(n_streams, past // tk),
        in_specs=[
            pl.BlockSpec((n_new, D_MODEL), lambda b, i: (b, 0)),
            pl.BlockSpec((n_new, D_MODEL), lambda b, i: (b, 0)),
            pl.BlockSpec((n_new, D_MODEL), lambda b, i: (b, 0)),
            pl.BlockSpec((1, tk * N_HEADS, HEAD_DIM), lambda b, i: (first + b, i, 0)),
            pl.BlockSpec((1, tk * N_HEADS, HEAD_DIM), lambda b, i: (first + b, i, 0)),
            pl.BlockSpec((1, n_new, N_HEADS), lambda b, i: (b, 0, 0)),
            pl.BlockSpec((1, N_HEADS, tk), lambda b, i: (b, 0, i)),
            pl.BlockSpec((1, N_HEADS, LANES), lambda b, i: (b, 0, past // LANES)),
        ],
        out_specs=pl.BlockSpec((n_new, D_MODEL), lambda b, i: (b, 0)),
        out_shape=jax.ShapeDtypeStruct((n_streams * n_new, D_MODEL), BF16),
        scratch_shapes=[
            pltpu.VMEM((rows, tk), F32),
            pltpu.VMEM((rows, n_new), F32),
            pltpu.VMEM((rows, 1), F32),
            pltpu.VMEM((rows, 2 * HEAD_DIM), F32),
        ],
        compiler_params=_cparams(("arbitrary", "arbitrary")),
        name="fox_sample_attn",
    )(q, kn, vn, cache_k, cache_v, cq_bth, cum_bht, cum_bht)


def _band_bias(rel_table, q_pos, k_pos):
    rel = np.clip(q_pos[:, None] - k_pos[None, :], -MAX_REL, MAX_REL) + MAX_REL
    qc = q_pos // CHUNK
    kc = k_pos // CHUNK
    ok = (k_pos[None, :] >= 0) & (kc[None, :] <= qc[:, None]) & (kc[None, :] >= qc[:, None] - BAND_CHUNKS)
    bias = rel_table.astype(F32)[:, rel]
    return jnp.where(jnp.asarray(ok)[None], bias, -jnp.inf)


def _softmax_rows(s):
    m = jnp.max(s, axis=-1, keepdims=True)
    p = jnp.exp(s - m)
    return p, jnp.sum(p, axis=-1, keepdims=True)


def _band_prompt_kernel(q_ref, kp_ref, kc_ref, vp_ref, vc_ref, g_ref, o_ref, b_ref):
    qi = pl.program_id(1)
    t = TQ_BAND

    @pl.when(qi == 0)
    def _():
        n = g_ref.shape[-1]
        g = jnp.broadcast_to(g_ref[0], (t, n))
        r = pltpu.roll(g, n - (t - 1), axis=1, stride=1, stride_axis=0)
        qc = lax.broadcasted_iota(jnp.int32, (t, 3 * t), 0) // CHUNK
        kc = lax.broadcasted_iota(jnp.int32, (t, 3 * t), 1) // CHUNK
        ok = (kc >= qc) & (kc <= qc + BAND_CHUNKS)
        b_ref[...] = jnp.where(ok, r[:, :3 * t], -jnp.inf)

    k = jnp.concatenate([kp_ref[...], kc_ref[...]], axis=0)
    v = jnp.concatenate([vp_ref[...], vc_ref[...]], axis=0)
    col = lax.broadcasted_iota(jnp.int32, (t, 3 * t), 1)
    for sub in range(BAND_PAST // t):
        lo = sub * t
        q = q_ref[lo:lo + t, :]
        s = lax.dot_general(q, k[lo:lo + 3 * t], (((1,), (1,)), ((), ())),
                            preferred_element_type=F32) + b_ref[...]
        first_valid = jnp.where(qi == 0, BAND_PAST - lo, 0)
        s = jnp.where(col >= first_valid, s, -jnp.inf)
        p, l = _softmax_rows(s)
        o = jnp.dot(p.astype(BF16), v[lo:lo + 3 * t], preferred_element_type=F32)
        o_ref[lo:lo + t, :] = (o / l).astype(BF16)


def _band_profile(rel_table):
    t = TQ_BAND
    dist = BAND_PAST + (t - 1) - np.arange(4 * t)
    idx = np.clip(dist, -MAX_REL, MAX_REL) + MAX_REL
    return rel_table.astype(F32)[:, idx].reshape(N_HEADS, 1, 4 * t)


def _band_prompt_attn(q, k, v, profile):
    t_len = q.shape[0]
    t = TQ_BAND
    tb = BAND_PAST
    assert tb == 2 * t and t % CHUNK == 0 and t_len % tb == 0
    cur = pl.BlockSpec((tb, HEAD_DIM), lambda h, i: (i, h))
    prev = pl.BlockSpec((tb, HEAD_DIM), lambda h, i: (jnp.maximum(i - 1, 0), h))
    return pl.pallas_call(
        _band_prompt_kernel,
        grid=(N_HEADS, t_len // tb),
        in_specs=[cur, prev, cur, prev, cur,
                  pl.BlockSpec((1, 1, 4 * t), lambda h, i: (h, 0, 0))],
        out_specs=cur,
        out_shape=jax.ShapeDtypeStruct((t_len, D_MODEL), BF16),
        scratch_shapes=[pltpu.VMEM((t, 3 * t), F32)],
        compiler_params=_cparams(("arbitrary", "arbitrary")),
        name="band_prompt_attn",
    )(q, k, k, v, v, profile)


def _band_sample_kernel(q_ref, kn_ref, vn_ref, kc_ref, vc_ref, b_ref, o_ref, *, n_cache):
    for h in range(N_HEADS):
        hs = slice(h * HEAD_DIM, (h + 1) * HEAD_DIM)
        q = q_ref[:, hs]
        kc = kc_ref[0, :, hs].astype(BF16)
        vc = vc_ref[0, :, hs].astype(BF16)
        dn = (((1,), (1,)), ((), ()))
        s1 = lax.dot_general(q, kc, dn, preferred_element_type=F32) + b_ref[h, :, 0:n_cache]
        s2 = lax.dot_general(q, kn_ref[:, hs], dn, preferred_element_type=F32) + b_ref[h, :, n_cache:]
        m = jnp.maximum(jnp.max(s1, axis=-1, keepdims=True), jnp.max(s2, axis=-1, keepdims=True))
        p1 = jnp.exp(s1 - m)
        p2 = jnp.exp(s2 - m)
        l = jnp.sum(p1, axis=-1, keepdims=True) + jnp.sum(p2, axis=-1, keepdims=True)
        o = jnp.dot(p1.astype(BF16), vc, preferred_element_type=F32)
        o = o + jnp.dot(p2.astype(BF16), vn_ref[:, hs], preferred_element_type=F32)
        o_ref[:, hs] = (o / l).astype(BF16)


def _band_sample_attn(q, kn, vn, cache_k, cache_v, layer, bias, n_streams, n_new):
    n_cache = cache_k.shape[1]
    first = layer * n_streams
    return pl.pallas_call(
        functools.partial(_band_sample_kernel, n_cache=n_cache),
        grid=(n_streams,),
        in_specs=[
            pl.BlockSpec((n_new, D_MODEL), lambda b: (b, 0)),
            pl.BlockSpec((n_new, D_MODEL), lambda b: (b, 0)),
            pl.BlockSpec((n_new, D_MODEL), lambda b: (b, 0)),
            pl.BlockSpec((1, n_cache, D_MODEL), lambda b: (first + b, 0, 0)),
            pl.BlockSpec((1, n_cache, D_MODEL), lambda b: (first + b, 0, 0)),
            pl.BlockSpec((N_HEADS, n_new, n_cache + n_new), lambda b: (0, 0, 0)),
        ],
        out_specs=pl.BlockSpec((n_new, D_MODEL), lambda b: (b, 0)),
        out_shape=jax.ShapeDtypeStruct((n_streams * n_new, D_MODEL), BF16),
        compiler_params=_cparams(("arbitrary",)),
        name="band_sample_attn",
    )(q, kn, vn, cache_k, cache_v, bias)


def _oproj_kernel(x_ref, a_ref, w_ref, o_ref):
    o_ref[...] = x_ref[...] + jnp.dot(a_ref[...], w_ref[...], preferred_element_type=F32)


def _out_proj(x, a, w_o_b):
    m = x.shape[0]
    tm = min(TM, m)
    return pl.pallas_call(
        _oproj_kernel,
        grid=(m // tm,),
        in_specs=[
            pl.BlockSpec((tm, D_MODEL), lambda i: (i, 0)),
            pl.BlockSpec((tm, D_MODEL), lambda i: (i, 0)),
            pl.BlockSpec((D_MODEL, D_MODEL), lambda i: (0, 0)),
        ],
        out_specs=pl.BlockSpec((tm, D_MODEL), lambda i: (i, 0)),
        out_shape=jax.ShapeDtypeStruct((m, D_MODEL), F32),
        compiler_params=_cparams(("arbitrary",)),
        name="out_proj",
    )(x, a, w_o_b)


def _ffn_kernel(*refs, stream_len, tm, final_norm):
    if final_norm:
        (x_ref, g_ref, wg_ref, wv_ref, hg_ref, hv_ref, cwg_ref, cwv_ref, cbg_ref, cbv_ref,
         wo_ref, gf_ref, o_ref, tg_ref, tv_ref, h_ref, pg_ref, pv_ref, u_ref) = refs
    else:
        (x_ref, g_ref, wg_ref, wv_ref, hg_ref, hv_ref, cwg_ref, cwv_ref, cbg_ref, cbv_ref,
         wo_ref, o_ref, tg_ref, tv_ref, h_ref, pg_ref, pv_ref, u_ref) = refs
        gf_ref = None
    i = pl.program_id(0)
    f = pl.program_id(1)
    nf = pl.num_programs(1)
    streams_per_tile = max(tm // stream_len, 1)
    tiles_per_stream = max(stream_len // tm, 1)

    @pl.when(f == 0)
    def _():
        h_ref[...] = _rmsnorm_f32(x_ref[...], g_ref[...]).astype(BF16)
        o_ref[...] = x_ref[...]

    n_groups = FFN_ROW_GROUPS
    rg = tm // n_groups
    assert rg % SUBLANES == 0 and (streams_per_tile == 1 or streams_per_tile % n_groups == 0)
    row = lax.broadcasted_iota(jnp.int32, (rg, TF), 0)
    if streams_per_tile > 1:
        row = row % stream_len
    halves = ((wg_ref, hg_ref, cwg_ref, cbg_ref, pg_ref, tg_ref),
              (wv_ref, hv_ref, cwv_ref, cbv_ref, pv_ref, tv_ref))

    class _Up:
        def __init__(self, g, half):
            self.ref = u_ref.at[half * n_groups + g]
            self.ref[...] = jnp.dot(h_ref[g * rg:(g + 1) * rg, :], halves[half][0][...],
                                    preferred_element_type=F32)

        def __getitem__(self, idx):
            return self.ref[idx]

        def reshape(self, *shape):
            return self.ref[...].reshape(*shape)

    if streams_per_tile == 1:
        @pl.when((i % tiles_per_stream) == 0)
        def _():
            for (_, hist_ref, _, _, prev_ref, _) in halves:
                prev_ref[f, SUBLANES - (CONV_W - 1):SUBLANES, :] = hist_ref[0]

    ups = [[_Up(0, half) for half in range(2)]]

    def conv(g, half):
        _, hist_ref, cw_ref, cb_ref, prev_ref, _ = halves[half]
        u = ups[g][half][...]
        if streams_per_tile > 1:
            spg = streams_per_tile // n_groups
            def per_row(r):
                hr = hist_ref[g * spg:(g + 1) * spg, r:r + 1, :]
                return jnp.broadcast_to(hr, (spg, stream_len, TF)).reshape(rg, TF)
            p0, p1 = per_row(0), per_row(1)
        elif g == 0:
            p0 = prev_ref[f, SUBLANES - 2:SUBLANES - 1, :]
            p1 = prev_ref[f, SUBLANES - 1:SUBLANES, :]
        else:
            p0 = ups[g - 1][half][rg - 2:rg - 1, :]
            p1 = ups[g - 1][half][rg - 1:rg, :]
        u1 = jnp.where(row == 0, p1, pltpu.roll(u, 1, axis=0))
        u2 = jnp.where(row == 0, p0, jnp.where(row == 1, p1, pltpu.roll(u, 2, axis=0)))
        c = cb_ref[...] + cw_ref[0:1, :] * u2
        c = c + cw_ref[1:2, :] * u1
        return c + cw_ref[2:3, :] * u

    for g in range(n_groups):
        cg = conv(g, 0)
        cv = conv(g, 1)
        a = ((cg * jax.nn.sigmoid(cg)) * cv).astype(BF16)
        if g + 1 < n_groups:
            ups.append([_Up(g + 1, half) for half in range(2)])
        o_ref[g * rg:(g + 1) * rg, :] += jnp.dot(a, wo_ref[...], preferred_element_type=F32)

    for half, (_, _, _, _, prev_ref, tail_ref) in enumerate(halves):
        if streams_per_tile > 1:
            spg = streams_per_tile // n_groups
            for g in range(n_groups):
                u3 = ups[g][half].reshape(spg, stream_len, TF)
                tail_ref[g * spg:(g + 1) * spg] = u3[:, stream_len - SUBLANES:, :]
        else:
            last = ups[n_groups - 1][half][rg - SUBLANES:, :]
            prev_ref[f] = last
            tail_ref[0] = last

    if final_norm:
        @pl.when(f == nf - 1)
        def _():
            o_ref[...] = _rmsnorm_f32(o_ref[...], gf_ref[...])


def _conv_ffn(x, g, w_in_g, w_in_v, hist_g, hist_v, cw_g, cw_v, cb_g, cb_v, w_out_b, stream_len,
              g_final=None):
    m = x.shape[0]
    tm = min(TM, m)
    nf = D_FF_PAD // TF
    n_tiles = m // tm
    final_norm = g_final is not None
    if stream_len >= tm:
        assert stream_len % tm == 0
        tiles_per_stream = stream_len // tm
        hist_spec = pl.BlockSpec((1, CONV_W - 1, TF), lambda i, f: (i // tiles_per_stream, 0, f))
        tail_rows = n_tiles
        tail_spec = pl.BlockSpec((1, SUBLANES, TF), lambda i, f: (i, 0, f))
    else:
        assert tm % stream_len == 0 and stream_len % SUBLANES == 0 and n_tiles == 1
        spt = tm // stream_len
        hist_spec = pl.BlockSpec((spt, CONV_W - 1, TF), lambda i, f: (i, 0, f))
        tail_rows = m // stream_len
        tail_spec = pl.BlockSpec((spt, SUBLANES, TF), lambda i, f: (i, 0, f))
    row_tile = pl.BlockSpec((tm, D_MODEL), lambda i, f: (i, 0))
    gain = pl.BlockSpec((1, D_MODEL), lambda i, f: (0, 0))
    w_in_spec = pl.BlockSpec((D_MODEL, TF), lambda i, f: (0, f))
    cw_spec = pl.BlockSpec((CONV_W, TF), lambda i, f: (0, f))
    cb_spec = pl.BlockSpec((1, TF), lambda i, f: (0, f))
    in_specs = [row_tile, gain, w_in_spec, w_in_spec, hist_spec, hist_spec, cw_spec, cw_spec,
                cb_spec, cb_spec, pl.BlockSpec((TF, D_MODEL), lambda i, f: (f, 0))]
    args = [x, g.reshape(1, D_MODEL), w_in_g, w_in_v, hist_g, hist_v, cw_g, cw_v, cb_g, cb_v, w_out_b]
    if final_norm:
        in_specs.append(gain)
        args.append(g_final.reshape(1, D_MODEL))
    tail_shape = jax.ShapeDtypeStruct((tail_rows, SUBLANES, D_FF_PAD), F32)
    return pl.pallas_call(
        functools.partial(_ffn_kernel, stream_len=stream_len, tm=tm, final_norm=final_norm),
        grid=(n_tiles, nf),
        in_specs=in_specs,
        out_specs=[row_tile, tail_spec, tail_spec],
        out_shape=[jax.ShapeDtypeStruct((m, D_MODEL), F32), tail_shape, tail_shape],
        scratch_shapes=[
            pltpu.VMEM((tm, D_MODEL), BF16),
            pltpu.VMEM((nf, SUBLANES, TF), F32),
            pltpu.VMEM((nf, SUBLANES, TF), F32),
            pltpu.VMEM((2 * FFN_ROW_GROUPS, tm // FFN_ROW_GROUPS, TF), F32),
        ],
        compiler_params=_cparams(("arbitrary", "arbitrary")),
        name="conv_ffn",
    )(*args)


def _pad_ff(a):
    pad = [(0, 0)] * (a.ndim - 1) + [(0, D_FF_PAD - D_FF)]
    return jnp.pad(a, pad)


def _split_ff(a):
    return _pad_ff(a[..., :D_FF]), _pad_ff(a[..., D_FF:])


def kernel(x_prompt, x_sample, cache_fox_k, cache_fox_v, cache_fox_logf, cache_band_k, cache_band_v,
           state_conv, g_attn, w_qkv, w_fgate, b_fgate, rel_bias, w_o, g_ffn, w_in, conv_w, conv_b,
           w_out, g_final):
    depth = g_attn.shape[0]
    bp, sp, _ = x_prompt.shape
    bs, ss, _ = x_sample.shape
    past = cache_fox_k.shape[2]
    n_band_cache = cache_band_k.shape[2]
    n_keep = min(BAND_PAST, sp)
    assert bp == 1

    xp = x_prompt.reshape(bp * sp, D_MODEL)
    xs = x_sample.reshape(bs * ss, D_MODEL)

    samp_q_pos = past + np.arange(ss)
    samp_k_pos = past - n_band_cache + np.arange(n_band_cache + ss)
    n_fox = cache_fox_k.shape[0]
    n_band = cache_band_k.shape[0]
    cfk = cache_fox_k.reshape(n_fox * bs, past * N_HEADS, HEAD_DIM)
    cfv = cache_fox_v.reshape(n_fox * bs, past * N_HEADS, HEAD_DIM)
    cbk = cache_band_k.reshape(n_band * bs, n_band_cache, D_MODEL)
    cbv = cache_band_v.reshape(n_band * bs, n_band_cache, D_MODEL)

    hist0 = jnp.zeros((bp, CONV_W - 1, D_FF_PAD), F32)

    fox_lp, fox_ks, fox_vs, fox_ls = [], [], [], []
    band_kp, band_vp, band_ks, band_vs = [], [], [], []
    conv_p, conv_s = [], []
    fox_kv_p = None

    for i in range(depth):
        j = i // N_MIXERS
        is_fox = (i % N_MIXERS == 0)
        wqkv_b = w_qkv[i].astype(BF16)
        wo_b = w_o[i].astype(BF16)
        w_in_g, w_in_v = _split_ff(w_in[i].astype(BF16))
        cw_g, cw_v = _split_ff(conv_w[i])
        cb_g, cb_v = _split_ff(conv_b[i].reshape(1, 2 * D_FF))
        w_out_b = jnp.pad(w_out[i].astype(BF16), ((0, D_FF_PAD - D_FF), (0, 0)))
        hs_g, hs_v = _split_ff(state_conv[i])

        if is_fox:
            wf_b = jnp.pad(w_fgate[j].astype(BF16), ((0, 0), (0, LANES - N_HEADS)))
            bf = jnp.pad(b_fgate[j].reshape(1, N_HEADS), ((0, 0), (0, LANES - N_HEADS)))
            qp, kst, vst, kpb, vpb, lfp_pad = _qkv_proj(xp, g_attn[i], wqkv_b, wf_b, bf,
                                                        slot=j, n_slots=n_fox, kv_stack=fox_kv_p)
            fox_kv_p = (kst, vst)
            qs, ks32, vs32, ksb, vsb, lfs = _qkv_proj(xs, g_attn[i], wqkv_b, wf_b, bf)
            lfp = lfp_pad[:, :N_HEADS]
            lfs = lfs[:, :N_HEADS]

            aq, ak, cum_pad = _fox_aug(lfp_pad)
            op = _fox_prompt_attn(qp, aq, kpb, ak, vpb, _fox_first_live_pair(cum_pad, qp, kpb))

            lf_all = jnp.concatenate([
                jnp.transpose(cache_fox_logf[j], (0, 2, 1)),
                jnp.transpose(lfs.reshape(bs, ss, N_HEADS), (0, 2, 1)),
                jnp.zeros((bs, N_HEADS, LANES - ss), F32),
            ], axis=-1).reshape(bs * N_HEADS, past + LANES)
            cum_s = _cumsum_lanes(lf_all).reshape(bs, N_HEADS, past + LANES)
            cq_s = jnp.transpose(cum_s[:, :, past:past + ss], (0, 2, 1))
            osm = _fox_sample_attn(qs, ksb, vsb, cfk, cfv, j, cq_s, cum_s, bs, ss)

            fox_lp.append(lfp.reshape(bp, sp, N_HEADS))
            fox_ks.append(ks32.reshape(bs, ss, N_HEADS, HEAD_DIM))
            fox_vs.append(vs32.reshape(bs, ss, N_HEADS, HEAD_DIM))
            fox_ls.append(lfs.reshape(bs, ss, N_HEADS))
        else:
            qp, kp32, vp32, kpb, vpb = _qkv_proj(xp, g_attn[i], wqkv_b)
            qs, ks32, vs32, ksb, vsb = _qkv_proj(xs, g_attn[i], wqkv_b)
            op = _band_prompt_attn(qp, kpb, vpb, _band_profile(rel_bias[j]))
            osm = _band_sample_attn(qs, ksb, vsb, cbk, cbv, j,
                                    _band_bias(rel_bias[j], samp_q_pos, samp_k_pos), bs, ss)
            band_kp.append(kp32[0, sp - n_keep:].reshape(bp, n_keep, N_HEADS, HEAD_DIM))
            band_vp.append(vp32[0, sp - n_keep:].reshape(bp, n_keep, N_HEADS, HEAD_DIM))
            band_ks.append(ks32.reshape(bs, ss, N_HEADS, HEAD_DIM))
            band_vs.append(vs32.reshape(bs, ss, N_HEADS, HEAD_DIM))

        xp = _out_proj(xp, op, wo_b)
        xs = _out_proj(xs, osm, wo_b)

        gf = g_final if i == depth - 1 else None
        xp, tgp, tvp = _conv_ffn(xp, g_ffn[i], w_in_g, w_in_v, hist0, hist0, cw_g, cw_v, cb_g, cb_v,
                                 w_out_b, sp, gf)
        xs, tgs, tvs = _conv_ffn(xs, g_ffn[i], w_in_g, w_in_v, hs_g, hs_v, cw_g, cw_v, cb_g, cb_v,
                                 w_out_b, ss, gf)
        keep = slice(SUBLANES - (CONV_W - 1), SUBLANES)
        conv_p.append(jnp.concatenate([tgp[-1:, keep, :D_FF], tvp[-1:, keep, :D_FF]], axis=-1))
        conv_s.append(jnp.concatenate([tgs[:, keep, :D_FF], tvs[:, keep, :D_FF]], axis=-1))

    y_prompt = xp.reshape(bp, sp, D_MODEL)
    y_sample = xs.reshape(bs, ss, D_MODEL)
    fox_kp, fox_vp = (a.reshape(n_fox, bp, sp, N_HEADS, HEAD_DIM) for a in fox_kv_p)
    return (y_prompt, y_sample,
            fox_kp, fox_vp, jnp.stack(fox_lp),
            jnp.stack(fox_ks), jnp.stack(fox_vs), jnp.stack(fox_ls),
            jnp.stack(band_kp), jnp.stack(band_vp), jnp.stack(band_ks), jnp.stack(band_vs),
            jnp.stack(conv_p), jnp.stack(conv_s))
```

```python
import functools
import math

import numpy as np
import jax
import jax.numpy as jnp
from jax import lax
from jax.experimental import pallas as pl
from jax.experimental.pallas import tpu as pltpu

F32 = jnp.float32
BF16 = jnp.bfloat16

D_MODEL = 2048
N_HEADS = 16
HEAD_DIM = D_MODEL // N_HEADS
SCALE = 1.0 / math.sqrt(HEAD_DIM)
FOX_DEAD_LOGIT = 128.0
CHUNK = 64
BAND_CHUNKS = 8
BAND_PAST = BAND_CHUNKS * CHUNK
MAX_REL = 2 * CHUNK
D_FF = 5504
CONV_W = 3
EPS = 1e-6
N_MIXERS = 2

LANES = 128
SUBLANES = 8
VMEM_LIMIT = 56 * 1024 * 1024

TM = 512
TN_QKV = 512
TF = 512
FFN_ROW_GROUPS = 2
D_FF_PAD = ((D_FF + TF - 1) // TF) * TF
TK_FOX = 512
TQ_FOX = 2 * TK_FOX
TK_FOX_S = 512
TQ_BAND = 256
TB_BAND = 1024
CUM_BLK = 128


def _cparams(sem):
    return pltpu.CompilerParams(dimension_semantics=sem, vmem_limit_bytes=VMEM_LIMIT)


def _rmsnorm_f32(x, g):
    ms = jnp.mean(x * x, axis=-1, keepdims=True)
    return (x * lax.rsqrt(ms + EPS)) * g


def _qkv_kernel(*refs, with_fgate, n_aliased, q_scale):
    refs = refs[:5 + 2 * with_fgate] + refs[5 + 2 * with_fgate + n_aliased:]
    if with_fgate:
        (x_ref, g_ref, wq_ref, wk_ref, wv_ref, wf_ref, bf_ref,
         q_ref, k32_ref, v32_ref, kb_ref, vb_ref, lf_ref, h_ref) = refs
    else:
        (x_ref, g_ref, wq_ref, wk_ref, wv_ref,
         q_ref, k32_ref, v32_ref, kb_ref, vb_ref, h_ref) = refs

    @pl.when(pl.program_id(1) == 0)
    def _():
        hb = _rmsnorm_f32(x_ref[...], g_ref[...]).astype(BF16)
        h_ref[...] = hb
        if with_fgate:
            z = jnp.dot(hb, wf_ref[...], preferred_element_type=F32) + bf_ref[...]
            lf_ref[...] = jnp.minimum(z, 0.0) - jnp.log1p(jnp.exp(-jnp.abs(z)))

    h = h_ref[...]
    q = jnp.dot(h, wq_ref[...], preferred_element_type=F32)
    q_ref[...] = (q * q_scale).astype(BF16)
    k = jnp.dot(h, wk_ref[...], preferred_element_type=F32)
    k32_ref[...] = k
    kb_ref[...] = k.astype(BF16)
    v = jnp.dot(h, wv_ref[...], preferred_element_type=F32)
    v32_ref[...] = v
    vb_ref[...] = v.astype(BF16)


def _qkv_proj(x, g, w_qkv_b, wf_b=None, bf=None, slot=0, n_slots=1, kv_stack=None, q_scale=SCALE):
    m = x.shape[0]
    tm = min(TM, m)
    nn = D_MODEL // TN_QKV
    with_fgate = wf_b is not None
    stack_tile = pl.BlockSpec((None, tm, TN_QKV), lambda i, n: (slot, i, n))
    stack_shape = jax.ShapeDtypeStruct((n_slots, m, D_MODEL), F32)
    in_specs = [
        pl.BlockSpec((tm, D_MODEL), lambda i, n: (i, 0)),
        pl.BlockSpec((1, D_MODEL), lambda i, n: (0, 0)),
        pl.BlockSpec((D_MODEL, TN_QKV), lambda i, n: (0, n)),
        pl.BlockSpec((D_MODEL, TN_QKV), lambda i, n: (0, n + nn)),
        pl.BlockSpec((D_MODEL, TN_QKV), lambda i, n: (0, n + 2 * nn)),
    ]
    args = [x, g.reshape(1, D_MODEL), w_qkv_b, w_qkv_b, w_qkv_b]
    tile = pl.BlockSpec((tm, TN_QKV), lambda i, n: (i, n))
    out_specs = [tile, stack_tile, stack_tile, tile, tile]
    out_shape = [
        jax.ShapeDtypeStruct((m, D_MODEL), BF16),
        stack_shape,
        stack_shape,
        jax.ShapeDtypeStruct((m, D_MODEL), BF16),
        jax.ShapeDtypeStruct((m, D_MODEL), BF16),
    ]
    if with_fgate:
        in_specs += [
            pl.BlockSpec((D_MODEL, LANES), lambda i, n: (0, 0)),
            pl.BlockSpec((1, LANES), lambda i, n: (0, 0)),
        ]
        args += [wf_b, bf]
        out_specs.append(pl.BlockSpec((tm, LANES), lambda i, n: (i, 0)))
        out_shape.append(jax.ShapeDtypeStruct((m, LANES), F32))
    aliases = {}
    if kv_stack is not None:
        aliases = {len(args): 1, len(args) + 1: 2}
        in_specs += [pl.BlockSpec(memory_space=pl.ANY)] * 2
        args += list(kv_stack)
    return pl.pallas_call(
        functools.partial(_qkv_kernel, with_fgate=with_fgate, n_aliased=len(aliases),
                          q_scale=q_scale),
        grid=(m // tm, nn),
        in_specs=in_specs,
        out_specs=out_specs,
        out_shape=out_shape,
        scratch_shapes=[pltpu.VMEM((tm, D_MODEL), BF16)],
        input_output_aliases=aliases,
        compiler_params=_cparams(("arbitrary", "arbitrary")),
        name="qkv_proj",
    )(*args)


def _cumsum_kernel(x_ref, o_ref, carry_ref):
    @pl.when(pl.program_id(0) == 0)
    def _():
        carry_ref[...] = jnp.zeros_like(carry_ref)

    x = x_ref[...]
    hi = x.astype(BF16)
    r1 = x - hi.astype(F32)
    mid = r1.astype(BF16)
    lo = (r1 - mid.astype(F32)).astype(BF16)
    row = lax.broadcasted_iota(jnp.int32, (CUM_BLK, CUM_BLK), 0)
    col = lax.broadcasted_iota(jnp.int32, (CUM_BLK, CUM_BLK), 1)
    tri = (row <= col).astype(BF16)
    s = jnp.dot(lo, tri, preferred_element_type=F32)
    s = s + jnp.dot(mid, tri, preferred_element_type=F32)
    s = s + jnp.dot(hi, tri, preferred_element_type=F32)
    s = s + carry_ref[...]
    o_ref[...] = s
    carry_ref[...] = jnp.broadcast_to(s[:, CUM_BLK - 1:CUM_BLK], s.shape)


def _cumsum_lanes(x):
    r, t = x.shape
    return pl.pallas_call(
        _cumsum_kernel,
        grid=(t // CUM_BLK,),
        in_specs=[pl.BlockSpec((r, CUM_BLK), lambda i: (0, i))],
        out_specs=pl.BlockSpec((r, CUM_BLK), lambda i: (0, i)),
        out_shape=jax.ShapeDtypeStruct((r, t), F32),
        scratch_shapes=[pltpu.VMEM((r, CUM_BLK), F32)],
        compiler_params=_cparams(("arbitrary",)),
        name="cumsum_time",
    )(x)


def _online_softmax_step(s, v, m_ref, l_ref, acc_ref):
    m_prev = m_ref[...]
    m_new = jnp.maximum(m_prev, jnp.max(s, axis=-1, keepdims=True))
    alpha = jnp.exp(m_prev - m_new)
    p = jnp.exp(s - m_new)
    l_ref[...] = alpha * l_ref[...] + jnp.sum(p, axis=-1, keepdims=True)
    acc_ref[...] = alpha * acc_ref[...] + jnp.dot(p.astype(BF16), v, preferred_element_type=F32)
    m_ref[...] = m_new


def _split3_bf16(x):
    hi = x.astype(BF16)
    r1 = x - hi.astype(F32)
    mid = r1.astype(BF16)
    lo = (r1 - mid.astype(F32)).astype(BF16)
    return jnp.concatenate([hi, mid, lo], axis=-1)


def _fox_aug_placement():
    pq = np.zeros((3 * LANES, D_MODEL), np.float32)
    pk = np.zeros((3 * LANES, D_MODEL), np.float32)
    cq = np.zeros((1, D_MODEL), np.float32)
    ck = np.zeros((1, D_MODEL), np.float32)
    for h in range(N_HEADS):
        for piece in range(3):
            pq[piece * LANES + h, h * HEAD_DIM + piece] = 1.0
            pk[piece * LANES + h, h * HEAD_DIM + 3 + piece] = -1.0
            cq[0, h * HEAD_DIM + 3 + piece] = 1.0
            ck[0, h * HEAD_DIM + piece] = 1.0
    return (jnp.asarray(pq, BF16), jnp.asarray(pk, BF16), jnp.asarray(cq), jnp.asarray(ck))


def _fox_aug_kernel(x_ref, pq_ref, pk_ref, cq_ref, ck_ref, q_ref, k_ref, e_ref,
                    aq_ref, ak_ref, cum_ref, qn_ref, kn_ref, carry_ref):
    @pl.when(pl.program_id(0) == 0)
    def _():
        carry_ref[...] = jnp.zeros_like(carry_ref)
        qn_ref[...] = jnp.zeros_like(qn_ref)
        kn_ref[...] = jnp.zeros_like(kn_ref)

    for a_ref, n_ref in ((q_ref, qn_ref), (k_ref, kn_ref)):
        a = a_ref[...].astype(F32)
        n2 = jnp.dot((a * a).astype(BF16), e_ref[...], preferred_element_type=F32)
        n_ref[...] = jnp.maximum(n_ref[...], jnp.max(n2, axis=0, keepdims=True))

    tb = x_ref.shape[0]
    row = lax.broadcasted_iota(jnp.int32, (tb, tb), 0)
    col = lax.broadcasted_iota(jnp.int32, (tb, tb), 1)
    tri = (col <= row).astype(BF16)
    x3 = _split3_bf16(x_ref[...])
    c3 = jnp.dot(tri, x3, preferred_element_type=F32)
    cum = (c3[:, 2 * LANES:] + c3[:, LANES:2 * LANES]) + c3[:, :LANES] + carry_ref[...]
    carry_ref[...] = cum[tb - 1:tb, :]
    cum_ref[...] = cum
    cum3 = _split3_bf16(cum)
    aq = jnp.dot(cum3, pq_ref[...], preferred_element_type=F32) + cq_ref[...]
    ak = jnp.dot(cum3, pk_ref[...], preferred_element_type=F32) + ck_ref[...]
    aq_ref[...] = aq.astype(BF16)
    ak_ref[...] = ak.astype(BF16)


def _fox_aug(logf_pad, q, k):
    t_len = logf_pad.shape[0]
    tb = 2 * LANES
    pq, pk, cq, ck = _fox_aug_placement()
    head_of_col = np.arange(D_MODEL)[:, None] // HEAD_DIM == np.arange(LANES)[None, :]
    const = lambda shape: pl.BlockSpec(shape, lambda i: (0, 0))
    rows = lambda width: pl.BlockSpec((tb, width), lambda i: (i, 0))
    return pl.pallas_call(
        _fox_aug_kernel,
        grid=(t_len // tb,),
        in_specs=[
            rows(LANES),
            const((3 * LANES, D_MODEL)), const((3 * LANES, D_MODEL)),
            const((1, D_MODEL)), const((1, D_MODEL)),
            rows(D_MODEL), rows(D_MODEL), const((D_MODEL, LANES)),
        ],
        out_specs=[rows(D_MODEL), rows(D_MODEL), rows(LANES), const((1, LANES)), const((1, LANES))],
        out_shape=[jax.ShapeDtypeStruct((t_len, D_MODEL), BF16)] * 2
        + [jax.ShapeDtypeStruct((t_len, LANES), F32)]
        + [jax.ShapeDtypeStruct((1, LANES), F32)] * 2,
        scratch_shapes=[pltpu.VMEM((1, LANES), F32)],
        compiler_params=_cparams(("arbitrary",)),
        name="fox_aug",
    )(logf_pad, pq, pk, cq, ck, q, k, jnp.asarray(head_of_col, BF16))


def _fox_prompt_kernel(start_ref, q_ref, aq_ref, k_ref, ak_ref, v_ref, o_ref,
                       s0_ref, s1_ref, p0_ref, p1_ref, a0_ref, a1_ref, m_ref, acc_ref):
    qi = pl.program_id(1)
    j0 = start_ref[pl.program_id(0), qi]
    tq, tk = TQ_FOX, TK_FOX
    dn = (((1,), (1,)), ((), ()))
    q = jnp.concatenate([q_ref[...], aq_ref[...]], axis=1)
    ones_col = (lax.broadcasted_iota(jnp.int32, (tk, HEAD_DIM), 1) == 0).astype(BF16)

    def scores(ki, s_ref):
        ks = pl.multiple_of(ki * tk, tk)
        k = jnp.concatenate([k_ref[pl.ds(ks, tk), :], ak_ref[pl.ds(ks, tk), :]], axis=1)
        s_ref[...] = lax.dot_general(q, k, dn, preferred_element_type=F32)

    def accumulate(ki, p_ref, a_ref):
        ks = pl.multiple_of(ki * tk, tk)
        v = jnp.concatenate([v_ref[pl.ds(ks, tk), :], ones_col], axis=1)
        pv = jnp.dot(p_ref[...], v, preferred_element_type=F32)
        acc_ref[...] = a_ref[...] * acc_ref[...] + pv

    def softmax(s, p_ref, a_ref):
        m_prev = m_ref[...]
        m_new = jnp.maximum(m_prev, jnp.max(s, axis=-1, keepdims=True))
        a_ref[...] = jnp.exp(m_prev - m_new)
        p_ref[...] = jnp.exp(s - m_new).astype(BF16)
        m_ref[...] = m_new

    m_ref[...] = jnp.full_like(m_ref, -jnp.inf)
    acc_ref[...] = jnp.zeros_like(acc_ref)
    p1_ref[...] = jnp.zeros_like(p1_ref)
    a1_ref[...] = jnp.ones_like(a1_ref)
    scores(2 * j0, s0_ref)

    def body(j, carry):
        scores(2 * j + 1, s1_ref)
        accumulate(jnp.maximum(2 * j - 1, 0), p1_ref, a1_ref)
        softmax(s0_ref[...], p0_ref, a0_ref)
        scores(2 * j + 2, s0_ref)
        accumulate(2 * j, p0_ref, a0_ref)
        softmax(s1_ref[...], p1_ref, a1_ref)
        return carry

    lax.fori_loop(j0, qi, body, 0)

    row = lax.broadcasted_iota(jnp.int32, (tq, tk), 0)
    col = lax.broadcasted_iota(jnp.int32, (tq, tk), 1)
    scores(2 * qi + 1, s1_ref)
    accumulate(jnp.maximum(2 * qi - 1, 0), p1_ref, a1_ref)
    softmax(jnp.where(col <= row, s0_ref[...], -jnp.inf), p0_ref, a0_ref)
    accumulate(2 * qi, p0_ref, a0_ref)
    softmax(jnp.where(col + tk <= row, s1_ref[...], -jnp.inf), p1_ref, a1_ref)
    accumulate(2 * qi + 1, p1_ref, a1_ref)
    acc = acc_ref[...]
    o_ref[...] = (acc[:, :HEAD_DIM] / acc[:, HEAD_DIM:HEAD_DIM + 1]).astype(BF16)


def _fox_first_live_pair(cum_pad, qn2, kn2):
    nq = cum_pad.shape[0] // TQ_FOX
    slack = 2.0 * 1.01 * jnp.sqrt(qn2[0, :N_HEADS] * kn2[0, :N_HEADS])
    cum = cum_pad[:, :N_HEADS]
    c_tile_start = cum[0::TQ_FOX]
    c_pair_end = cum[TQ_FOX - 1::TQ_FOX]
    bound = c_tile_start[:, None, :] - c_pair_end[None, :, :] + slack[None, None, :]
    before = np.arange(nq)[None, :, None] < np.arange(nq)[:, None, None]
    dead = jnp.logical_and(bound < -FOX_DEAD_LOGIT, before)
    lead = jnp.cumprod(dead.astype(jnp.int32), axis=1)
    return jnp.transpose(jnp.sum(lead, axis=1)).astype(jnp.int32)


def _fox_prompt_attn(q, aq, k, ak, v, first_pair):
    t_len = q.shape[0]
    tq, tk = TQ_FOX, TK_FOX
    assert tq == 2 * tk and t_len % tq == 0
    tile = pl.BlockSpec((tq, HEAD_DIM), lambda h, i, start: (i, h))
    whole = pl.BlockSpec((t_len, HEAD_DIM), lambda h, i, start: (0, h))
    return pl.pallas_call(
        _fox_prompt_kernel,
        grid_spec=pltpu.PrefetchScalarGridSpec(
            num_scalar_prefetch=1,
            grid=(N_HEADS, t_len // tq),
            in_specs=[tile, tile, whole, whole, whole],
            out_specs=tile,
            scratch_shapes=[
                pltpu.VMEM((tq, tk), F32), pltpu.VMEM((tq, tk), F32),
                pltpu.VMEM((tq, tk), BF16), pltpu.VMEM((tq, tk), BF16),
                pltpu.VMEM((tq, 1), F32), pltpu.VMEM((tq, 1), F32),
                pltpu.VMEM((tq, 1), F32),
                pltpu.VMEM((tq, 2 * HEAD_DIM), F32),
            ],
        ),
        out_shape=jax.ShapeDtypeStruct((t_len, D_MODEL), BF16),
        compiler_params=_cparams(("arbitrary", "arbitrary")),
        name="fox_prompt_attn",
    )(first_pair, q, aq, k, ak, v)


def _fox_sample_kernel(q_ref, kn_ref, vn_ref, kc_ref, vc_ref, cq_ref, ckc_ref, ckn_ref,
                       o_ref, s_ref, sn_ref, m_ref, acc_ref, *, n_new, tk):
    ki = pl.program_id(1)
    nk = pl.num_programs(1)
    dn = (((1,), (1,)), ((), ()))

    @pl.when(ki == 0)
    def _():
        m_ref[...] = jnp.full_like(m_ref, -jnp.inf)
        acc_ref[...] = jnp.zeros_like(acc_ref)

    def head_rows(h):
        return slice(h * n_new, (h + 1) * n_new)

    def head_cols(h):
        return slice(h * HEAD_DIM, (h + 1) * HEAD_DIM)

    def softmax_update(s):
        m_prev = m_ref[...]
        m_new = jnp.maximum(m_prev, jnp.max(s, axis=-1, keepdims=True))
        m_ref[...] = m_new
        return jnp.exp(m_prev - m_new), jnp.exp(s - m_new).astype(BF16)

    def accumulate(alpha, p, values):
        n_keys = p.shape[1]
        ones_col = (lax.broadcasted_iota(jnp.int32, (n_keys, HEAD_DIM), 1) == 0).astype(BF16)
        for h in range(N_HEADS):
            r = head_rows(h)
            pv = jnp.dot(p[r], jnp.concatenate([values(h), ones_col], axis=1),
                         preferred_element_type=F32)
            acc_ref[r, :] = alpha[r] * acc_ref[r, :] + pv

    for h in range(N_HEADS):
        kc = kc_ref[0, pl.ds(h, tk, stride=N_HEADS), :].astype(BF16)
        s = lax.dot_general(q_ref[:, head_cols(h)], kc, dn, preferred_element_type=F32)
        s_ref[head_rows(h), :] = (s + cq_ref[0, :, h:h + 1]) - ckc_ref[0, h:h + 1, :]
    alpha, p = softmax_update(s_ref[...])
    accumulate(alpha, p, lambda h: vc_ref[0, pl.ds(h, tk, stride=N_HEADS), :].astype(BF16))

    @pl.when(ki == nk - 1)
    def _():
        row = lax.broadcasted_iota(jnp.int32, (n_new, n_new), 0)
        col = lax.broadcasted_iota(jnp.int32, (n_new, n_new), 1)
        for h in range(N_HEADS):
            s = lax.dot_general(q_ref[:, head_cols(h)], kn_ref[:, head_cols(h)], dn,
                                preferred_element_type=F32)
            s = (s + cq_ref[0, :, h:h + 1]) - ckn_ref[0, h:h + 1, 0:n_new]
            sn_ref[head_rows(h), :] = jnp.where(col <= row, s, -jnp.inf)
        alpha, p = softmax_update(sn_ref[...])
        accumulate(alpha, p, lambda h: vn_ref[:, head_cols(h)])
        for h in range(N_HEADS):
            acc = acc_ref[head_rows(h), :]
            o_ref[:, head_cols(h)] = (acc[:, :HEAD_DIM] / acc[:, HEAD_DIM:HEAD_DIM + 1]).astype(BF16)


def _fox_sample_attn(q, kn, vn, cache_k, cache_v, layer, cq_bth, cum_bht, n_streams, n_new):
    past = cache_k.shape[1] // N_HEADS
    tk = TK_FOX_S
    first = layer * n_streams
    rows = N_HEADS * n_new
    return pl.pallas_call(
        functools.partial(_fox_sample_kernel, n_new=n_new, tk=tk),
        grid=(n_streams, past // tk),
        in_specs=[
            pl.BlockSpec((n_new, D_MODEL), lambda b, i: (b, 0)),
            pl.BlockSpec((n_new, D_MODEL), lambda b, i: (b, 0)),
            pl.BlockSpec((n_new, D_MODEL), lambda b, i: (b, 0)),
            pl.BlockSpec((1, tk * N_HEADS, HEAD_DIM), lambda b, i: (first + b, i, 0)),
            pl.BlockSpec((1, tk * N_HEADS, HEAD_DIM), lambda b, i: (first + b, i, 0)),
            pl.BlockSpec((1, n_new, N_HEADS), lambda b, i: (b, 0, 0)),
            pl.BlockSpec((1, N_HEADS, tk), lambda b, i: (b, 0, i)),
            pl.BlockSpec((1, N_HEADS, LANES), lambda b, i: (b, 0, past // LANES)),
        ],
        out_specs=pl.BlockSpec((n_new, D_MODEL), lambda b, i: (b, 0)),
        out_shape=jax.ShapeDtypeStruct((n_streams * n_new, D_MODEL), BF16),
        scratch_shapes=[
            pltpu.VMEM((rows, tk), F32),
            pltpu.VMEM((rows, n_new), F32),
            pltpu.VMEM((rows, 1), F32),
            pltpu.VMEM((rows, 2 * HEAD_DIM), F32),
        ],
        compiler_params=_cparams(("arbitrary", "arbitrary")),
        name="fox_sample_attn",
    )(q, kn, vn, cache_k, cache_v, cq_bth, cum_bht, cum_bht)


def _band_profile(rel_table, d0, n_rows, width):
    dist = d0 + (n_rows - 1) - np.arange(width)
    idx = np.clip(dist, -MAX_REL, MAX_REL) + MAX_REL
    return rel_table.astype(F32)[:, idx].reshape(N_HEADS, 1, width)


def _toeplitz_rows(g, n_rows, n_cols):
    width = g.shape[-1]
    assert n_cols + n_rows - 1 <= width and width % LANES == 0
    r = pltpu.roll(jnp.broadcast_to(g, (n_rows, width)), width - (n_rows - 1), axis=1,
                   stride=1, stride_axis=0)
    return r[:, :n_cols]


def _softmax_rows(s):
    m = jnp.max(s, axis=-1, keepdims=True)
    p = jnp.exp(s - m)
    return p, jnp.sum(p, axis=-1, keepdims=True)


def _band_prompt_kernel(q_ref, kp_ref, kc_ref, vp_ref, vc_ref, g_ref, o_ref, b_ref):
    qi = pl.program_id(1)
    t = TQ_BAND

    @pl.when(qi == 0)
    def _():
        qc = lax.broadcasted_iota(jnp.int32, (t, 3 * t), 0) // CHUNK
        kc = lax.broadcasted_iota(jnp.int32, (t, 3 * t), 1) // CHUNK
        ok = (kc >= qc) & (kc <= qc + BAND_CHUNKS)
        b_ref[...] = jnp.where(ok, _toeplitz_rows(g_ref[0], t, 3 * t), -jnp.inf)

    k = jnp.concatenate([kp_ref[...], kc_ref[...]], axis=0)
    v = jnp.concatenate([vp_ref[...], vc_ref[...]], axis=0)
    col = lax.broadcasted_iota(jnp.int32, (t, 3 * t), 1)
    for sub in range(TB_BAND // t):
        lo = sub * t
        q = q_ref[lo:lo + t, :]
        s = lax.dot_general(q, k[lo:lo + 3 * t], (((1,), (1,)), ((), ())),
                            preferred_element_type=F32) + b_ref[...]
        if lo < BAND_PAST:
            first_valid = jnp.where(qi == 0, BAND_PAST - lo, 0)
            s = jnp.where(col >= first_valid, s, -jnp.inf)
        p, l = _softmax_rows(s)
        o = jnp.dot(p.astype(BF16), v[lo:lo + 3 * t], preferred_element_type=F32)
        o_ref[lo:lo + t, :] = (o / l).astype(BF16)


def _band_prompt_attn(q, k, v, profile):
    t_len = q.shape[0]
    t = TQ_BAND
    tb = TB_BAND
    per = tb // BAND_PAST
    assert BAND_PAST == 2 * t and t % CHUNK == 0 and tb % BAND_PAST == 0 and t_len % tb == 0
    cur = pl.BlockSpec((tb, HEAD_DIM), lambda h, i: (i, h))
    prev = pl.BlockSpec((BAND_PAST, HEAD_DIM), lambda h, i: (jnp.maximum(i * per - 1, 0), h))
    return pl.pallas_call(
        _band_prompt_kernel,
        grid=(N_HEADS, t_len // tb),
        in_specs=[cur, prev, cur, prev, cur,
                  pl.BlockSpec((1, 1, 4 * t), lambda h, i: (h, 0, 0))],
        out_specs=cur,
        out_shape=jax.ShapeDtypeStruct((t_len, D_MODEL), BF16),
        scratch_shapes=[pltpu.VMEM((t, 3 * t), F32)],
        compiler_params=_cparams(("arbitrary", "arbitrary")),
        name="band_prompt_attn",
    )(q, k, k, v, v, profile)


def _band_sample_kernel(q_ref, kn_ref, vn_ref, kc_ref, vc_ref, g_ref, mask_ref, o_ref, b_ref, *,
                        n_cache):
    n_new = q_ref.shape[0]

    @pl.when(pl.program_id(0) == 0)
    def _():
        for h in range(N_HEADS):
            b_ref[h] = _toeplitz_rows(g_ref[h], n_new, n_cache + n_new) + mask_ref[...]

    for h in range(N_HEADS):
        hs = slice(h * HEAD_DIM, (h + 1) * HEAD_DIM)
        q = q_ref[:, hs]
        kc = kc_ref[0, pl.ds(h, n_cache, stride=N_HEADS), :].astype(BF16)
        vc = vc_ref[0, pl.ds(h, n_cache, stride=N_HEADS), :].astype(BF16)
        dn = (((1,), (1,)), ((), ()))
        s1 = lax.dot_general(q, kc, dn, preferred_element_type=F32) + b_ref[h, :, 0:n_cache]
        s2 = lax.dot_general(q, kn_ref[:, hs], dn, preferred_element_type=F32) + b_ref[h, :, n_cache:]
        m = jnp.maximum(jnp.max(s1, axis=-1, keepdims=True), jnp.max(s2, axis=-1, keepdims=True))
        p1 = jnp.exp(s1 - m)
        p2 = jnp.exp(s2 - m)
        l = jnp.sum(p1, axis=-1, keepdims=True) + jnp.sum(p2, axis=-1, keepdims=True)
        o = jnp.dot(p1.astype(BF16), vc, preferred_element_type=F32)
        o = o + jnp.dot(p2.astype(BF16), vn_ref[:, hs], preferred_element_type=F32)
        o_ref[:, hs] = (o / l).astype(BF16)


def _band_sample_attn(q, kn, vn, cache_k, cache_v, layer, rel_table, past, n_streams, n_new):
    n_cache = cache_k.shape[1] // N_HEADS
    n_keys = n_cache + n_new
    first = layer * n_streams
    q_pos = past + np.arange(n_new)
    k_pos = past - n_cache + np.arange(n_keys)
    qc, kc = q_pos[:, None] // CHUNK, k_pos[None, :] // CHUNK
    ok = (k_pos[None, :] >= 0) & (kc <= qc) & (kc >= qc - BAND_CHUNKS)
    mask = jnp.asarray(np.where(ok, 0.0, -np.inf), F32)
    width = -(-(n_keys + n_new - 1) // LANES) * LANES
    profile = _band_profile(rel_table, n_cache, n_new, width)
    return pl.pallas_call(
        functools.partial(_band_sample_kernel, n_cache=n_cache),
        grid=(n_streams,),
        in_specs=[
            pl.BlockSpec((n_new, D_MODEL), lambda b: (b, 0)),
            pl.BlockSpec((n_new, D_MODEL), lambda b: (b, 0)),
            pl.BlockSpec((n_new, D_MODEL), lambda b: (b, 0)),
            pl.BlockSpec((1, n_cache * N_HEADS, HEAD_DIM), lambda b: (first + b, 0, 0)),
            pl.BlockSpec((1, n_cache * N_HEADS, HEAD_DIM), lambda b: (first + b, 0, 0)),
            pl.BlockSpec((N_HEADS, 1, width), lambda b: (0, 0, 0)),
            pl.BlockSpec((n_new, n_keys), lambda b: (0, 0)),
        ],
        out_specs=pl.BlockSpec((n_new, D_MODEL), lambda b: (b, 0)),
        out_shape=jax.ShapeDtypeStruct((n_streams * n_new, D_MODEL), BF16),
        scratch_shapes=[pltpu.VMEM((N_HEADS, n_new, n_keys), F32)],
        compiler_params=_cparams(("arbitrary",)),
        name="band_sample_attn",
    )(q, kn, vn, cache_k, cache_v, profile, mask)


def _oproj_kernel(x_ref, a_ref, w_ref, o_ref):
    o_ref[...] = x_ref[...] + jnp.dot(a_ref[...], w_ref[...], preferred_element_type=F32)


def _out_proj(x, a, w_o_b):
    m = x.shape[0]
    tm = min(TM, m)
    return pl.pallas_call(
        _oproj_kernel,
        grid=(m // tm,),
        in_specs=[
            pl.BlockSpec((tm, D_MODEL), lambda i: (i, 0)),
            pl.BlockSpec((tm, D_MODEL), lambda i: (i, 0)),
            pl.BlockSpec((D_MODEL, D_MODEL), lambda i: (0, 0)),
        ],
        out_specs=pl.BlockSpec((tm, D_MODEL), lambda i: (i, 0)),
        out_shape=jax.ShapeDtypeStruct((m, D_MODEL), F32),
        compiler_params=_cparams(("arbitrary",)),
        name="out_proj",
    )(x, a, w_o_b)


def _ffn_kernel(*refs, stream_len, tm, final_norm):
    if final_norm:
        (x_ref, g_ref, wg_ref, wv_ref, hg_ref, hv_ref, cwg_ref, cwv_ref, cbg_ref, cbv_ref,
         wo_ref, gf_ref, o_ref, tg_ref, tv_ref, h_ref, pg_ref, pv_ref, u_ref) = refs
    else:
        (x_ref, g_ref, wg_ref, wv_ref, hg_ref, hv_ref, cwg_ref, cwv_ref, cbg_ref, cbv_ref,
         wo_ref, o_ref, tg_ref, tv_ref, h_ref, pg_ref, pv_ref, u_ref) = refs
        gf_ref = None
    i = pl.program_id(0)
    f = pl.program_id(1)
    nf = pl.num_programs(1)
    streams_per_tile = max(tm // stream_len, 1)
    tiles_per_stream = max(stream_len // tm, 1)

    @pl.when(f == 0)
    def _():
        h_ref[...] = _rmsnorm_f32(x_ref[...], g_ref[...]).astype(BF16)
        o_ref[...] = x_ref[...]

    n_groups = FFN_ROW_GROUPS
    rg = tm // n_groups
    assert rg % SUBLANES == 0 and (streams_per_tile == 1 or streams_per_tile % n_groups == 0)
    row = lax.broadcasted_iota(jnp.int32, (rg, TF), 0)
    if streams_per_tile > 1:
        row = row % stream_len
    halves = ((wg_ref, hg_ref, cwg_ref, cbg_ref, pg_ref, tg_ref),
              (wv_ref, hv_ref, cwv_ref, cbv_ref, pv_ref, tv_ref))

    class _Up:
        def __init__(self, g, half):
            self.ref = u_ref.at[half * n_groups + g]
            self.ref[...] = jnp.dot(h_ref[g * rg:(g + 1) * rg, :], halves[half][0][...],
                                    preferred_element_type=F32)

        def __getitem__(self, idx):
            return self.ref[idx]

        def reshape(self, *shape):
            return self.ref[...].reshape(*shape)

    if streams_per_tile == 1:
        @pl.when((i % tiles_per_stream) == 0)
        def _():
            for (_, hist_ref, _, _, prev_ref, _) in halves:
                prev_ref[f, SUBLANES - (CONV_W - 1):SUBLANES, :] = hist_ref[0]

    ups = [[_Up(0, half) for half in range(2)]]

    def conv(g, half):
        _, hist_ref, cw_ref, cb_ref, prev_ref, _ = halves[half]
        u = ups[g][half][...]
        if streams_per_tile > 1:
            spg = streams_per_tile // n_groups
            def per_row(r):
                hr = hist_ref[g * spg:(g + 1) * spg, r:r + 1, :]
                return jnp.broadcast_to(hr, (spg, stream_len, TF)).reshape(rg, TF)
            p0, p1 = per_row(0), per_row(1)
        elif g == 0:
            p0 = prev_ref[f, SUBLANES - 2:SUBLANES - 1, :]
            p1 = prev_ref[f, SUBLANES - 1:SUBLANES, :]
        else:
            p0 = ups[g - 1][half][rg - 2:rg - 1, :]
            p1 = ups[g - 1][half][rg - 1:rg, :]
        u1 = jnp.where(row == 0, p1, pltpu.roll(u, 1, axis=0))
        u2 = jnp.where(row == 0, p0, jnp.where(row == 1, p1, pltpu.roll(u, 2, axis=0)))
        c = cb_ref[...] + cw_ref[0:1, :] * u2
        c = c + cw_ref[1:2, :] * u1
        return c + cw_ref[2:3, :] * u

    for g in range(n_groups):
        cg = conv(g, 0)
        cv = conv(g, 1)
        a = ((cg * jax.nn.sigmoid(cg)) * cv).astype(BF16)
        if g + 1 < n_groups:
            ups.append([_Up(g + 1, half) for half in range(2)])
        o_ref[g * rg:(g + 1) * rg, :] += jnp.dot(a, wo_ref[...], preferred_element_type=F32)

    for half, (_, _, _, _, prev_ref, tail_ref) in enumerate(halves):
        if streams_per_tile > 1:
            spg = streams_per_tile // n_groups
            for g in range(n_groups):
                u3 = ups[g][half].reshape(spg, stream_len, TF)
                tail_ref[g * spg:(g + 1) * spg] = u3[:, stream_len - SUBLANES:, :]
        else:
            last = ups[n_groups - 1][half][rg - SUBLANES:, :]
            prev_ref[f] = last
            tail_ref[0] = last

    if final_norm:
        @pl.when(f == nf - 1)
        def _():
            o_ref[...] = _rmsnorm_f32(o_ref[...], gf_ref[...])


def _conv_ffn(x, g, w_in_g, w_in_v, hist_g, hist_v, cw_g, cw_v, cb_g, cb_v, w_out_b, stream_len,
              g_final=None):
    m = x.shape[0]
    tm = min(TM, m)
    nf = D_FF_PAD // TF
    n_tiles = m // tm
    final_norm = g_final is not None
    if stream_len >= tm:
        assert stream_len % tm == 0
        tiles_per_stream = stream_len // tm
        hist_spec = pl.BlockSpec((1, CONV_W - 1, TF), lambda i, f: (i // tiles_per_stream, 0, f))
        tail_rows = n_tiles
        tail_spec = pl.BlockSpec((1, SUBLANES, TF), lambda i, f: (i, 0, f))
    else:
        assert tm % stream_len == 0 and stream_len % SUBLANES == 0 and n_tiles == 1
        spt = tm // stream_len
        hist_spec = pl.BlockSpec((spt, CONV_W - 1, TF), lambda i, f: (i, 0, f))
        tail_rows = m // stream_len
        tail_spec = pl.BlockSpec((spt, SUBLANES, TF), lambda i, f: (i, 0, f))
    row_tile = pl.BlockSpec((tm, D_MODEL), lambda i, f: (i, 0))
    gain = pl.BlockSpec((1, D_MODEL), lambda i, f: (0, 0))
    w_in_spec = pl.BlockSpec((D_MODEL, TF), lambda i, f: (0, f))
    cw_spec = pl.BlockSpec((CONV_W, TF), lambda i, f: (0, f))
    cb_spec = pl.BlockSpec((1, TF), lambda i, f: (0, f))
    in_specs = [row_tile, gain, w_in_spec, w_in_spec, hist_spec, hist_spec, cw_spec, cw_spec,
                cb_spec, cb_spec, pl.BlockSpec((TF, D_MODEL), lambda i, f: (f, 0))]
    args = [x, g.reshape(1, D_MODEL), w_in_g, w_in_v, hist_g, hist_v, cw_g, cw_v, cb_g, cb_v, w_out_b]
    if final_norm:
        in_specs.append(gain)
        args.append(g_final.reshape(1, D_MODEL))
    tail_shape = jax.ShapeDtypeStruct((tail_rows, SUBLANES, D_FF_PAD), F32)
    return pl.pallas_call(
        functools.partial(_ffn_kernel, stream_len=stream_len, tm=tm, final_norm=final_norm),
        grid=(n_tiles, nf),
        in_specs=in_specs,
        out_specs=[row_tile, tail_spec, tail_spec],
        out_shape=[jax.ShapeDtypeStruct((m, D_MODEL), F32), tail_shape, tail_shape],
        scratch_shapes=[
            pltpu.VMEM((tm, D_MODEL), BF16),
            pltpu.VMEM((nf, SUBLANES, TF), F32),
            pltpu.VMEM((nf, SUBLANES, TF), F32),
            pltpu.VMEM((2 * FFN_ROW_GROUPS, tm // FFN_ROW_GROUPS, TF), F32),
        ],
        compiler_params=_cparams(("arbitrary", "arbitrary")),
        name="conv_ffn",
    )(*args)


def _pad_ff(a):
    pad = [(0, 0)] * (a.ndim - 1) + [(0, D_FF_PAD - D_FF)]
    return jnp.pad(a, pad)


def _split_ff(a):
    return _pad_ff(a[..., :D_FF]), _pad_ff(a[..., D_FF:])


def kernel(x_prompt, x_sample, cache_fox_k, cache_fox_v, cache_fox_logf, cache_band_k, cache_band_v,
           state_conv, g_attn, w_qkv, w_fgate, b_fgate, rel_bias, w_o, g_ffn, w_in, conv_w, conv_b,
           w_out, g_final):
    depth = g_attn.shape[0]
    bp, sp, _ = x_prompt.shape
    bs, ss, _ = x_sample.shape
    past = cache_fox_k.shape[2]
    n_band_cache = cache_band_k.shape[2]
    n_keep = min(BAND_PAST, sp)
    assert bp == 1

    xp = x_prompt.reshape(bp * sp, D_MODEL)
    xs = x_sample.reshape(bs * ss, D_MODEL)

    n_fox = cache_fox_k.shape[0]
    n_band = cache_band_k.shape[0]
    cfk = cache_fox_k.reshape(n_fox * bs, past * N_HEADS, HEAD_DIM)
    cfv = cache_fox_v.reshape(n_fox * bs, past * N_HEADS, HEAD_DIM)
    cbk = cache_band_k.reshape(n_band * bs, n_band_cache * N_HEADS, HEAD_DIM)
    cbv = cache_band_v.reshape(n_band * bs, n_band_cache * N_HEADS, HEAD_DIM)

    hist0 = jnp.zeros((bp, CONV_W - 1, D_FF_PAD), F32)

    fox_lp, fox_ks, fox_vs, fox_ls = [], [], [], []
    band_kp, band_vp, band_ks, band_vs = [], [], [], []
    conv_p, conv_s = [], []
    fox_kv_p = None

    for i in range(depth):
        j = i // N_MIXERS
        is_fox = (i % N_MIXERS == 0)
        wqkv_b = w_qkv[i].astype(BF16)
        wo_b = w_o[i].astype(BF16)
        w_in_g, w_in_v = _split_ff(w_in[i].astype(BF16))
        cw_g, cw_v = _split_ff(conv_w[i])
        cb_g, cb_v = _split_ff(conv_b[i].reshape(1, 2 * D_FF))
        w_out_b = jnp.pad(w_out[i].astype(BF16), ((0, D_FF_PAD - D_FF), (0, 0)))
        hs_g, hs_v = _split_ff(state_conv[i])

        if is_fox:
            wf_b = jnp.pad(w_fgate[j].astype(BF16), ((0, 0), (0, LANES - N_HEADS)))
            bf = jnp.pad(b_fgate[j].reshape(1, N_HEADS), ((0, 0), (0, LANES - N_HEADS)))
            qp, kst, vst, kpb, vpb, lfp_pad = _qkv_proj(xp, g_attn[i], wqkv_b, wf_b, bf,
                                                        slot=j, n_slots=n_fox, kv_stack=fox_kv_p)
            fox_kv_p = (kst, vst)
            qs, ks32, vs32, ksb, vsb, lfs = _qkv_proj(xs, g_attn[i], wqkv_b, wf_b, bf)
            lfp = lfp_pad[:, :N_HEADS]
            lfs = lfs[:, :N_HEADS]

            aq, ak, cum_pad, qn2, kn2 = _fox_aug(lfp_pad, qp, kpb)
            op = _fox_prompt_attn(qp, aq, kpb, ak, vpb, _fox_first_live_pair(cum_pad, qn2, kn2))

            lf_all = jnp.concatenate([
                jnp.transpose(cache_fox_logf[j], (0, 2, 1)),
                jnp.transpose(lfs.reshape(bs, ss, N_HEADS), (0, 2, 1)),
                jnp.zeros((bs, N_HEADS, LANES - ss), F32),
            ], axis=-1).reshape(bs * N_HEADS, past + LANES)
            cum_s = _cumsum_lanes(lf_all).reshape(bs, N_HEADS, past + LANES)
            cq_s = jnp.transpose(cum_s[:, :, past:past + ss], (0, 2, 1))
            osm = _fox_sample_attn(qs, ksb, vsb, cfk, cfv, j, cq_s, cum_s, bs, ss)

            fox_lp.append(lfp.reshape(bp, sp, N_HEADS))
            fox_ks.append(ks32.reshape(bs, ss, N_HEADS, HEAD_DIM))
            fox_vs.append(vs32.reshape(bs, ss, N_HEADS, HEAD_DIM))
            fox_ls.append(lfs.reshape(bs, ss, N_HEADS))
        else:
            qp, kp32, vp32, kpb, vpb = _qkv_proj(xp, g_attn[i], wqkv_b)
            qs, ks32, vs32, ksb, vsb = _qkv_proj(xs, g_attn[i], wqkv_b)
            op = _band_prompt_attn(qp, kpb, vpb,
                                   _band_profile(rel_bias[j], BAND_PAST, TQ_BAND, 4 * TQ_BAND))
            osm = _band_sample_attn(qs, ksb, vsb, cbk, cbv, j, rel_bias[j], past, bs, ss)
            last_rows = slice(sp - n_keep, sp)
            band_kp.append(kp32[0, last_rows].reshape(bp, n_keep, N_HEADS, HEAD_DIM))
            band_vp.append(vp32[0, last_rows].reshape(bp, n_keep, N_HEADS, HEAD_DIM))
            band_ks.append(ks32.reshape(bs, ss, N_HEADS, HEAD_DIM))
            band_vs.append(vs32.reshape(bs, ss, N_HEADS, HEAD_DIM))

        xp = _out_proj(xp, op, wo_b)
        xs = _out_proj(xs, osm, wo_b)

        gf = g_final if i == depth - 1 else None
        xp, tgp, tvp = _conv_ffn(xp, g_ffn[i], w_in_g, w_in_v, hist0, hist0, cw_g, cw_v, cb_g, cb_v,
                                 w_out_b, sp, gf)
        xs, tgs, tvs = _conv_ffn(xs, g_ffn[i], w_in_g, w_in_v, hs_g, hs_v, cw_g, cw_v, cb_g, cb_v,
                                 w_out_b, ss, gf)
        keep = slice(SUBLANES - (CONV_W - 1), SUBLANES)
        conv_p.append(jnp.concatenate([tgp[-1:, keep, :D_FF], tvp[-1:, keep, :D_FF]], axis=-1))
        conv_s.append(jnp.concatenate([tgs[:, keep, :D_FF], tvs[:, keep, :D_FF]], axis=-1))

    y_prompt = xp.reshape(bp, sp, D_MODEL)
    y_sample = xs.reshape(bs, ss, D_MODEL)
    fox_kp, fox_vp = (a.reshape(n_fox, bp, sp, N_HEADS, HEAD_DIM) for a in fox_kv_p)
    return (y_prompt, y_sample,
            fox_kp, fox_vp, jnp.stack(fox_lp),
            jnp.stack(fox_ks), jnp.stack(fox_vs), jnp.stack(fox_ls),
            jnp.stack(band_kp), jnp.stack(band_vp), jnp.stack(band_ks), jnp.stack(band_vs),
            jnp.stack(conv_p), jnp.stack(conv_s))
```

```python
import functools
import math

import numpy as np
import jax
import jax.numpy as jnp
from jax import lax
from jax.experimental import pallas as pl
from jax.experimental.pallas import tpu as pltpu

F32 = jnp.float32
BF16 = jnp.bfloat16

D_MODEL = 2048
N_HEADS = 16
HEAD_DIM = D_MODEL // N_HEADS
SCALE = 1.0 / math.sqrt(HEAD_DIM)
FOX_DEAD_LOGIT = 128.0
CHUNK = 64
BAND_CHUNKS = 8
BAND_PAST = BAND_CHUNKS * CHUNK
MAX_REL = 2 * CHUNK
D_FF = 5504
CONV_W = 3
EPS = 1e-6
N_MIXERS = 2

LANES = 128
SUBLANES = 8
VMEM_LIMIT = 56 * 1024 * 1024

TM = 512
TM_QKV = 256
TM_FFN = 1024
TN_QKV = 512
TF = 512
FFN_ROW_GROUPS = 2
D_FF_PAD = ((D_FF + TF - 1) // TF) * TF
TK_FOX = 512
TQ_FOX = 2 * TK_FOX
TK_FOX_S = 512
TQ_BAND = 256
TB_BAND = 1024
CUM_BLK = 128


def _cparams(sem):
    return pltpu.CompilerParams(dimension_semantics=sem, vmem_limit_bytes=VMEM_LIMIT)


def _rmsnorm_f32(x, g):
    ms = jnp.mean(x * x, axis=-1, keepdims=True)
    return (x * lax.rsqrt(ms + EPS)) * g


def _qkv_kernel(*refs, with_fgate, n_aliased, q_scale):
    refs = refs[:3 + 2 * with_fgate] + refs[3 + 2 * with_fgate + n_aliased:]
    if with_fgate:
        (x_ref, g_ref, w_ref, wf_ref, bf_ref,
         q_ref, k32_ref, v32_ref, kb_ref, vb_ref, lf_ref) = refs
    else:
        x_ref, g_ref, w_ref, q_ref, k32_ref, v32_ref, kb_ref, vb_ref = refs

    h = _rmsnorm_f32(x_ref[...], g_ref[...]).astype(BF16)
    if with_fgate:
        z = jnp.dot(h, wf_ref[...], preferred_element_type=F32) + bf_ref[...]
        lf_ref[...] = jnp.minimum(z, 0.0) - jnp.log1p(jnp.exp(-jnp.abs(z)))

    def project(part, c):
        lo = part * D_MODEL + c * TN_QKV
        return jnp.dot(h, w_ref[:, lo:lo + TN_QKV], preferred_element_type=F32)

    for c in range(D_MODEL // TN_QKV):
        cs = slice(c * TN_QKV, (c + 1) * TN_QKV)
        q_ref[:, cs] = (project(0, c) * q_scale).astype(BF16)
        k = project(1, c)
        k32_ref[:, cs] = k
        kb_ref[:, cs] = k.astype(BF16)
        v = project(2, c)
        v32_ref[:, cs] = v
        vb_ref[:, cs] = v.astype(BF16)


def _qkv_proj(x, g, w_qkv_b, wf_b=None, bf=None, slot=0, n_slots=1, kv_stack=None, q_scale=SCALE):
    m = x.shape[0]
    tm = min(TM_QKV, m)
    with_fgate = wf_b is not None
    stack_tile = pl.BlockSpec((None, tm, D_MODEL), lambda i: (slot, i, 0))
    stack_shape = jax.ShapeDtypeStruct((n_slots, m, D_MODEL), F32)
    resident = lambda shape: pl.BlockSpec(shape, lambda i: (0, 0), pipeline_mode=pl.Buffered(1))
    in_specs = [
        pl.BlockSpec((tm, D_MODEL), lambda i: (i, 0)),
        resident((1, D_MODEL)),
        resident((D_MODEL, 3 * D_MODEL)),
    ]
    args = [x, g.reshape(1, D_MODEL), w_qkv_b]
    tile = pl.BlockSpec((tm, D_MODEL), lambda i: (i, 0))
    out_specs = [tile, stack_tile, stack_tile, tile, tile]
    out_shape = [
        jax.ShapeDtypeStruct((m, D_MODEL), BF16),
        stack_shape,
        stack_shape,
        jax.ShapeDtypeStruct((m, D_MODEL), BF16),
        jax.ShapeDtypeStruct((m, D_MODEL), BF16),
    ]
    if with_fgate:
        in_specs += [resident((D_MODEL, LANES)), resident((1, LANES))]
        args += [wf_b, bf]
        out_specs.append(pl.BlockSpec((tm, LANES), lambda i: (i, 0)))
        out_shape.append(jax.ShapeDtypeStruct((m, LANES), F32))
    aliases = {}
    if kv_stack is not None:
        aliases = {len(args): 1, len(args) + 1: 2}
        in_specs += [pl.BlockSpec(memory_space=pl.ANY)] * 2
        args += list(kv_stack)
    return pl.pallas_call(
        functools.partial(_qkv_kernel, with_fgate=with_fgate, n_aliased=len(aliases),
                          q_scale=q_scale),
        grid=(m // tm,),
        in_specs=in_specs,
        out_specs=out_specs,
        out_shape=out_shape,
        input_output_aliases=aliases,
        compiler_params=_cparams(("arbitrary",)),
        name="qkv_proj",
    )(*args)


def _cumsum_kernel(x_ref, o_ref, carry_ref):
    @pl.when(pl.program_id(0) == 0)
    def _():
        carry_ref[...] = jnp.zeros_like(carry_ref)

    x = x_ref[...]
    hi = x.astype(BF16)
    r1 = x - hi.astype(F32)
    mid = r1.astype(BF16)
    lo = (r1 - mid.astype(F32)).astype(BF16)
    row = lax.broadcasted_iota(jnp.int32, (CUM_BLK, CUM_BLK), 0)
    col = lax.broadcasted_iota(jnp.int32, (CUM_BLK, CUM_BLK), 1)
    tri = (row <= col).astype(BF16)
    s = jnp.dot(lo, tri, preferred_element_type=F32)
    s = s + jnp.dot(mid, tri, preferred_element_type=F32)
    s = s + jnp.dot(hi, tri, preferred_element_type=F32)
    s = s + carry_ref[...]
    o_ref[...] = s
    carry_ref[...] = jnp.broadcast_to(s[:, CUM_BLK - 1:CUM_BLK], s.shape)


def _cumsum_lanes(x):
    r, t = x.shape
    return pl.pallas_call(
        _cumsum_kernel,
        grid=(t // CUM_BLK,),
        in_specs=[pl.BlockSpec((r, CUM_BLK), lambda i: (0, i))],
        out_specs=pl.BlockSpec((r, CUM_BLK), lambda i: (0, i)),
        out_shape=jax.ShapeDtypeStruct((r, t), F32),
        scratch_shapes=[pltpu.VMEM((r, CUM_BLK), F32)],
        compiler_params=_cparams(("arbitrary",)),
        name="cumsum_time",
    )(x)


def _online_softmax_step(s, v, m_ref, l_ref, acc_ref):
    m_prev = m_ref[...]
    m_new = jnp.maximum(m_prev, jnp.max(s, axis=-1, keepdims=True))
    alpha = jnp.exp(m_prev - m_new)
    p = jnp.exp(s - m_new)
    l_ref[...] = alpha * l_ref[...] + jnp.sum(p, axis=-1, keepdims=True)
    acc_ref[...] = alpha * acc_ref[...] + jnp.dot(p.astype(BF16), v, preferred_element_type=F32)
    m_ref[...] = m_new


def _split3_bf16(x):
    hi = x.astype(BF16)
    r1 = x - hi.astype(F32)
    mid = r1.astype(BF16)
    lo = (r1 - mid.astype(F32)).astype(BF16)
    return jnp.concatenate([hi, mid, lo], axis=-1)


def _fox_aug_placement():
    pq = np.zeros((3 * LANES, D_MODEL), np.float32)
    pk = np.zeros((3 * LANES, D_MODEL), np.float32)
    cq = np.zeros((1, D_MODEL), np.float32)
    ck = np.zeros((1, D_MODEL), np.float32)
    for h in range(N_HEADS):
        for piece in range(3):
            pq[piece * LANES + h, h * HEAD_DIM + piece] = 1.0
            pk[piece * LANES + h, h * HEAD_DIM + 3 + piece] = -1.0
            cq[0, h * HEAD_DIM + 3 + piece] = 1.0
            ck[0, h * HEAD_DIM + piece] = 1.0
    return (jnp.asarray(pq, BF16), jnp.asarray(pk, BF16), jnp.asarray(cq), jnp.asarray(ck))


def _fox_aug_kernel(x_ref, pq_ref, pk_ref, cq_ref, ck_ref, q_ref, k_ref, e_ref,
                    aq_ref, ak_ref, cum_ref, qn_ref, kn_ref, carry_ref):
    @pl.when(pl.program_id(0) == 0)
    def _():
        carry_ref[...] = jnp.zeros_like(carry_ref)
        qn_ref[...] = jnp.zeros_like(qn_ref)
        kn_ref[...] = jnp.zeros_like(kn_ref)

    for a_ref, n_ref in ((q_ref, qn_ref), (k_ref, kn_ref)):
        a = a_ref[...].astype(F32)
        n2 = jnp.dot((a * a).astype(BF16), e_ref[...], preferred_element_type=F32)
        n_ref[...] = jnp.maximum(n_ref[...], jnp.max(n2, axis=0, keepdims=True))

    tb = x_ref.shape[0]
    row = lax.broadcasted_iota(jnp.int32, (tb, tb), 0)
    col = lax.broadcasted_iota(jnp.int32, (tb, tb), 1)
    tri = (col <= row).astype(BF16)
    x3 = _split3_bf16(x_ref[...])
    c3 = jnp.dot(tri, x3, preferred_element_type=F32)
    cum = (c3[:, 2 * LANES:] + c3[:, LANES:2 * LANES]) + c3[:, :LANES] + carry_ref[...]
    carry_ref[...] = cum[tb - 1:tb, :]
    cum_ref[...] = cum
    cum3 = _split3_bf16(cum)
    aq = jnp.dot(cum3, pq_ref[...], preferred_element_type=F32) + cq_ref[...]
    ak = jnp.dot(cum3, pk_ref[...], preferred_element_type=F32) + ck_ref[...]
    aq_ref[...] = aq.astype(BF16)
    ak_ref[...] = ak.astype(BF16)


def _fox_aug(logf_pad, q, k):
    t_len = logf_pad.shape[0]
    tb = 2 * LANES
    pq, pk, cq, ck = _fox_aug_placement()
    head_of_col = np.arange(D_MODEL)[:, None] // HEAD_DIM == np.arange(LANES)[None, :]
    const = lambda shape: pl.BlockSpec(shape, lambda i: (0, 0))
    rows = lambda width: pl.BlockSpec((tb, width), lambda i: (i, 0))
    return pl.pallas_call(
        _fox_aug_kernel,
        grid=(t_len // tb,),
        in_specs=[
            rows(LANES),
            const((3 * LANES, D_MODEL)), const((3 * LANES, D_MODEL)),
            const((1, D_MODEL)), const((1, D_MODEL)),
            rows(D_MODEL), rows(D_MODEL), const((D_MODEL, LANES)),
        ],
        out_specs=[rows(D_MODEL), rows(D_MODEL), rows(LANES), const((1, LANES)), const((1, LANES))],
        out_shape=[jax.ShapeDtypeStruct((t_len, D_MODEL), BF16)] * 2
        + [jax.ShapeDtypeStruct((t_len, LANES), F32)]
        + [jax.ShapeDtypeStruct((1, LANES), F32)] * 2,
        scratch_shapes=[pltpu.VMEM((1, LANES), F32)],
        compiler_params=_cparams(("arbitrary",)),
        name="fox_aug",
    )(logf_pad, pq, pk, cq, ck, q, k, jnp.asarray(head_of_col, BF16))


def _fox_prompt_kernel(start_ref, q_ref, aq_ref, k_ref, ak_ref, v_ref, o_ref,
                       s0_ref, s1_ref, p0_ref, p1_ref, a0_ref, a1_ref, m_ref, acc_ref):
    qi = pl.program_id(1)
    j0 = start_ref[pl.program_id(0), qi]
    tq, tk = TQ_FOX, TK_FOX
    dn = (((1,), (1,)), ((), ()))
    q = jnp.concatenate([q_ref[...], aq_ref[...]], axis=1)
    ones_col = (lax.broadcasted_iota(jnp.int32, (tk, HEAD_DIM), 1) == 0).astype(BF16)

    def scores(ki, s_ref):
        ks = pl.multiple_of(ki * tk, tk)
        k = jnp.concatenate([k_ref[pl.ds(ks, tk), :], ak_ref[pl.ds(ks, tk), :]], axis=1)
        s_ref[...] = lax.dot_general(q, k, dn, preferred_element_type=F32)

    def accumulate(ki, p_ref, a_ref):
        ks = pl.multiple_of(ki * tk, tk)
        v = jnp.concatenate([v_ref[pl.ds(ks, tk), :], ones_col], axis=1)
        pv = jnp.dot(p_ref[...], v, preferred_element_type=F32)
        acc_ref[...] = a_ref[...] * acc_ref[...] + pv

    def softmax(s, p_ref, a_ref):
        m_prev = m_ref[...]
        m_new = jnp.maximum(m_prev, jnp.max(s, axis=-1, keepdims=True))
        a_ref[...] = jnp.exp(m_prev - m_new)
        p_ref[...] = jnp.exp(s - m_new).astype(BF16)
        m_ref[...] = m_new

    m_ref[...] = jnp.full_like(m_ref, -jnp.inf)
    acc_ref[...] = jnp.zeros_like(acc_ref)
    p1_ref[...] = jnp.zeros_like(p1_ref)
    a1_ref[...] = jnp.ones_like(a1_ref)
    scores(2 * j0, s0_ref)

    def body(j, carry):
        scores(2 * j + 1, s1_ref)
        accumulate(jnp.maximum(2 * j - 1, 0), p1_ref, a1_ref)
        softmax(s0_ref[...], p0_ref, a0_ref)
        scores(2 * j + 2, s0_ref)
        accumulate(2 * j, p0_ref, a0_ref)
        softmax(s1_ref[...], p1_ref, a1_ref)
        return carry

    lax.fori_loop(j0, qi, body, 0)

    row = lax.broadcasted_iota(jnp.int32, (tq, tk), 0)
    col = lax.broadcasted_iota(jnp.int32, (tq, tk), 1)
    scores(2 * qi + 1, s1_ref)
    accumulate(jnp.maximum(2 * qi - 1, 0), p1_ref, a1_ref)
    softmax(jnp.where(col <= row, s0_ref[...], -jnp.inf), p0_ref, a0_ref)
    accumulate(2 * qi, p0_ref, a0_ref)
    softmax(jnp.where(col + tk <= row, s1_ref[...], -jnp.inf), p1_ref, a1_ref)
    accumulate(2 * qi + 1, p1_ref, a1_ref)
    acc = acc_ref[...]
    o_ref[...] = (acc[:, :HEAD_DIM] / acc[:, HEAD_DIM:HEAD_DIM + 1]).astype(BF16)


def _fox_first_live_pair(cum_pad, qn2, kn2):
    nq = cum_pad.shape[0] // TQ_FOX
    slack = 2.0 * 1.01 * jnp.sqrt(qn2[0, :N_HEADS] * kn2[0, :N_HEADS])
    cum = cum_pad[:, :N_HEADS]
    c_tile_start = cum[0::TQ_FOX]
    c_pair_end = cum[TQ_FOX - 1::TQ_FOX]
    bound = c_tile_start[:, None, :] - c_pair_end[None, :, :] + slack[None, None, :]
    before = np.arange(nq)[None, :, None] < np.arange(nq)[:, None, None]
    dead = jnp.logical_and(bound < -FOX_DEAD_LOGIT, before)
    lead = jnp.cumprod(dead.astype(jnp.int32), axis=1)
    return jnp.transpose(jnp.sum(lead, axis=1)).astype(jnp.int32)


def _fox_prompt_attn(q, aq, k, ak, v, first_pair):
    t_len = q.shape[0]
    tq, tk = TQ_FOX, TK_FOX
    assert tq == 2 * tk and t_len % tq == 0
    tile = pl.BlockSpec((tq, HEAD_DIM), lambda h, i, start: (i, h))
    whole = pl.BlockSpec((t_len, HEAD_DIM), lambda h, i, start: (0, h))
    return pl.pallas_call(
        _fox_prompt_kernel,
        grid_spec=pltpu.PrefetchScalarGridSpec(
            num_scalar_prefetch=1,
            grid=(N_HEADS, t_len // tq),
            in_specs=[tile, tile, whole, whole, whole],
            out_specs=tile,
            scratch_shapes=[
                pltpu.VMEM((tq, tk), F32), pltpu.VMEM((tq, tk), F32),
                pltpu.VMEM((tq, tk), BF16), pltpu.VMEM((tq, tk), BF16),
                pltpu.VMEM((tq, 1), F32), pltpu.VMEM((tq, 1), F32),
                pltpu.VMEM((tq, 1), F32),
                pltpu.VMEM((tq, 2 * HEAD_DIM), F32),
            ],
        ),
        out_shape=jax.ShapeDtypeStruct((t_len, D_MODEL), BF16),
        compiler_params=_cparams(("arbitrary", "arbitrary")),
        name="fox_prompt_attn",
    )(first_pair, q, aq, k, ak, v)


def _fox_sample_kernel(q_ref, kn_ref, vn_ref, kc_ref, vc_ref, cq_ref, ckc_ref, ckn_ref,
                       o_ref, s_ref, sn_ref, m_ref, acc_ref, *, n_new, tk):
    ki = pl.program_id(1)
    nk = pl.num_programs(1)
    dn = (((1,), (1,)), ((), ()))

    @pl.when(ki == 0)
    def _():
        m_ref[...] = jnp.full_like(m_ref, -jnp.inf)
        acc_ref[...] = jnp.zeros_like(acc_ref)

    def head_rows(h):
        return slice(h * n_new, (h + 1) * n_new)

    def head_cols(h):
        return slice(h * HEAD_DIM, (h + 1) * HEAD_DIM)

    def softmax_update(s):
        m_prev = m_ref[...]
        m_new = jnp.maximum(m_prev, jnp.max(s, axis=-1, keepdims=True))
        m_ref[...] = m_new
        return jnp.exp(m_prev - m_new), jnp.exp(s - m_new).astype(BF16)

    def accumulate(alpha, p, values):
        n_keys = p.shape[1]
        ones_col = (lax.broadcasted_iota(jnp.int32, (n_keys, HEAD_DIM), 1) == 0).astype(BF16)
        for h in range(N_HEADS):
            r = head_rows(h)
            pv = jnp.dot(p[r], jnp.concatenate([values(h), ones_col], axis=1),
                         preferred_element_type=F32)
            acc_ref[r, :] = alpha[r] * acc_ref[r, :] + pv

    for h in range(N_HEADS):
        kc = kc_ref[0, pl.ds(h, tk, stride=N_HEADS), :].astype(BF16)
        s = lax.dot_general(q_ref[:, head_cols(h)], kc, dn, preferred_element_type=F32)
        s_ref[head_rows(h), :] = (s + cq_ref[0, :, h:h + 1]) - ckc_ref[0, h:h + 1, :]
    alpha, p = softmax_update(s_ref[...])
    accumulate(alpha, p, lambda h: vc_ref[0, pl.ds(h, tk, stride=N_HEADS), :].astype(BF16))

    @pl.when(ki == nk - 1)
    def _():
        row = lax.broadcasted_iota(jnp.int32, (n_new, n_new), 0)
        col = lax.broadcasted_iota(jnp.int32, (n_new, n_new), 1)
        for h in range(N_HEADS):
            s = lax.dot_general(q_ref[:, head_cols(h)], kn_ref[:, head_cols(h)], dn,
                                preferred_element_type=F32)
            s = (s + cq_ref[0, :, h:h + 1]) - ckn_ref[0, h:h + 1, 0:n_new]
            sn_ref[head_rows(h), :] = jnp.where(col <= row, s, -jnp.inf)
        alpha, p = softmax_update(sn_ref[...])
        accumulate(alpha, p, lambda h: vn_ref[:, head_cols(h)])
        for h in range(N_HEADS):
            acc = acc_ref[head_rows(h), :]
            o_ref[:, head_cols(h)] = (acc[:, :HEAD_DIM] / acc[:, HEAD_DIM:HEAD_DIM + 1]).astype(BF16)


def _fox_sample_attn(q, kn, vn, cache_k, cache_v, layer, cq_bth, cum_bht, n_streams, n_new):
    past = cache_k.shape[1] // N_HEADS
    tk = TK_FOX_S
    first = layer * n_streams
    rows = N_HEADS * n_new
    return pl.pallas_call(
        functools.partial(_fox_sample_kernel, n_new=n_new, tk=tk),
        grid=(n_streams, past // tk),
        in_specs=[
            pl.BlockSpec((n_new, D_MODEL), lambda b, i: (b, 0)),
            pl.BlockSpec((n_new, D_MODEL), lambda b, i: (b, 0)),
            pl.BlockSpec((n_new, D_MODEL), lambda b, i: (b, 0)),
            pl.BlockSpec((1, tk * N_HEADS, HEAD_DIM), lambda b, i: (first + b, i, 0)),
            pl.BlockSpec((1, tk * N_HEADS, HEAD_DIM), lambda b, i: (first + b, i, 0)),
            pl.BlockSpec((1, n_new, N_HEADS), lambda b, i: (b, 0, 0)),
            pl.BlockSpec((1, N_HEADS, tk), lambda b, i: (b, 0, i)),
            pl.BlockSpec((1, N_HEADS, LANES), lambda b, i: (b, 0, past // LANES)),
        ],
        out_specs=pl.BlockSpec((n_new, D_MODEL), lambda b, i: (b, 0)),
        out_shape=jax.ShapeDtypeStruct((n_streams * n_new, D_MODEL), BF16),
        scratch_shapes=[
            pltpu.VMEM((rows, tk), F32),
            pltpu.VMEM((rows, n_new), F32),
            pltpu.VMEM((rows, 1), F32),
            pltpu.VMEM((rows, 2 * HEAD_DIM), F32),
        ],
        compiler_params=_cparams(("arbitrary", "arbitrary")),
        name="fox_sample_attn",
    )(q, kn, vn, cache_k, cache_v, cq_bth, cum_bht, cum_bht)


def _band_profile(rel_table, d0, n_rows, width):
    dist = d0 + (n_rows - 1) - np.arange(width)
    idx = np.clip(dist, -MAX_REL, MAX_REL) + MAX_REL
    return rel_table.astype(F32)[:, idx].reshape(N_HEADS, 1, width)


def _toeplitz_rows(g, n_rows, n_cols):
    width = g.shape[-1]
    assert n_cols + n_rows - 1 <= width and width % LANES == 0
    r = pltpu.roll(jnp.broadcast_to(g, (n_rows, width)), width - (n_rows - 1), axis=1,
                   stride=1, stride_axis=0)
    return r[:, :n_cols]


def _softmax_rows(s):
    m = jnp.max(s, axis=-1, keepdims=True)
    p = jnp.exp(s - m)
    return p, jnp.sum(p, axis=-1, keepdims=True)


def _band_prompt_kernel(q_ref, kp_ref, kc_ref, vp_ref, vc_ref, g_ref, o_ref, b_ref):
    qi = pl.program_id(1)
    t = TQ_BAND

    @pl.when(qi == 0)
    def _():
        qc = lax.broadcasted_iota(jnp.int32, (t, 3 * t), 0) // CHUNK
        kc = lax.broadcasted_iota(jnp.int32, (t, 3 * t), 1) // CHUNK
        ok = (kc >= qc) & (kc <= qc + BAND_CHUNKS)
        b_ref[...] = jnp.where(ok, _toeplitz_rows(g_ref[0], t, 3 * t), -jnp.inf)

    k = jnp.concatenate([kp_ref[...], kc_ref[...]], axis=0)
    v = jnp.concatenate([vp_ref[...], vc_ref[...]], axis=0)
    col = lax.broadcasted_iota(jnp.int32, (t, 3 * t), 1)
    for sub in range(TB_BAND // t):
        lo = sub * t
        q = q_ref[lo:lo + t, :]
        s = lax.dot_general(q, k[lo:lo + 3 * t], (((1,), (1,)), ((), ())),
                            preferred_element_type=F32) + b_ref[...]
        if lo < BAND_PAST:
            first_valid = jnp.where(qi == 0, BAND_PAST - lo, 0)
            s = jnp.where(col >= first_valid, s, -jnp.inf)
        p, l = _softmax_rows(s)
        o = jnp.dot(p.astype(BF16), v[lo:lo + 3 * t], preferred_element_type=F32)
        o_ref[lo:lo + t, :] = (o / l).astype(BF16)


def _band_prompt_attn(q, k, v, profile):
    t_len = q.shape[0]
    t = TQ_BAND
    tb = TB_BAND
    per = tb // BAND_PAST
    assert BAND_PAST == 2 * t and t % CHUNK == 0 and tb % BAND_PAST == 0 and t_len % tb == 0
    cur = pl.BlockSpec((tb, HEAD_DIM), lambda h, i: (i, h))
    prev = pl.BlockSpec((BAND_PAST, HEAD_DIM), lambda h, i: (jnp.maximum(i * per - 1, 0), h))
    return pl.pallas_call(
        _band_prompt_kernel,
        grid=(N_HEADS, t_len // tb),
        in_specs=[cur, prev, cur, prev, cur,
                  pl.BlockSpec((1, 1, 4 * t), lambda h, i: (h, 0, 0))],
        out_specs=cur,
        out_shape=jax.ShapeDtypeStruct((t_len, D_MODEL), BF16),
        scratch_shapes=[pltpu.VMEM((t, 3 * t), F32)],
        compiler_params=_cparams(("arbitrary", "arbitrary")),
        name="band_prompt_attn",
    )(q, k, k, v, v, profile)


def _band_sample_kernel(q_ref, kn_ref, vn_ref, kc_ref, vc_ref, g_ref, mask_ref, o_ref, b_ref, *,
                        n_cache):
    n_new = q_ref.shape[0]

    @pl.when(pl.program_id(0) == 0)
    def _():
        for h in range(N_HEADS):
            b_ref[h] = _toeplitz_rows(g_ref[h], n_new, n_cache + n_new) + mask_ref[...]

    for h in range(N_HEADS):
        hs = slice(h * HEAD_DIM, (h + 1) * HEAD_DIM)
        q = q_ref[:, hs]
        kc = kc_ref[0, pl.ds(h, n_cache, stride=N_HEADS), :].astype(BF16)
        vc = vc_ref[0, pl.ds(h, n_cache, stride=N_HEADS), :].astype(BF16)
        dn = (((1,), (1,)), ((), ()))
        s1 = lax.dot_general(q, kc, dn, preferred_element_type=F32) + b_ref[h, :, 0:n_cache]
        s2 = lax.dot_general(q, kn_ref[:, hs], dn, preferred_element_type=F32) + b_ref[h, :, n_cache:]
        m = jnp.maximum(jnp.max(s1, axis=-1, keepdims=True), jnp.max(s2, axis=-1, keepdims=True))
        p1 = jnp.exp(s1 - m)
        p2 = jnp.exp(s2 - m)
        l = jnp.sum(p1, axis=-1, keepdims=True) + jnp.sum(p2, axis=-1, keepdims=True)
        o = jnp.dot(p1.astype(BF16), vc, preferred_element_type=F32)
        o = o + jnp.dot(p2.astype(BF16), vn_ref[:, hs], preferred_element_type=F32)
        o_ref[:, hs] = (o / l).astype(BF16)


def _band_sample_attn(q, kn, vn, cache_k, cache_v, layer, rel_table, past, n_streams, n_new):
    n_cache = cache_k.shape[1] // N_HEADS
    n_keys = n_cache + n_new
    first = layer * n_streams
    q_pos = past + np.arange(n_new)
    k_pos = past - n_cache + np.arange(n_keys)
    qc, kc = q_pos[:, None] // CHUNK, k_pos[None, :] // CHUNK
    ok = (k_pos[None, :] >= 0) & (kc <= qc) & (kc >= qc - BAND_CHUNKS)
    mask = jnp.asarray(np.where(ok, 0.0, -np.inf), F32)
    width = -(-(n_keys + n_new - 1) // LANES) * LANES
    profile = _band_profile(rel_table, n_cache, n_new, width)
    return pl.pallas_call(
        functools.partial(_band_sample_kernel, n_cache=n_cache),
        grid=(n_streams,),
        in_specs=[
            pl.BlockSpec((n_new, D_MODEL), lambda b: (b, 0)),
            pl.BlockSpec((n_new, D_MODEL), lambda b: (b, 0)),
            pl.BlockSpec((n_new, D_MODEL), lambda b: (b, 0)),
            pl.BlockSpec((1, n_cache * N_HEADS, HEAD_DIM), lambda b: (first + b, 0, 0)),
            pl.BlockSpec((1, n_cache * N_HEADS, HEAD_DIM), lambda b: (first + b, 0, 0)),
            pl.BlockSpec((N_HEADS, 1, width), lambda b: (0, 0, 0)),
            pl.BlockSpec((n_new, n_keys), lambda b: (0, 0)),
        ],
        out_specs=pl.BlockSpec((n_new, D_MODEL), lambda b: (b, 0)),
        out_shape=jax.ShapeDtypeStruct((n_streams * n_new, D_MODEL), BF16),
        scratch_shapes=[pltpu.VMEM((N_HEADS, n_new, n_keys), F32)],
        compiler_params=_cparams(("arbitrary",)),
        name="band_sample_attn",
    )(q, kn, vn, cache_k, cache_v, profile, mask)


def _oproj_kernel(x_ref, a_ref, w_ref, o_ref):
    o_ref[...] = x_ref[...] + jnp.dot(a_ref[...], w_ref[...], preferred_element_type=F32)


def _out_proj(x, a, w_o_b):
    m = x.shape[0]
    tm = min(TM, m)
    return pl.pallas_call(
        _oproj_kernel,
        grid=(m // tm,),
        in_specs=[
            pl.BlockSpec((tm, D_MODEL), lambda i: (i, 0)),
            pl.BlockSpec((tm, D_MODEL), lambda i: (i, 0)),
            pl.BlockSpec((D_MODEL, D_MODEL), lambda i: (0, 0)),
        ],
        out_specs=pl.BlockSpec((tm, D_MODEL), lambda i: (i, 0)),
        out_shape=jax.ShapeDtypeStruct((m, D_MODEL), F32),
        compiler_params=_cparams(("arbitrary",)),
        name="out_proj",
    )(x, a, w_o_b)


def _ffn_kernel(*refs, stream_len, tm, final_norm):
    if final_norm:
        (x_ref, g_ref, wg_ref, wv_ref, hg_ref, hv_ref, cwg_ref, cwv_ref, cbg_ref, cbv_ref,
         wo_ref, gf_ref, o_ref, tg_ref, tv_ref, h_ref, pg_ref, pv_ref, u_ref) = refs
    else:
        (x_ref, g_ref, wg_ref, wv_ref, hg_ref, hv_ref, cwg_ref, cwv_ref, cbg_ref, cbv_ref,
         wo_ref, o_ref, tg_ref, tv_ref, h_ref, pg_ref, pv_ref, u_ref) = refs
        gf_ref = None
    i = pl.program_id(0)
    f = pl.program_id(1)
    nf = pl.num_programs(1)
    streams_per_tile = max(tm // stream_len, 1)
    tiles_per_stream = max(stream_len // tm, 1)

    @pl.when(f == 0)
    def _():
        h_ref[...] = _rmsnorm_f32(x_ref[...], g_ref[...]).astype(BF16)
        o_ref[...] = x_ref[...]

    n_groups = FFN_ROW_GROUPS
    rg = tm // n_groups
    assert rg % SUBLANES == 0 and (streams_per_tile == 1 or streams_per_tile % n_groups == 0)
    row = lax.broadcasted_iota(jnp.int32, (rg, TF), 0)
    if streams_per_tile > 1:
        row = row % stream_len
    halves = ((wg_ref, hg_ref, cwg_ref, cbg_ref, pg_ref, tg_ref),
              (wv_ref, hv_ref, cwv_ref, cbv_ref, pv_ref, tv_ref))

    class _Up:
        def __init__(self, g, half):
            self.ref = u_ref.at[half * n_groups + g]
            self.ref[...] = jnp.dot(h_ref[g * rg:(g + 1) * rg, :], halves[half][0][...],
                                    preferred_element_type=F32)

        def __getitem__(self, idx):
            return self.ref[idx]

        def reshape(self, *shape):
            return self.ref[...].reshape(*shape)

    if streams_per_tile == 1:
        @pl.when((i % tiles_per_stream) == 0)
        def _():
            for (_, hist_ref, _, _, prev_ref, _) in halves:
                prev_ref[f, SUBLANES - (CONV_W - 1):SUBLANES, :] = hist_ref[0]

    ups = [[_Up(0, half) for half in range(2)]]

    def conv(g, half):
        _, hist_ref, cw_ref, cb_ref, prev_ref, _ = halves[half]
        u = ups[g][half][...]
        if streams_per_tile > 1:
            spg = streams_per_tile // n_groups
            def per_row(r):
                hr = hist_ref[g * spg:(g + 1) * spg, r:r + 1, :]
                return jnp.broadcast_to(hr, (spg, stream_len, TF)).reshape(rg, TF)
            p0, p1 = per_row(0), per_row(1)
        elif g == 0:
            p0 = prev_ref[f, SUBLANES - 2:SUBLANES - 1, :]
            p1 = prev_ref[f, SUBLANES - 1:SUBLANES, :]
        else:
            p0 = ups[g - 1][half][rg - 2:rg - 1, :]
            p1 = ups[g - 1][half][rg - 1:rg, :]
        u1 = jnp.where(row == 0, p1, pltpu.roll(u, 1, axis=0))
        u2 = jnp.where(row == 0, p0, jnp.where(row == 1, p1, pltpu.roll(u, 2, axis=0)))
        c = cb_ref[...] + cw_ref[0:1, :] * u2
        c = c + cw_ref[1:2, :] * u1
        return c + cw_ref[2:3, :] * u

    for g in range(n_groups):
        cg = conv(g, 0)
        cv = conv(g, 1)
        a = ((cg * jax.nn.sigmoid(cg)) * cv).astype(BF16)
        if g + 1 < n_groups:
            ups.append([_Up(g + 1, half) for half in range(2)])
        o_ref[g * rg:(g + 1) * rg, :] += jnp.dot(a, wo_ref[...], preferred_element_type=F32)

    for half, (_, _, _, _, prev_ref, tail_ref) in enumerate(halves):
        if streams_per_tile > 1:
            spg = streams_per_tile // n_groups
            for g in range(n_groups):
                u3 = ups[g][half].reshape(spg, stream_len, TF)
                tail_ref[g * spg:(g + 1) * spg] = u3[:, stream_len - SUBLANES:, :]
        else:
            last = ups[n_groups - 1][half][rg - SUBLANES:, :]
            prev_ref[f] = last
            tail_ref[0] = last

    if final_norm:
        @pl.when(f == nf - 1)
        def _():
            o_ref[...] = _rmsnorm_f32(o_ref[...], gf_ref[...])


def _conv_ffn(x, g, w_in_g, w_in_v, hist_g, hist_v, cw_g, cw_v, cb_g, cb_v, w_out_b, stream_len,
              g_final=None):
    m = x.shape[0]
    tm = min(TM_FFN, m)
    nf = D_FF_PAD // TF
    n_tiles = m // tm
    final_norm = g_final is not None
    if stream_len >= tm:
        assert stream_len % tm == 0
        tiles_per_stream = stream_len // tm
        hist_spec = pl.BlockSpec((1, CONV_W - 1, TF), lambda i, f: (i // tiles_per_stream, 0, f))
        tail_rows = n_tiles
        tail_spec = pl.BlockSpec((1, SUBLANES, TF), lambda i, f: (i, 0, f))
    else:
        assert tm % stream_len == 0 and stream_len % SUBLANES == 0 and n_tiles == 1
        spt = tm // stream_len
        hist_spec = pl.BlockSpec((spt, CONV_W - 1, TF), lambda i, f: (i, 0, f))
        tail_rows = m // stream_len
        tail_spec = pl.BlockSpec((spt, SUBLANES, TF), lambda i, f: (i, 0, f))
    row_tile = pl.BlockSpec((tm, D_MODEL), lambda i, f: (i, 0))
    gain = pl.BlockSpec((1, D_MODEL), lambda i, f: (0, 0))
    w_in_spec = pl.BlockSpec((D_MODEL, TF), lambda i, f: (0, f))
    cw_spec = pl.BlockSpec((CONV_W, TF), lambda i, f: (0, f))
    cb_spec = pl.BlockSpec((1, TF), lambda i, f: (0, f))
    x_tile = pl.BlockSpec((tm, D_MODEL), lambda i, f: (i, 0), pipeline_mode=pl.Buffered(1))
    in_specs = [x_tile, gain, w_in_spec, w_in_spec, hist_spec, hist_spec, cw_spec, cw_spec,
                cb_spec, cb_spec, pl.BlockSpec((TF, D_MODEL), lambda i, f: (f, 0))]
    args = [x, g.reshape(1, D_MODEL), w_in_g, w_in_v, hist_g, hist_v, cw_g, cw_v, cb_g, cb_v, w_out_b]
    if final_norm:
        in_specs.append(gain)
        args.append(g_final.reshape(1, D_MODEL))
    tail_shape = jax.ShapeDtypeStruct((tail_rows, SUBLANES, D_FF_PAD), F32)
    return pl.pallas_call(
        functools.partial(_ffn_kernel, stream_len=stream_len, tm=tm, final_norm=final_norm),
        grid=(n_tiles, nf),
        in_specs=in_specs,
        out_specs=[row_tile, tail_spec, tail_spec],
        out_shape=[jax.ShapeDtypeStruct((m, D_MODEL), F32), tail_shape, tail_shape],
        scratch_shapes=[
            pltpu.VMEM((tm, D_MODEL), BF16),
            pltpu.VMEM((nf, SUBLANES, TF), F32),
            pltpu.VMEM((nf, SUBLANES, TF), F32),
            pltpu.VMEM((2 * FFN_ROW_GROUPS, tm // FFN_ROW_GROUPS, TF), F32),
        ],
        compiler_params=_cparams(("arbitrary", "arbitrary")),
        name="conv_ffn",
    )(*args)


def _pad_ff(a):
    pad = [(0, 0)] * (a.ndim - 1) + [(0, D_FF_PAD - D_FF)]
    return jnp.pad(a, pad)


def _split_ff(a):
    return _pad_ff(a[..., :D_FF]), _pad_ff(a[..., D_FF:])


def kernel(x_prompt, x_sample, cache_fox_k, cache_fox_v, cache_fox_logf, cache_band_k, cache_band_v,
           state_conv, g_attn, w_qkv, w_fgate, b_fgate, rel_bias, w_o, g_ffn, w_in, conv_w, conv_b,
           w_out, g_final):
    depth = g_attn.shape[0]
    bp, sp, _ = x_prompt.shape
    bs, ss, _ = x_sample.shape
    past = cache_fox_k.shape[2]
    n_band_cache = cache_band_k.shape[2]
    n_keep = min(BAND_PAST, sp)
    assert bp == 1

    xp = x_prompt.reshape(bp * sp, D_MODEL)
    xs = x_sample.reshape(bs * ss, D_MODEL)

    n_fox = cache_fox_k.shape[0]
    n_band = cache_band_k.shape[0]
    cfk = cache_fox_k.reshape(n_fox * bs, past * N_HEADS, HEAD_DIM)
    cfv = cache_fox_v.reshape(n_fox * bs, past * N_HEADS, HEAD_DIM)
    cbk = cache_band_k.reshape(n_band * bs, n_band_cache * N_HEADS, HEAD_DIM)
    cbv = cache_band_v.reshape(n_band * bs, n_band_cache * N_HEADS, HEAD_DIM)

    hist0 = jnp.zeros((bp, CONV_W - 1, D_FF_PAD), F32)

    fox_lp, fox_ks, fox_vs, fox_ls = [], [], [], []
    band_kp, band_vp, band_ks, band_vs = [], [], [], []
    conv_p, conv_s = [], []
    fox_kv_p = None

    for i in range(depth):
        j = i // N_MIXERS
        is_fox = (i % N_MIXERS == 0)
        wqkv_b = w_qkv[i].astype(BF16)
        wo_b = w_o[i].astype(BF16)
        w_in_g, w_in_v = _split_ff(w_in[i].astype(BF16))
        cw_g, cw_v = _split_ff(conv_w[i])
        cb_g, cb_v = _split_ff(conv_b[i].reshape(1, 2 * D_FF))
        w_out_b = jnp.pad(w_out[i].astype(BF16), ((0, D_FF_PAD - D_FF), (0, 0)))
        hs_g, hs_v = _split_ff(state_conv[i])

        if is_fox:
            wf_b = jnp.pad(w_fgate[j].astype(BF16), ((0, 0), (0, LANES - N_HEADS)))
            bf = jnp.pad(b_fgate[j].reshape(1, N_HEADS), ((0, 0), (0, LANES - N_HEADS)))
            qp, kst, vst, kpb, vpb, lfp_pad = _qkv_proj(xp, g_attn[i], wqkv_b, wf_b, bf,
                                                        slot=j, n_slots=n_fox, kv_stack=fox_kv_p)
            fox_kv_p = (kst, vst)
            qs, ks32, vs32, ksb, vsb, lfs = _qkv_proj(xs, g_attn[i], wqkv_b, wf_b, bf)
            lfp = lfp_pad[:, :N_HEADS]
            lfs = lfs[:, :N_HEADS]

            aq, ak, cum_pad, qn2, kn2 = _fox_aug(lfp_pad, qp, kpb)
            op = _fox_prompt_attn(qp, aq, kpb, ak, vpb, _fox_first_live_pair(cum_pad, qn2, kn2))

            lf_all = jnp.concatenate([
                jnp.transpose(cache_fox_logf[j], (0, 2, 1)),
                jnp.transpose(lfs.reshape(bs, ss, N_HEADS), (0, 2, 1)),
                jnp.zeros((bs, N_HEADS, LANES - ss), F32),
            ], axis=-1).reshape(bs * N_HEADS, past + LANES)
            cum_s = _cumsum_lanes(lf_all).reshape(bs, N_HEADS, past + LANES)
            cq_s = jnp.transpose(cum_s[:, :, past:past + ss], (0, 2, 1))
            osm = _fox_sample_attn(qs, ksb, vsb, cfk, cfv, j, cq_s, cum_s, bs, ss)

            fox_lp.append(lfp.reshape(bp, sp, N_HEADS))
            fox_ks.append(ks32.reshape(bs, ss, N_HEADS, HEAD_DIM))
            fox_vs.append(vs32.reshape(bs, ss, N_HEADS, HEAD_DIM))
            fox_ls.append(lfs.reshape(bs, ss, N_HEADS))
        else:
            qp, kp32, vp32, kpb, vpb = _qkv_proj(xp, g_attn[i], wqkv_b)
            qs, ks32, vs32, ksb, vsb = _qkv_proj(xs, g_attn[i], wqkv_b)
            op = _band_prompt_attn(qp, kpb, vpb,
                                   _band_profile(rel_bias[j], BAND_PAST, TQ_BAND, 4 * TQ_BAND))
            osm = _band_sample_attn(qs, ksb, vsb, cbk, cbv, j, rel_bias[j], past, bs, ss)
            last_rows = slice(sp - n_keep, sp)
            band_kp.append(kp32[0, last_rows].reshape(bp, n_keep, N_HEADS, HEAD_DIM))
            band_vp.append(vp32[0, last_rows].reshape(bp, n_keep, N_HEADS, HEAD_DIM))
            band_ks.append(ks32.reshape(bs, ss, N_HEADS, HEAD_DIM))
            band_vs.append(vs32.reshape(bs, ss, N_HEADS, HEAD_DIM))

        xp = _out_proj(xp, op, wo_b)
        xs = _out_proj(xs, osm, wo_b)

        gf = g_final if i == depth - 1 else None
        xp, tgp, tvp = _conv_ffn(xp, g_ffn[i], w_in_g, w_in_v, hist0, hist0, cw_g, cw_v, cb_g, cb_v,
                                 w_out_b, sp, gf)
        xs, tgs, tvs = _conv_ffn(xs, g_ffn[i], w_in_g, w_in_v, hs_g, hs_v, cw_g, cw_v, cb_g, cb_v,
                                 w_out_b, ss, gf)
        keep = slice(SUBLANES - (CONV_W - 1), SUBLANES)
        conv_p.append(jnp.concatenate([tgp[-1:, keep, :D_FF], tvp[-1:, keep, :D_FF]], axis=-1))
        conv_s.append(jnp.concatenate([tgs[:, keep, :D_FF], tvs[:, keep, :D_FF]], axis=-1))

    y_prompt = xp.reshape(bp, sp, D_MODEL)
    y_sample = xs.reshape(bs, ss, D_MODEL)
    fox_kp, fox_vp = (a.reshape(n_fox, bp, sp, N_HEADS, HEAD_DIM) for a in fox_kv_p)
    return (y_prompt, y_sample,
            fox_kp, fox_vp, jnp.stack(fox_lp),
            jnp.stack(fox_ks), jnp.stack(fox_vs), jnp.stack(fox_ls),
            jnp.stack(band_kp), jnp.stack(band_vp), jnp.stack(band_ks), jnp.stack(band_vs),
            jnp.stack(conv_p), jnp.stack(conv_s))
```

```python
import functools
import math

import numpy as np
import jax
import jax.numpy as jnp
from jax import lax
from jax.experimental import pallas as pl
from jax.experimental.pallas import tpu as pltpu

F32 = jnp.float32
BF16 = jnp.bfloat16

D_MODEL = 2048
N_HEADS = 16
HEAD_DIM = D_MODEL // N_HEADS
SCALE = 1.0 / math.sqrt(HEAD_DIM)
FOX_DEAD_LOGIT = 128.0
CHUNK = 64
BAND_CHUNKS = 8
BAND_PAST = BAND_CHUNKS * CHUNK
MAX_REL = 2 * CHUNK
D_FF = 5504
CONV_W = 3
EPS = 1e-6
N_MIXERS = 2

LANES = 128
SUBLANES = 8
VMEM_LIMIT = 56 * 1024 * 1024

TM = 512
TM_QKV = 256
TM_FFN = 512
TN_QKV = 512
TF = 512
FFN_ROW_GROUPS = 2
D_FF_PAD = ((D_FF + TF - 1) // TF) * TF
TK_FOX = 512
TQ_FOX = 2 * TK_FOX
TK_FOX_S = 512
TQ_BAND = 256
TB_BAND = 1024
CUM_BLK = 128


def _cparams(sem):
    return pltpu.CompilerParams(dimension_semantics=sem, vmem_limit_bytes=VMEM_LIMIT)


def _rmsnorm_f32(x, g):
    ms = jnp.mean(x * x, axis=-1, keepdims=True)
    return (x * lax.rsqrt(ms + EPS)) * g


def _qkv_kernel(*refs, with_fgate, n_aliased, q_scale):
    refs = refs[:3 + 2 * with_fgate] + refs[3 + 2 * with_fgate + n_aliased:]
    if with_fgate:
        (x_ref, g_ref, w_ref, wf_ref, bf_ref,
         q_ref, k32_ref, v32_ref, kb_ref, vb_ref, lf_ref) = refs
    else:
        x_ref, g_ref, w_ref, q_ref, k32_ref, v32_ref, kb_ref, vb_ref = refs

    h = _rmsnorm_f32(x_ref[...], g_ref[...]).astype(BF16)
    if with_fgate:
        z = jnp.dot(h, wf_ref[...], preferred_element_type=F32) + bf_ref[...]
        lf_ref[...] = jnp.minimum(z, 0.0) - jnp.log1p(jnp.exp(-jnp.abs(z)))

    def project(part, c):
        lo = part * D_MODEL + c * TN_QKV
        return jnp.dot(h, w_ref[:, lo:lo + TN_QKV], preferred_element_type=F32)

    for c in range(D_MODEL // TN_QKV):
        cs = slice(c * TN_QKV, (c + 1) * TN_QKV)
        q_ref[:, cs] = (project(0, c) * q_scale).astype(BF16)
        k = project(1, c)
        k32_ref[:, cs] = k
        kb_ref[:, cs] = k.astype(BF16)
        v = project(2, c)
        v32_ref[:, cs] = v
        vb_ref[:, cs] = v.astype(BF16)


def _qkv_proj(x, g, w_qkv_b, wf_b=None, bf=None, slot=0, n_slots=1, kv_stack=None, q_scale=SCALE):
    m = x.shape[0]
    tm = min(TM_QKV, m)
    with_fgate = wf_b is not None
    stack_tile = pl.BlockSpec((None, tm, D_MODEL), lambda i: (slot, i, 0))
    stack_shape = jax.ShapeDtypeStruct((n_slots, m, D_MODEL), F32)
    resident = lambda shape: pl.BlockSpec(shape, lambda i: (0, 0), pipeline_mode=pl.Buffered(1))
    in_specs = [
        pl.BlockSpec((tm, D_MODEL), lambda i: (i, 0)),
        resident((1, D_MODEL)),
        resident((D_MODEL, 3 * D_MODEL)),
    ]
    args = [x, g.reshape(1, D_MODEL), w_qkv_b]
    tile = pl.BlockSpec((tm, D_MODEL), lambda i: (i, 0))
    out_specs = [tile, stack_tile, stack_tile, tile, tile]
    out_shape = [
        jax.ShapeDtypeStruct((m, D_MODEL), BF16),
        stack_shape,
        stack_shape,
        jax.ShapeDtypeStruct((m, D_MODEL), BF16),
        jax.ShapeDtypeStruct((m, D_MODEL), BF16),
    ]
    if with_fgate:
        in_specs += [resident((D_MODEL, LANES)), resident((1, LANES))]
        args += [wf_b, bf]
        out_specs.append(pl.BlockSpec((tm, LANES), lambda i: (i, 0)))
        out_shape.append(jax.ShapeDtypeStruct((m, LANES), F32))
    aliases = {}
    if kv_stack is not None:
        aliases = {len(args): 1, len(args) + 1: 2}
        in_specs += [pl.BlockSpec(memory_space=pl.ANY)] * 2
        args += list(kv_stack)
    return pl.pallas_call(
        functools.partial(_qkv_kernel, with_fgate=with_fgate, n_aliased=len(aliases),
                          q_scale=q_scale),
        grid=(m // tm,),
        in_specs=in_specs,
        out_specs=out_specs,
        out_shape=out_shape,
        input_output_aliases=aliases,
        compiler_params=_cparams(("arbitrary",)),
        name="qkv_proj",
    )(*args)


def _cumsum_kernel(x_ref, o_ref, carry_ref):
    @pl.when(pl.program_id(0) == 0)
    def _():
        carry_ref[...] = jnp.zeros_like(carry_ref)

    x = x_ref[...]
    hi = x.astype(BF16)
    r1 = x - hi.astype(F32)
    mid = r1.astype(BF16)
    lo = (r1 - mid.astype(F32)).astype(BF16)
    row = lax.broadcasted_iota(jnp.int32, (CUM_BLK, CUM_BLK), 0)
    col = lax.broadcasted_iota(jnp.int32, (CUM_BLK, CUM_BLK), 1)
    tri = (row <= col).astype(BF16)
    s = jnp.dot(lo, tri, preferred_element_type=F32)
    s = s + jnp.dot(mid, tri, preferred_element_type=F32)
    s = s + jnp.dot(hi, tri, preferred_element_type=F32)
    s = s + carry_ref[...]
    o_ref[...] = s
    carry_ref[...] = jnp.broadcast_to(s[:, CUM_BLK - 1:CUM_BLK], s.shape)


def _cumsum_lanes(x):
    r, t = x.shape
    return pl.pallas_call(
        _cumsum_kernel,
        grid=(t // CUM_BLK,),
        in_specs=[pl.BlockSpec((r, CUM_BLK), lambda i: (0, i))],
        out_specs=pl.BlockSpec((r, CUM_BLK), lambda i: (0, i)),
        out_shape=jax.ShapeDtypeStruct((r, t), F32),
        scratch_shapes=[pltpu.VMEM((r, CUM_BLK), F32)],
        compiler_params=_cparams(("arbitrary",)),
        name="cumsum_time",
    )(x)


def _online_softmax_step(s, v, m_ref, l_ref, acc_ref):
    m_prev = m_ref[...]
    m_new = jnp.maximum(m_prev, jnp.max(s, axis=-1, keepdims=True))
    alpha = jnp.exp(m_prev - m_new)
    p = jnp.exp(s - m_new)
    l_ref[...] = alpha * l_ref[...] + jnp.sum(p, axis=-1, keepdims=True)
    acc_ref[...] = alpha * acc_ref[...] + jnp.dot(p.astype(BF16), v, preferred_element_type=F32)
    m_ref[...] = m_new


def _split3_bf16(x):
    hi = x.astype(BF16)
    r1 = x - hi.astype(F32)
    mid = r1.astype(BF16)
    lo = (r1 - mid.astype(F32)).astype(BF16)
    return jnp.concatenate([hi, mid, lo], axis=-1)


def _fox_aug_placement():
    pq = np.zeros((3 * LANES, D_MODEL), np.float32)
    pk = np.zeros((3 * LANES, D_MODEL), np.float32)
    cq = np.zeros((1, D_MODEL), np.float32)
    ck = np.zeros((1, D_MODEL), np.float32)
    for h in range(N_HEADS):
        for piece in range(3):
            pq[piece * LANES + h, h * HEAD_DIM + piece] = 1.0
            pk[piece * LANES + h, h * HEAD_DIM + 3 + piece] = -1.0
            cq[0, h * HEAD_DIM + 3 + piece] = 1.0
            ck[0, h * HEAD_DIM + piece] = 1.0
    return (jnp.asarray(pq, BF16), jnp.asarray(pk, BF16), jnp.asarray(cq), jnp.asarray(ck))


def _fox_aug_kernel(x_ref, pq_ref, pk_ref, cq_ref, ck_ref, q_ref, k_ref, e_ref,
                    aq_ref, ak_ref, cum_ref, qn_ref, kn_ref, carry_ref):
    @pl.when(pl.program_id(0) == 0)
    def _():
        carry_ref[...] = jnp.zeros_like(carry_ref)
        qn_ref[...] = jnp.zeros_like(qn_ref)
        kn_ref[...] = jnp.zeros_like(kn_ref)

    for a_ref, n_ref in ((q_ref, qn_ref), (k_ref, kn_ref)):
        a = a_ref[...].astype(F32)
        n2 = jnp.dot((a * a).astype(BF16), e_ref[...], preferred_element_type=F32)
        n_ref[...] = jnp.maximum(n_ref[...], jnp.max(n2, axis=0, keepdims=True))

    tb = x_ref.shape[0]
    row = lax.broadcasted_iota(jnp.int32, (tb, tb), 0)
    col = lax.broadcasted_iota(jnp.int32, (tb, tb), 1)
    tri = (col <= row).astype(BF16)
    x3 = _split3_bf16(x_ref[...])
    c3 = jnp.dot(tri, x3, preferred_element_type=F32)
    cum = (c3[:, 2 * LANES:] + c3[:, LANES:2 * LANES]) + c3[:, :LANES] + carry_ref[...]
    carry_ref[...] = cum[tb - 1:tb, :]
    cum_ref[...] = cum
    cum3 = _split3_bf16(cum)
    aq = jnp.dot(cum3, pq_ref[...], preferred_element_type=F32) + cq_ref[...]
    ak = jnp.dot(cum3, pk_ref[...], preferred_element_type=F32) + ck_ref[...]
    aq_ref[...] = aq.astype(BF16)
    ak_ref[...] = ak.astype(BF16)


def _fox_aug(logf_pad, q, k):
    t_len = logf_pad.shape[0]
    tb = 2 * LANES
    pq, pk, cq, ck = _fox_aug_placement()
    head_of_col = np.arange(D_MODEL)[:, None] // HEAD_DIM == np.arange(LANES)[None, :]
    const = lambda shape: pl.BlockSpec(shape, lambda i: (0, 0))
    rows = lambda width: pl.BlockSpec((tb, width), lambda i: (i, 0))
    return pl.pallas_call(
        _fox_aug_kernel,
        grid=(t_len // tb,),
        in_specs=[
            rows(LANES),
            const((3 * LANES, D_MODEL)), const((3 * LANES, D_MODEL)),
            const((1, D_MODEL)), const((1, D_MODEL)),
            rows(D_MODEL), rows(D_MODEL), const((D_MODEL, LANES)),
        ],
        out_specs=[rows(D_MODEL), rows(D_MODEL), rows(LANES), const((1, LANES)), const((1, LANES))],
        out_shape=[jax.ShapeDtypeStruct((t_len, D_MODEL), BF16)] * 2
        + [jax.ShapeDtypeStruct((t_len, LANES), F32)]
        + [jax.ShapeDtypeStruct((1, LANES), F32)] * 2,
        scratch_shapes=[pltpu.VMEM((1, LANES), F32)],
        compiler_params=_cparams(("arbitrary",)),
        name="fox_aug",
    )(logf_pad, pq, pk, cq, ck, q, k, jnp.asarray(head_of_col, BF16))


def _fox_prompt_kernel(start_ref, q_ref, aq_ref, k_ref, ak_ref, v_ref, o_ref,
                       s0_ref, s1_ref, p0_ref, p1_ref, a0_ref, a1_ref, m_ref, acc_ref):
    qi = pl.program_id(1)
    j0 = start_ref[pl.program_id(0), qi]
    tq, tk = TQ_FOX, TK_FOX
    dn = (((1,), (1,)), ((), ()))
    q = jnp.concatenate([q_ref[...], aq_ref[...]], axis=1)
    ones_col = (lax.broadcasted_iota(jnp.int32, (tk, HEAD_DIM), 1) == 0).astype(BF16)

    everything, lower = slice(0, tq), slice(tk, tq)

    def stat(ref, rows):
        return ref[...].reshape(tq, 1) if rows is everything else ref[1]

    def set_stat(ref, rows, value):
        if rows is everything:
            ref[...] = value.reshape(2, tk, 1)
        else:
            ref[1] = value

    def scores(ki, s_ref, rows=everything):
        ks = pl.multiple_of(ki * tk, tk)
        k = jnp.concatenate([k_ref[pl.ds(ks, tk), :], ak_ref[pl.ds(ks, tk), :]], axis=1)
        s_ref[rows, :] = lax.dot_general(q[rows], k, dn, preferred_element_type=F32)

    def accumulate(ki, p_ref, a_ref, rows=everything):
        ks = pl.multiple_of(ki * tk, tk)
        v = jnp.concatenate([v_ref[pl.ds(ks, tk), :], ones_col], axis=1)
        pv = jnp.dot(p_ref[rows, :], v, preferred_element_type=F32)
        acc_ref[rows, :] = stat(a_ref, rows) * acc_ref[rows, :] + pv

    def softmax(s, p_ref, a_ref, rows=everything):
        m_prev = stat(m_ref, rows)
        m_new = jnp.maximum(m_prev, jnp.max(s, axis=-1, keepdims=True))
        set_stat(a_ref, rows, jnp.exp(m_prev - m_new))
        p_ref[rows, :] = jnp.exp(s - m_new).astype(BF16)
        set_stat(m_ref, rows, m_new)

    m_ref[...] = jnp.full_like(m_ref, -jnp.inf)
    acc_ref[...] = jnp.zeros_like(acc_ref)
    p1_ref[...] = jnp.zeros_like(p1_ref)
    a1_ref[...] = jnp.ones_like(a1_ref)
    scores(2 * j0, s0_ref)

    def body(j, carry):
        scores(2 * j + 1, s1_ref)
        accumulate(jnp.maximum(2 * j - 1, 0), p1_ref, a1_ref)
        softmax(s0_ref[...], p0_ref, a0_ref)
        scores(2 * j + 2, s0_ref)
        accumulate(2 * j, p0_ref, a0_ref)
        softmax(s1_ref[...], p1_ref, a1_ref)
        return carry

    lax.fori_loop(j0, qi, body, 0)

    row = lax.broadcasted_iota(jnp.int32, (tq, tk), 0)
    col = lax.broadcasted_iota(jnp.int32, (tq, tk), 1)
    scores(2 * qi + 1, s1_ref, lower)
    accumulate(jnp.maximum(2 * qi - 1, 0), p1_ref, a1_ref)
    softmax(jnp.where(col <= row, s0_ref[...], -jnp.inf), p0_ref, a0_ref)
    accumulate(2 * qi, p0_ref, a0_ref)
    row_l = lax.broadcasted_iota(jnp.int32, (tq - tk, tk), 0)
    col_l = lax.broadcasted_iota(jnp.int32, (tq - tk, tk), 1)
    softmax(jnp.where(col_l <= row_l, s1_ref[lower, :], -jnp.inf), p1_ref, a1_ref, lower)
    accumulate(2 * qi + 1, p1_ref, a1_ref, lower)
    acc = acc_ref[...]
    o_ref[...] = (acc[:, :HEAD_DIM] / acc[:, HEAD_DIM:HEAD_DIM + 1]).astype(BF16)


def _fox_first_live_pair(cum_pad, qn2, kn2):
    nq = cum_pad.shape[0] // TQ_FOX
    slack = 2.0 * 1.01 * jnp.sqrt(qn2[0, :N_HEADS] * kn2[0, :N_HEADS])
    cum = cum_pad[:, :N_HEADS]
    c_tile_start = cum[0::TQ_FOX]
    c_pair_end = cum[TQ_FOX - 1::TQ_FOX]
    bound = c_tile_start[:, None, :] - c_pair_end[None, :, :] + slack[None, None, :]
    before = np.arange(nq)[None, :, None] < np.arange(nq)[:, None, None]
    dead = jnp.logical_and(bound < -FOX_DEAD_LOGIT, before)
    lead = jnp.cumprod(dead.astype(jnp.int32), axis=1)
    return jnp.transpose(jnp.sum(lead, axis=1)).astype(jnp.int32)


def _fox_prompt_attn(q, aq, k, ak, v, first_pair):
    t_len = q.shape[0]
    tq, tk = TQ_FOX, TK_FOX
    assert tq == 2 * tk and t_len % tq == 0
    tile = pl.BlockSpec((tq, HEAD_DIM), lambda h, i, start: (i, h))
    whole = pl.BlockSpec((t_len, HEAD_DIM), lambda h, i, start: (0, h))
    return pl.pallas_call(
        _fox_prompt_kernel,
        grid_spec=pltpu.PrefetchScalarGridSpec(
            num_scalar_prefetch=1,
            grid=(N_HEADS, t_len // tq),
            in_specs=[tile, tile, whole, whole, whole],
            out_specs=tile,
            scratch_shapes=[
                pltpu.VMEM((tq, tk), F32), pltpu.VMEM((tq, tk), F32),
                pltpu.VMEM((tq, tk), BF16), pltpu.VMEM((tq, tk), BF16),
                pltpu.VMEM((2, tk, 1), F32), pltpu.VMEM((2, tk, 1), F32),
                pltpu.VMEM((2, tk, 1), F32),
                pltpu.VMEM((tq, 2 * HEAD_DIM), F32),
            ],
        ),
        out_shape=jax.ShapeDtypeStruct((t_len, D_MODEL), BF16),
        compiler_params=_cparams(("arbitrary", "arbitrary")),
        name="fox_prompt_attn",
    )(first_pair, q, aq, k, ak, v)


def _fox_sample_kernel(q_ref, kn_ref, vn_ref, kc_ref, vc_ref, cq_ref, ckc_ref, ckn_ref,
                       o_ref, s_ref, sn_ref, m_ref, acc_ref, *, n_new, tk):
    ki = pl.program_id(1)
    nk = pl.num_programs(1)
    dn = (((1,), (1,)), ((), ()))

    @pl.when(ki == 0)
    def _():
        m_ref[...] = jnp.full_like(m_ref, -jnp.inf)
        acc_ref[...] = jnp.zeros_like(acc_ref)

    def head_rows(h):
        return slice(h * n_new, (h + 1) * n_new)

    def head_cols(h):
        return slice(h * HEAD_DIM, (h + 1) * HEAD_DIM)

    def softmax_update(s):
        m_prev = m_ref[...]
        m_new = jnp.maximum(m_prev, jnp.max(s, axis=-1, keepdims=True))
        m_ref[...] = m_new
        return jnp.exp(m_prev - m_new), jnp.exp(s - m_new).astype(BF16)

    def accumulate(alpha, p, values):
        n_keys = p.shape[1]
        ones_col = (lax.broadcasted_iota(jnp.int32, (n_keys, HEAD_DIM), 1) == 0).astype(BF16)
        for h in range(N_HEADS):
            r = head_rows(h)
            pv = jnp.dot(p[r], jnp.concatenate([values(h), ones_col], axis=1),
                         preferred_element_type=F32)
            acc_ref[r, :] = alpha[r] * acc_ref[r, :] + pv

    for h in range(N_HEADS):
        kc = kc_ref[0, pl.ds(h, tk, stride=N_HEADS), :].astype(BF16)
        s = lax.dot_general(q_ref[:, head_cols(h)], kc, dn, preferred_element_type=F32)
        s_ref[head_rows(h), :] = (s + cq_ref[0, :, h:h + 1]) - ckc_ref[0, h:h + 1, :]
    alpha, p = softmax_update(s_ref[...])
    accumulate(alpha, p, lambda h: vc_ref[0, pl.ds(h, tk, stride=N_HEADS), :].astype(BF16))

    @pl.when(ki == nk - 1)
    def _():
        row = lax.broadcasted_iota(jnp.int32, (n_new, n_new), 0)
        col = lax.broadcasted_iota(jnp.int32, (n_new, n_new), 1)
        for h in range(N_HEADS):
            s = lax.dot_general(q_ref[:, head_cols(h)], kn_ref[:, head_cols(h)], dn,
                                preferred_element_type=F32)
            s = (s + cq_ref[0, :, h:h + 1]) - ckn_ref[0, h:h + 1, 0:n_new]
            sn_ref[head_rows(h), :] = jnp.where(col <= row, s, -jnp.inf)
        alpha, p = softmax_update(sn_ref[...])
        accumulate(alpha, p, lambda h: vn_ref[:, head_cols(h)])
        for h in range(N_HEADS):
            acc = acc_ref[head_rows(h), :]
            o_ref[:, head_cols(h)] = (acc[:, :HEAD_DIM] / acc[:, HEAD_DIM:HEAD_DIM + 1]).astype(BF16)


def _fox_sample_attn(q, kn, vn, cache_k, cache_v, layer, cq_bth, cum_bht, n_streams, n_new):
    past = cache_k.shape[1] // N_HEADS
    tk = TK_FOX_S
    first = layer * n_streams
    rows = N_HEADS * n_new
    return pl.pallas_call(
        functools.partial(_fox_sample_kernel, n_new=n_new, tk=tk),
        grid=(n_streams, past // tk),
        in_specs=[
            pl.BlockSpec((n_new, D_MODEL), lambda b, i: (b, 0)),
            pl.BlockSpec((n_new, D_MODEL), lambda b, i: (b, 0)),
            pl.BlockSpec((n_new, D_MODEL), lambda b, i: (b, 0)),
            pl.BlockSpec((1, tk * N_HEADS, HEAD_DIM), lambda b, i: (first + b, i, 0)),
            pl.BlockSpec((1, tk * N_HEADS, HEAD_DIM), lambda b, i: (first + b, i, 0)),
            pl.BlockSpec((1, n_new, N_HEADS), lambda b, i: (b, 0, 0)),
            pl.BlockSpec((1, N_HEADS, tk), lambda b, i: (b, 0, i)),
            pl.BlockSpec((1, N_HEADS, LANES), lambda b, i: (b, 0, past // LANES)),
        ],
        out_specs=pl.BlockSpec((n_new, D_MODEL), lambda b, i: (b, 0)),
        out_shape=jax.ShapeDtypeStruct((n_streams * n_new, D_MODEL), BF16),
        scratch_shapes=[
            pltpu.VMEM((rows, tk), F32),
            pltpu.VMEM((rows, n_new), F32),
            pltpu.VMEM((rows, 1), F32),
            pltpu.VMEM((rows, 2 * HEAD_DIM), F32),
        ],
        compiler_params=_cparams(("arbitrary", "arbitrary")),
        name="fox_sample_attn",
    )(q, kn, vn, cache_k, cache_v, cq_bth, cum_bht, cum_bht)


def _band_profile(rel_table, d0, n_rows, width):
    dist = d0 + (n_rows - 1) - np.arange(width)
    idx = np.clip(dist, -MAX_REL, MAX_REL) + MAX_REL
    return rel_table.astype(F32)[:, idx].reshape(N_HEADS, 1, width)


def _toeplitz_rows(g, n_rows, n_cols):
    width = g.shape[-1]
    assert n_cols + n_rows - 1 <= width and width % LANES == 0
    r = pltpu.roll(jnp.broadcast_to(g, (n_rows, width)), width - (n_rows - 1), axis=1,
                   stride=1, stride_axis=0)
    return r[:, :n_cols]


def _softmax_rows(s):
    m = jnp.max(s, axis=-1, keepdims=True)
    p = jnp.exp(s - m)
    return p, jnp.sum(p, axis=-1, keepdims=True)


def _band_prompt_kernel(q_ref, kp_ref, kc_ref, vp_ref, vc_ref, g_ref, o_ref, b_ref):
    qi = pl.program_id(1)
    t = TQ_BAND

    def attend(first_block):
        k = jnp.concatenate([kp_ref[...], kc_ref[...]], axis=0)
        v = jnp.concatenate([vp_ref[...], vc_ref[...]], axis=0)
        col = lax.broadcasted_iota(jnp.int32, (t, 3 * t), 1)
        for sub in range(TB_BAND // t):
            lo = sub * t
            q = q_ref[lo:lo + t, :]
            s = lax.dot_general(q, k[lo:lo + 3 * t], (((1,), (1,)), ((), ())),
                                preferred_element_type=F32) + b_ref[...]
            if first_block and lo < BAND_PAST:
                s = jnp.where(col >= BAND_PAST - lo, s, -jnp.inf)
            p, l = _softmax_rows(s)
            o = jnp.dot(p.astype(BF16), v[lo:lo + 3 * t], preferred_element_type=F32)
            o_ref[lo:lo + t, :] = (o / l).astype(BF16)

    @pl.when(qi == 0)
    def _():
        qc = lax.broadcasted_iota(jnp.int32, (t, 3 * t), 0) // CHUNK
        kc = lax.broadcasted_iota(jnp.int32, (t, 3 * t), 1) // CHUNK
        ok = (kc >= qc) & (kc <= qc + BAND_CHUNKS)
        b_ref[...] = jnp.where(ok, _toeplitz_rows(g_ref[0], t, 3 * t), -jnp.inf)
        attend(first_block=True)

    @pl.when(qi > 0)
    def _():
        attend(first_block=False)


def _band_prompt_attn(q, k, v, profile):
    t_len = q.shape[0]
    t = TQ_BAND
    tb = TB_BAND
    per = tb // BAND_PAST
    assert BAND_PAST == 2 * t and t % CHUNK == 0 and tb % BAND_PAST == 0 and t_len % tb == 0
    cur = pl.BlockSpec((tb, HEAD_DIM), lambda h, i: (i, h))
    prev = pl.BlockSpec((BAND_PAST, HEAD_DIM), lambda h, i: (jnp.maximum(i * per - 1, 0), h))
    return pl.pallas_call(
        _band_prompt_kernel,
        grid=(N_HEADS, t_len // tb),
        in_specs=[cur, prev, cur, prev, cur,
                  pl.BlockSpec((1, 1, 4 * t), lambda h, i: (h, 0, 0))],
        out_specs=cur,
        out_shape=jax.ShapeDtypeStruct((t_len, D_MODEL), BF16),
        scratch_shapes=[pltpu.VMEM((t, 3 * t), F32)],
        compiler_params=_cparams(("arbitrary", "arbitrary")),
        name="band_prompt_attn",
    )(q, k, k, v, v, profile)


def _band_sample_kernel(q_ref, kn_ref, vn_ref, kc_ref, vc_ref, g_ref, mask_ref, o_ref, b_ref, *,
                        n_cache):
    n_new = q_ref.shape[0]

    @pl.when(pl.program_id(0) == 0)
    def _():
        for h in range(N_HEADS):
            b_ref[h] = _toeplitz_rows(g_ref[h], n_new, n_cache + n_new) + mask_ref[...]

    for h in range(N_HEADS):
        hs = slice(h * HEAD_DIM, (h + 1) * HEAD_DIM)
        q = q_ref[:, hs]
        kc = kc_ref[0, pl.ds(h, n_cache, stride=N_HEADS), :].astype(BF16)
        vc = vc_ref[0, pl.ds(h, n_cache, stride=N_HEADS), :].astype(BF16)
        dn = (((1,), (1,)), ((), ()))
        s1 = lax.dot_general(q, kc, dn, preferred_element_type=F32) + b_ref[h, :, 0:n_cache]
        s2 = lax.dot_general(q, kn_ref[:, hs], dn, preferred_element_type=F32) + b_ref[h, :, n_cache:]
        m = jnp.maximum(jnp.max(s1, axis=-1, keepdims=True), jnp.max(s2, axis=-1, keepdims=True))
        p1 = jnp.exp(s1 - m)
        p2 = jnp.exp(s2 - m)
        l = jnp.sum(p1, axis=-1, keepdims=True) + jnp.sum(p2, axis=-1, keepdims=True)
        o = jnp.dot(p1.astype(BF16), vc, preferred_element_type=F32)
        o = o + jnp.dot(p2.astype(BF16), vn_ref[:, hs], preferred_element_type=F32)
        o_ref[:, hs] = (o / l).astype(BF16)


def _band_sample_attn(q, kn, vn, cache_k, cache_v, layer, rel_table, past, n_streams, n_new):
    n_cache = cache_k.shape[1] // N_HEADS
    n_keys = n_cache + n_new
    first = layer * n_streams
    q_pos = past + np.arange(n_new)
    k_pos = past - n_cache + np.arange(n_keys)
    qc, kc = q_pos[:, None] // CHUNK, k_pos[None, :] // CHUNK
    ok = (k_pos[None, :] >= 0) & (kc <= qc) & (kc >= qc - BAND_CHUNKS)
    mask = jnp.asarray(np.where(ok, 0.0, -np.inf), F32)
    width = -(-(n_keys + n_new - 1) // LANES) * LANES
    profile = _band_profile(rel_table, n_cache, n_new, width)
    return pl.pallas_call(
        functools.partial(_band_sample_kernel, n_cache=n_cache),
        grid=(n_streams,),
        in_specs=[
            pl.BlockSpec((n_new, D_MODEL), lambda b: (b, 0)),
            pl.BlockSpec((n_new, D_MODEL), lambda b: (b, 0)),
            pl.BlockSpec((n_new, D_MODEL), lambda b: (b, 0)),
            pl.BlockSpec((1, n_cache * N_HEADS, HEAD_DIM), lambda b: (first + b, 0, 0)),
            pl.BlockSpec((1, n_cache * N_HEADS, HEAD_DIM), lambda b: (first + b, 0, 0)),
            pl.BlockSpec((N_HEADS, 1, width), lambda b: (0, 0, 0)),
            pl.BlockSpec((n_new, n_keys), lambda b: (0, 0)),
        ],
        out_specs=pl.BlockSpec((n_new, D_MODEL), lambda b: (b, 0)),
        out_shape=jax.ShapeDtypeStruct((n_streams * n_new, D_MODEL), BF16),
        scratch_shapes=[pltpu.VMEM((N_HEADS, n_new, n_keys), F32)],
        compiler_params=_cparams(("arbitrary",)),
        name="band_sample_attn",
    )(q, kn, vn, cache_k, cache_v, profile, mask)


def _oproj_kernel(x_ref, a_ref, w_ref, o_ref):
    o_ref[...] = x_ref[...] + jnp.dot(a_ref[...], w_ref[...], preferred_element_type=F32)


def _out_proj(x, a, w_o_b):
    m = x.shape[0]
    tm = min(TM, m)
    return pl.pallas_call(
        _oproj_kernel,
        grid=(m // tm,),
        in_specs=[
            pl.BlockSpec((tm, D_MODEL), lambda i: (i, 0)),
            pl.BlockSpec((tm, D_MODEL), lambda i: (i, 0)),
            pl.BlockSpec((D_MODEL, D_MODEL), lambda i: (0, 0)),
        ],
        out_specs=pl.BlockSpec((tm, D_MODEL), lambda i: (i, 0)),
        out_shape=jax.ShapeDtypeStruct((m, D_MODEL), F32),
        compiler_params=_cparams(("arbitrary",)),
        name="out_proj",
    )(x, a, w_o_b)


def _ffn_kernel(*refs, stream_len, tm, n_f, final_norm):
    if final_norm:
        (x_ref, g_ref, wg_ref, wv_ref, hg_ref, hv_ref, cwg_ref, cwv_ref, cbg_ref, cbv_ref,
         wo_ref, gf_ref, o_ref, tg_ref, tv_ref, h_ref, pg_ref, pv_ref, u_ref) = refs
    else:
        (x_ref, g_ref, wg_ref, wv_ref, hg_ref, hv_ref, cwg_ref, cwv_ref, cbg_ref, cbv_ref,
         wo_ref, o_ref, tg_ref, tv_ref, h_ref, pg_ref, pv_ref, u_ref) = refs
        gf_ref = None
    i = pl.program_id(0)
    f = pl.program_id(1)
    streams_per_tile = max(tm // stream_len, 1)
    tiles_per_stream = max(stream_len // tm, 1)

    @pl.when(f == 0)
    def _():
        h_ref[...] = _rmsnorm_f32(x_ref[...], g_ref[...]).astype(BF16)
        o_ref[...] = x_ref[...]

    n_groups = FFN_ROW_GROUPS
    rg = tm // n_groups
    assert rg % SUBLANES == 0 and (streams_per_tile == 1 or streams_per_tile % n_groups == 0)
    row = lax.broadcasted_iota(jnp.int32, (rg, TF), 0)
    if streams_per_tile > 1:
        row = row % stream_len
    halves = ((wg_ref, hg_ref, cwg_ref, cbg_ref, pg_ref, tg_ref),
              (wv_ref, hv_ref, cwv_ref, cbv_ref, pv_ref, tv_ref))

    class _Up:
        def __init__(self, g, half):
            self.ref = u_ref.at[half * n_groups + g]
            self.ref[...] = jnp.dot(h_ref[g * rg:(g + 1) * rg, :], halves[half][0][...],
                                    preferred_element_type=F32)

        def __getitem__(self, idx):
            return self.ref[idx]

        def reshape(self, *shape):
            return self.ref[...].reshape(*shape)

    if streams_per_tile == 1:
        @pl.when((i % tiles_per_stream) == 0)
        def _():
            for (_, hist_ref, _, _, prev_ref, _) in halves:
                prev_ref[f, SUBLANES - (CONV_W - 1):SUBLANES, :] = hist_ref[0]

    ups = []

    def conv(g, half):
        _, hist_ref, cw_ref, cb_ref, prev_ref, _ = halves[half]
        u = ups[g][half][...]
        if streams_per_tile > 1:
            spg = streams_per_tile // n_groups
            def per_row(r):
                hr = hist_ref[g * spg:(g + 1) * spg, r:r + 1, :]
                return jnp.broadcast_to(hr, (spg, stream_len, TF)).reshape(rg, TF)
            p0, p1 = per_row(0), per_row(1)
        elif g == 0:
            p0 = prev_ref[f, SUBLANES - 2:SUBLANES - 1, :]
            p1 = prev_ref[f, SUBLANES - 1:SUBLANES, :]
        else:
            p0 = ups[g - 1][half][rg - 2:rg - 1, :]
            p1 = ups[g - 1][half][rg - 1:rg, :]
        u1 = jnp.where(row == 0, p1, pltpu.roll(u, 1, axis=0))
        u2 = jnp.where(row == 0, p0, jnp.where(row == 1, p1, pltpu.roll(u, 2, axis=0)))
        c = cb_ref[...] + cw_ref[0:1, :] * u2
        c = c + cw_ref[1:2, :] * u1
        return c + cw_ref[2:3, :] * u

    ups.append([_Up(0, half) for half in range(2)])
    for g in range(n_groups):
        cg = conv(g, 0)
        cv = conv(g, 1)
        a = ((cg * jax.nn.sigmoid(cg)) * cv).astype(BF16)
        if g + 1 < n_groups:
            ups.append([_Up(g + 1, half) for half in range(2)])
        o_ref[g * rg:(g + 1) * rg, :] += jnp.dot(a, wo_ref[...], preferred_element_type=F32)

    for half, (_, _, _, _, prev_ref, tail_ref) in enumerate(halves):
        if streams_per_tile > 1:
            spg = streams_per_tile // n_groups
            for g in range(n_groups):
                u3 = ups[g][half].reshape(spg, stream_len, TF)
                tail_ref[g * spg:(g + 1) * spg] = u3[:, stream_len - SUBLANES:, :]
        else:
            last = ups[n_groups - 1][half][rg - SUBLANES:, :]
            prev_ref[f] = last
            tail_ref[0] = last

    if final_norm:
        @pl.when(f == n_f - 1)
        def _():
            o_ref[...] = _rmsnorm_f32(o_ref[...], gf_ref[...])


def _conv_ffn(x, g, w_in_g, w_in_v, hist_g, hist_v, cw_g, cw_v, cb_g, cb_v, w_out_b, stream_len,
              g_final=None):
    m = x.shape[0]
    tm = min(TM_FFN, m)
    nf = D_FF_PAD // TF
    n_tiles = m // tm
    final_norm = g_final is not None
    if stream_len >= tm:
        assert stream_len % tm == 0
        tiles_per_stream = stream_len // tm
        hist_spec = pl.BlockSpec((1, CONV_W - 1, TF), lambda i, f: (i // tiles_per_stream, 0, f))
        tail_rows = n_tiles
        tail_spec = pl.BlockSpec((1, SUBLANES, TF), lambda i, f: (i, 0, f))
    else:
        assert tm % stream_len == 0 and stream_len % SUBLANES == 0 and n_tiles == 1
        spt = tm // stream_len
        hist_spec = pl.BlockSpec((spt, CONV_W - 1, TF), lambda i, f: (i, 0, f))
        tail_rows = m // stream_len
        tail_spec = pl.BlockSpec((spt, SUBLANES, TF), lambda i, f: (i, 0, f))
    row_tile = pl.BlockSpec((tm, D_MODEL), lambda i, f: (i, 0))
    gain = pl.BlockSpec((1, D_MODEL), lambda i, f: (0, 0))
    w_in_spec = pl.BlockSpec((D_MODEL, TF), lambda i, f: (0, f))
    cw_spec = pl.BlockSpec((CONV_W, TF), lambda i, f: (0, f))
    cb_spec = pl.BlockSpec((1, TF), lambda i, f: (0, f))
    in_specs = [row_tile, gain, w_in_spec, w_in_spec, hist_spec, hist_spec, cw_spec, cw_spec,
                cb_spec, cb_spec, pl.BlockSpec((TF, D_MODEL), lambda i, f: (f, 0))]
    args = [x, g.reshape(1, D_MODEL), w_in_g, w_in_v, hist_g, hist_v, cw_g, cw_v, cb_g, cb_v, w_out_b]
    if final_norm:
        in_specs.append(gain)
        args.append(g_final.reshape(1, D_MODEL))
    tail_shape = jax.ShapeDtypeStruct((tail_rows, SUBLANES, D_FF_PAD), F32)
    return pl.pallas_call(
        functools.partial(_ffn_kernel, stream_len=stream_len, tm=tm, n_f=nf, final_norm=final_norm),
        grid=(n_tiles, nf),
        in_specs=in_specs,
        out_specs=[row_tile, tail_spec, tail_spec],
        out_shape=[jax.ShapeDtypeStruct((m, D_MODEL), F32), tail_shape, tail_shape],
        scratch_shapes=[
            pltpu.VMEM((tm, D_MODEL), BF16),
            pltpu.VMEM((nf, SUBLANES, TF), F32),
            pltpu.VMEM((nf, SUBLANES, TF), F32),
            pltpu.VMEM((2 * FFN_ROW_GROUPS, tm // FFN_ROW_GROUPS, TF), F32),
        ],
        compiler_params=_cparams(("arbitrary", "arbitrary")),
        name="conv_ffn",
    )(*args)


def _pad_ff(a):
    pad = [(0, 0)] * (a.ndim - 1) + [(0, D_FF_PAD - D_FF)]
    return jnp.pad(a, pad)


def _split_ff(a):
    return _pad_ff(a[..., :D_FF]), _pad_ff(a[..., D_FF:])


def kernel(x_prompt, x_sample, cache_fox_k, cache_fox_v, cache_fox_logf, cache_band_k, cache_band_v,
           state_conv, g_attn, w_qkv, w_fgate, b_fgate, rel_bias, w_o, g_ffn, w_in, conv_w, conv_b,
           w_out, g_final):
    depth = g_attn.shape[0]
    bp, sp, _ = x_prompt.shape
    bs, ss, _ = x_sample.shape
    past = cache_fox_k.shape[2]
    n_band_cache = cache_band_k.shape[2]
    n_keep = min(BAND_PAST, sp)
    assert bp == 1

    xp = x_prompt.reshape(bp * sp, D_MODEL)
    xs = x_sample.reshape(bs * ss, D_MODEL)

    n_fox = cache_fox_k.shape[0]
    n_band = cache_band_k.shape[0]
    cfk = cache_fox_k.reshape(n_fox * bs, past * N_HEADS, HEAD_DIM)
    cfv = cache_fox_v.reshape(n_fox * bs, past * N_HEADS, HEAD_DIM)
    cbk = cache_band_k.reshape(n_band * bs, n_band_cache * N_HEADS, HEAD_DIM)
    cbv = cache_band_v.reshape(n_band * bs, n_band_cache * N_HEADS, HEAD_DIM)

    hist0 = jnp.zeros((bp, CONV_W - 1, D_FF_PAD), F32)

    fox_lp, fox_ks, fox_vs, fox_ls = [], [], [], []
    band_kp, band_vp, band_ks, band_vs = [], [], [], []
    conv_p, conv_s = [], []
    fox_kv_p = None

    for i in range(depth):
        j = i // N_MIXERS
        is_fox = (i % N_MIXERS == 0)
        wqkv_b = w_qkv[i].astype(BF16)
        wo_b = w_o[i].astype(BF16)
        w_in_g, w_in_v = _split_ff(w_in[i].astype(BF16))
        cw_g, cw_v = _split_ff(conv_w[i])
        cb_g, cb_v = _split_ff(conv_b[i].reshape(1, 2 * D_FF))
        w_out_b = jnp.pad(w_out[i].astype(BF16), ((0, D_FF_PAD - D_FF), (0, 0)))
        hs_g, hs_v = _split_ff(state_conv[i])

        if is_fox:
            wf_b = jnp.pad(w_fgate[j].astype(BF16), ((0, 0), (0, LANES - N_HEADS)))
            bf = jnp.pad(b_fgate[j].reshape(1, N_HEADS), ((0, 0), (0, LANES - N_HEADS)))
            qp, kst, vst, kpb, vpb, lfp_pad = _qkv_proj(xp, g_attn[i], wqkv_b, wf_b, bf,
                                                        slot=j, n_slots=n_fox, kv_stack=fox_kv_p)
            fox_kv_p = (kst, vst)
            qs, ks32, vs32, ksb, vsb, lfs = _qkv_proj(xs, g_attn[i], wqkv_b, wf_b, bf)
            lfp = lfp_pad[:, :N_HEADS]
            lfs = lfs[:, :N_HEADS]

            aq, ak, cum_pad, qn2, kn2 = _fox_aug(lfp_pad, qp, kpb)
            op = _fox_prompt_attn(qp, aq, kpb, ak, vpb, _fox_first_live_pair(cum_pad, qn2, kn2))

            lf_all = jnp.concatenate([
                jnp.transpose(cache_fox_logf[j], (0, 2, 1)),
                jnp.transpose(lfs.reshape(bs, ss, N_HEADS), (0, 2, 1)),
                jnp.zeros((bs, N_HEADS, LANES - ss), F32),
            ], axis=-1).reshape(bs * N_HEADS, past + LANES)
            cum_s = _cumsum_lanes(lf_all).reshape(bs, N_HEADS, past + LANES)
            cq_s = jnp.transpose(cum_s[:, :, past:past + ss], (0, 2, 1))
            osm = _fox_sample_attn(qs, ksb, vsb, cfk, cfv, j, cq_s, cum_s, bs, ss)

            fox_lp.append(lfp.reshape(bp, sp, N_HEADS))
            fox_ks.append(ks32.reshape(bs, ss, N_HEADS, HEAD_DIM))
            fox_vs.append(vs32.reshape(bs, ss, N_HEADS, HEAD_DIM))
            fox_ls.append(lfs.reshape(bs, ss, N_HEADS))
        else:
            qp, kp32, vp32, kpb, vpb = _qkv_proj(xp, g_attn[i], wqkv_b)
            qs, ks32, vs32, ksb, vsb = _qkv_proj(xs, g_attn[i], wqkv_b)
            op = _band_prompt_attn(qp, kpb, vpb,
                                   _band_profile(rel_bias[j], BAND_PAST, TQ_BAND, 4 * TQ_BAND))
            osm = _band_sample_attn(qs, ksb, vsb, cbk, cbv, j, rel_bias[j], past, bs, ss)
            last_rows = slice(sp - n_keep, sp)
            band_kp.append(kp32[0, last_rows].reshape(bp, n_keep, N_HEADS, HEAD_DIM))
            band_vp.append(vp32[0, last_rows].reshape(bp, n_keep, N_HEADS, HEAD_DIM))
            band_ks.append(ks32.reshape(bs, ss, N_HEADS, HEAD_DIM))
            band_vs.append(vs32.reshape(bs, ss, N_HEADS, HEAD_DIM))

        xp = _out_proj(xp, op, wo_b)
        xs = _out_proj(xs, osm, wo_b)

        gf = g_final if i == depth - 1 else None
        xp, tgp, tvp = _conv_ffn(xp, g_ffn[i], w_in_g, w_in_v, hist0, hist0, cw_g, cw_v, cb_g, cb_v,
                                 w_out_b, sp, gf)
        xs, tgs, tvs = _conv_ffn(xs, g_ffn[i], w_in_g, w_in_v, hs_g, hs_v, cw_g, cw_v, cb_g, cb_v,
                                 w_out_b, ss, gf)
        keep = slice(SUBLANES - (CONV_W - 1), SUBLANES)
        conv_p.append(jnp.concatenate([tgp[-1:, keep, :D_FF], tvp[-1:, keep, :D_FF]], axis=-1))
        conv_s.append(jnp.concatenate([tgs[:, keep, :D_FF], tvs[:, keep, :D_FF]], axis=-1))

    y_prompt = xp.reshape(bp, sp, D_MODEL)
    y_sample = xs.reshape(bs, ss, D_MODEL)
    fox_kp, fox_vp = (a.reshape(n_fox, bp, sp, N_HEADS, HEAD_DIM) for a in fox_kv_p)
    return (y_prompt, y_sample,
            fox_kp, fox_vp, jnp.stack(fox_lp),
            jnp.stack(fox_ks), jnp.stack(fox_vs), jnp.stack(fox_ls),
            jnp.stack(band_kp), jnp.stack(band_vp), jnp.stack(band_ks), jnp.stack(band_vs),
            jnp.stack(conv_p), jnp.stack(conv_s))
```

```python
import functools
import math

import numpy as np
import jax
import jax.numpy as jnp
from jax import lax
from jax.experimental import pallas as pl
from jax.experimental.pallas import tpu as pltpu

F32 = jnp.float32
BF16 = jnp.bfloat16

D_MODEL = 2048
N_HEADS = 16
HEAD_DIM = D_MODEL // N_HEADS
SCALE = 1.0 / math.sqrt(HEAD_DIM)
FOX_DEAD_LOGIT = 128.0
CHUNK = 64
BAND_CHUNKS = 8
BAND_PAST = BAND_CHUNKS * CHUNK
MAX_REL = 2 * CHUNK
D_FF = 5504
CONV_W = 3
EPS = 1e-6
N_MIXERS = 2

LANES = 128
SUBLANES = 8
VMEM_LIMIT = 56 * 1024 * 1024

TM = 512
TM_QKV = 256
TM_FFN = 512
TN_QKV = 512
TF = 512
FFN_ROW_GROUPS = 2
D_FF_PAD = ((D_FF + TF - 1) // TF) * TF
TK_FOX = 512
TQ_FOX = 2 * TK_FOX
TK_FOX_S = 512
TQ_BAND = 256
TB_BAND = 1024
CUM_BLK = 128


def _cparams(sem):
    return pltpu.CompilerParams(dimension_semantics=sem, vmem_limit_bytes=VMEM_LIMIT)


def _rmsnorm_f32(x, g):
    ms = jnp.mean(x * x, axis=-1, keepdims=True)
    return (x * lax.rsqrt(ms + EPS)) * g


def _qkv_kernel(*refs, with_fgate, n_aliased, q_scale):
    refs = refs[:3 + 2 * with_fgate] + refs[3 + 2 * with_fgate + n_aliased:]
    if with_fgate:
        (x_ref, g_ref, w_ref, wf_ref, bf_ref,
         q_ref, k32_ref, v32_ref, kb_ref, vb_ref, lf_ref) = refs
    else:
        x_ref, g_ref, w_ref, q_ref, k32_ref, v32_ref, kb_ref, vb_ref = refs

    h = _rmsnorm_f32(x_ref[...], g_ref[...]).astype(BF16)
    if with_fgate:
        z = jnp.dot(h, wf_ref[...], preferred_element_type=F32) + bf_ref[...]
        lf_ref[...] = jnp.minimum(z, 0.0) - jnp.log1p(jnp.exp(-jnp.abs(z)))

    def project(part, c):
        lo = part * D_MODEL + c * TN_QKV
        return jnp.dot(h, w_ref[:, lo:lo + TN_QKV], preferred_element_type=F32)

    for c in range(D_MODEL // TN_QKV):
        cs = slice(c * TN_QKV, (c + 1) * TN_QKV)
        q_ref[:, cs] = (project(0, c) * q_scale).astype(BF16)
        k = project(1, c)
        k32_ref[:, cs] = k
        kb_ref[:, cs] = k.astype(BF16)
        v = project(2, c)
        v32_ref[:, cs] = v
        vb_ref[:, cs] = v.astype(BF16)


def _qkv_proj(x, g, w_qkv_b, wf_b=None, bf=None, slot=0, n_slots=1, kv_stack=None, q_scale=SCALE):
    m = x.shape[0]
    tm = min(TM_QKV, m)
    with_fgate = wf_b is not None
    stack_tile = pl.BlockSpec((None, tm, D_MODEL), lambda i: (slot, i, 0))
    stack_shape = jax.ShapeDtypeStruct((n_slots, m, D_MODEL), F32)
    resident = lambda shape: pl.BlockSpec(shape, lambda i: (0, 0), pipeline_mode=pl.Buffered(1))
    in_specs = [
        pl.BlockSpec((tm, D_MODEL), lambda i: (i, 0)),
        resident((1, D_MODEL)),
        resident((D_MODEL, 3 * D_MODEL)),
    ]
    args = [x, g.reshape(1, D_MODEL), w_qkv_b]
    tile = pl.BlockSpec((tm, D_MODEL), lambda i: (i, 0))
    out_specs = [tile, stack_tile, stack_tile, tile, tile]
    out_shape = [
        jax.ShapeDtypeStruct((m, D_MODEL), BF16),
        stack_shape,
        stack_shape,
        jax.ShapeDtypeStruct((m, D_MODEL), BF16),
        jax.ShapeDtypeStruct((m, D_MODEL), BF16),
    ]
    if with_fgate:
        in_specs += [resident((D_MODEL, LANES)), resident((1, LANES))]
        args += [wf_b, bf]
        out_specs.append(pl.BlockSpec((tm, LANES), lambda i: (i, 0)))
        out_shape.append(jax.ShapeDtypeStruct((m, LANES), F32))
    aliases = {}
    if kv_stack is not None:
        aliases = {len(args): 1, len(args) + 1: 2}
        in_specs += [pl.BlockSpec(memory_space=pl.ANY)] * 2
        args += list(kv_stack)
    return pl.pallas_call(
        functools.partial(_qkv_kernel, with_fgate=with_fgate, n_aliased=len(aliases),
                          q_scale=q_scale),
        grid=(m // tm,),
        in_specs=in_specs,
        out_specs=out_specs,
        out_shape=out_shape,
        input_output_aliases=aliases,
        compiler_params=_cparams(("arbitrary",)),
        name="qkv_proj",
    )(*args)


def _cumsum_kernel(x_ref, o_ref, carry_ref):
    @pl.when(pl.program_id(0) == 0)
    def _():
        carry_ref[...] = jnp.zeros_like(carry_ref)

    x = x_ref[...]
    hi = x.astype(BF16)
    r1 = x - hi.astype(F32)
    mid = r1.astype(BF16)
    lo = (r1 - mid.astype(F32)).astype(BF16)
    row = lax.broadcasted_iota(jnp.int32, (CUM_BLK, CUM_BLK), 0)
    col = lax.broadcasted_iota(jnp.int32, (CUM_BLK, CUM_BLK), 1)
    tri = (row <= col).astype(BF16)
    s = jnp.dot(lo, tri, preferred_element_type=F32)
    s = s + jnp.dot(mid, tri, preferred_element_type=F32)
    s = s + jnp.dot(hi, tri, preferred_element_type=F32)
    s = s + carry_ref[...]
    o_ref[...] = s
    carry_ref[...] = jnp.broadcast_to(s[:, CUM_BLK - 1:CUM_BLK], s.shape)


def _cumsum_lanes(x):
    r, t = x.shape
    return pl.pallas_call(
        _cumsum_kernel,
        grid=(t // CUM_BLK,),
        in_specs=[pl.BlockSpec((r, CUM_BLK), lambda i: (0, i))],
        out_specs=pl.BlockSpec((r, CUM_BLK), lambda i: (0, i)),
        out_shape=jax.ShapeDtypeStruct((r, t), F32),
        scratch_shapes=[pltpu.VMEM((r, CUM_BLK), F32)],
        compiler_params=_cparams(("arbitrary",)),
        name="cumsum_time",
    )(x)


def _split3_bf16(x):
    hi = x.astype(BF16)
    r1 = x - hi.astype(F32)
    mid = r1.astype(BF16)
    lo = (r1 - mid.astype(F32)).astype(BF16)
    return jnp.concatenate([hi, mid, lo], axis=-1)


def _fox_aug_placement():
    pq = np.zeros((3 * LANES, D_MODEL), np.float32)
    pk = np.zeros((3 * LANES, D_MODEL), np.float32)
    cq = np.zeros((1, D_MODEL), np.float32)
    ck = np.zeros((1, D_MODEL), np.float32)
    for h in range(N_HEADS):
        for piece in range(3):
            pq[piece * LANES + h, h * HEAD_DIM + piece] = 1.0
            pk[piece * LANES + h, h * HEAD_DIM + 3 + piece] = -1.0
            cq[0, h * HEAD_DIM + 3 + piece] = 1.0
            ck[0, h * HEAD_DIM + piece] = 1.0
    return (jnp.asarray(pq, BF16), jnp.asarray(pk, BF16), jnp.asarray(cq), jnp.asarray(ck))


def _fox_aug_kernel(x_ref, pq_ref, pk_ref, cq_ref, ck_ref, q_ref, k_ref, e_ref,
                    aq_ref, ak_ref, cum_ref, qn_ref, kn_ref, carry_ref):
    @pl.when(pl.program_id(0) == 0)
    def _():
        carry_ref[...] = jnp.zeros_like(carry_ref)
        qn_ref[...] = jnp.zeros_like(qn_ref)
        kn_ref[...] = jnp.zeros_like(kn_ref)

    for a_ref, n_ref in ((q_ref, qn_ref), (k_ref, kn_ref)):
        a = a_ref[...].astype(F32)
        n2 = jnp.dot((a * a).astype(BF16), e_ref[...], preferred_element_type=F32)
        n_ref[...] = jnp.maximum(n_ref[...], jnp.max(n2, axis=0, keepdims=True))

    tb = x_ref.shape[0]
    row = lax.broadcasted_iota(jnp.int32, (tb, tb), 0)
    col = lax.broadcasted_iota(jnp.int32, (tb, tb), 1)
    tri = (col <= row).astype(BF16)
    x3 = _split3_bf16(x_ref[...])
    c3 = jnp.dot(tri, x3, preferred_element_type=F32)
    cum = (c3[:, 2 * LANES:] + c3[:, LANES:2 * LANES]) + c3[:, :LANES] + carry_ref[...]
    carry_ref[...] = cum[tb - 1:tb, :]
    cum_ref[...] = cum
    cum3 = _split3_bf16(cum)
    aq = jnp.dot(cum3, pq_ref[...], preferred_element_type=F32) + cq_ref[...]
    ak = jnp.dot(cum3, pk_ref[...], preferred_element_type=F32) + ck_ref[...]
    aq_ref[...] = aq.astype(BF16)
    ak_ref[...] = ak.astype(BF16)


def _fox_aug(logf_pad, q, k):
    t_len = logf_pad.shape[0]
    tb = 2 * LANES
    pq, pk, cq, ck = _fox_aug_placement()
    head_of_col = np.arange(D_MODEL)[:, None] // HEAD_DIM == np.arange(LANES)[None, :]
    const = lambda shape: pl.BlockSpec(shape, lambda i: (0, 0))
    rows = lambda width: pl.BlockSpec((tb, width), lambda i: (i, 0))
    return pl.pallas_call(
        _fox_aug_kernel,
        grid=(t_len // tb,),
        in_specs=[
            rows(LANES),
            const((3 * LANES, D_MODEL)), const((3 * LANES, D_MODEL)),
            const((1, D_MODEL)), const((1, D_MODEL)),
            rows(D_MODEL), rows(D_MODEL), const((D_MODEL, LANES)),
        ],
        out_specs=[rows(D_MODEL), rows(D_MODEL), rows(LANES), const((1, LANES)), const((1, LANES))],
        out_shape=[jax.ShapeDtypeStruct((t_len, D_MODEL), BF16)] * 2
        + [jax.ShapeDtypeStruct((t_len, LANES), F32)]
        + [jax.ShapeDtypeStruct((1, LANES), F32)] * 2,
        scratch_shapes=[pltpu.VMEM((1, LANES), F32)],
        compiler_params=_cparams(("arbitrary",)),
        name="fox_aug",
    )(logf_pad, pq, pk, cq, ck, q, k, jnp.asarray(head_of_col, BF16))


def _fox_prompt_kernel(start_ref, q_ref, aq_ref, k_ref, ak_ref, v_ref, o_ref,
                       s0_ref, s1_ref, p0_ref, p1_ref, a0_ref, a1_ref, m_ref, acc_ref):
    qi = pl.program_id(1)
    j0 = start_ref[pl.program_id(0), qi]
    tq, tk = TQ_FOX, TK_FOX
    dn = (((1,), (1,)), ((), ()))
    q = jnp.concatenate([q_ref[...], aq_ref[...]], axis=1)
    ones_col = (lax.broadcasted_iota(jnp.int32, (tk, HEAD_DIM), 1) == 0).astype(BF16)

    everything, lower = slice(0, tq), slice(tk, tq)

    def stat(ref, rows):
        return ref[...].reshape(tq, 1) if rows is everything else ref[1]

    def set_stat(ref, rows, value):
        if rows is everything:
            ref[...] = value.reshape(2, tk, 1)
        else:
            ref[1] = value

    def scores(ki, s_ref, rows=everything):
        ks = pl.multiple_of(ki * tk, tk)
        k = jnp.concatenate([k_ref[pl.ds(ks, tk), :], ak_ref[pl.ds(ks, tk), :]], axis=1)
        s_ref[rows, :] = lax.dot_general(q[rows], k, dn, preferred_element_type=F32)

    def accumulate(ki, p_ref, a_ref, rows=everything):
        ks = pl.multiple_of(ki * tk, tk)
        v = jnp.concatenate([v_ref[pl.ds(ks, tk), :], ones_col], axis=1)
        pv = jnp.dot(p_ref[rows, :], v, preferred_element_type=F32)
        acc_ref[rows, :] = stat(a_ref, rows) * acc_ref[rows, :] + pv

    def softmax(s, p_ref, a_ref, rows=everything):
        m_prev = stat(m_ref, rows)
        m_new = jnp.maximum(m_prev, jnp.max(s, axis=-1, keepdims=True))
        set_stat(a_ref, rows, jnp.exp(m_prev - m_new))
        p_ref[rows, :] = jnp.exp(s - m_new).astype(BF16)
        set_stat(m_ref, rows, m_new)

    m_ref[...] = jnp.full_like(m_ref, -jnp.inf)
    acc_ref[...] = jnp.zeros_like(acc_ref)
    p1_ref[...] = jnp.zeros_like(p1_ref)
    a1_ref[...] = jnp.ones_like(a1_ref)
    scores(2 * j0, s0_ref)

    def body(j, carry):
        scores(2 * j + 1, s1_ref)
        accumulate(jnp.maximum(2 * j - 1, 0), p1_ref, a1_ref)
        softmax(s0_ref[...], p0_ref, a0_ref)
        scores(2 * j + 2, s0_ref)
        accumulate(2 * j, p0_ref, a0_ref)
        softmax(s1_ref[...], p1_ref, a1_ref)
        return carry

    lax.fori_loop(j0, qi, body, 0)

    row = lax.broadcasted_iota(jnp.int32, (tq, tk), 0)
    col = lax.broadcasted_iota(jnp.int32, (tq, tk), 1)
    scores(2 * qi + 1, s1_ref, lower)
    accumulate(jnp.maximum(2 * qi - 1, 0), p1_ref, a1_ref)
    softmax(jnp.where(col <= row, s0_ref[...], -jnp.inf), p0_ref, a0_ref)
    accumulate(2 * qi, p0_ref, a0_ref)
    row_l = lax.broadcasted_iota(jnp.int32, (tq - tk, tk), 0)
    col_l = lax.broadcasted_iota(jnp.int32, (tq - tk, tk), 1)
    softmax(jnp.where(col_l <= row_l, s1_ref[lower, :], -jnp.inf), p1_ref, a1_ref, lower)
    accumulate(2 * qi + 1, p1_ref, a1_ref, lower)
    acc = acc_ref[...]
    o_ref[...] = (acc[:, :HEAD_DIM] / acc[:, HEAD_DIM:HEAD_DIM + 1]).astype(BF16)


def _fox_first_live_pair(cum_pad, qn2, kn2):
    nq = cum_pad.shape[0] // TQ_FOX
    slack = 2.0 * 1.01 * jnp.sqrt(qn2[0, :N_HEADS] * kn2[0, :N_HEADS])
    cum = cum_pad[:, :N_HEADS]
    c_tile_start = cum[0::TQ_FOX]
    c_pair_end = cum[TQ_FOX - 1::TQ_FOX]
    bound = c_tile_start[:, None, :] - c_pair_end[None, :, :] + slack[None, None, :]
    before = np.arange(nq)[None, :, None] < np.arange(nq)[:, None, None]
    dead = jnp.logical_and(bound < -FOX_DEAD_LOGIT, before)
    lead = jnp.cumprod(dead.astype(jnp.int32), axis=1)
    return jnp.transpose(jnp.sum(lead, axis=1)).astype(jnp.int32)


def _fox_prompt_attn(q, aq, k, ak, v, first_pair):
    t_len = q.shape[0]
    tq, tk = TQ_FOX, TK_FOX
    assert tq == 2 * tk and t_len % tq == 0
    tile = pl.BlockSpec((tq, HEAD_DIM), lambda h, i, start: (i, h))
    whole = pl.BlockSpec((t_len, HEAD_DIM), lambda h, i, start: (0, h))
    return pl.pallas_call(
        _fox_prompt_kernel,
        grid_spec=pltpu.PrefetchScalarGridSpec(
            num_scalar_prefetch=1,
            grid=(N_HEADS, t_len // tq),
            in_specs=[tile, tile, whole, whole, whole],
            out_specs=tile,
            scratch_shapes=[
                pltpu.VMEM((tq, tk), F32), pltpu.VMEM((tq, tk), F32),
                pltpu.VMEM((tq, tk), BF16), pltpu.VMEM((tq, tk), BF16),
                pltpu.VMEM((2, tk, 1), F32), pltpu.VMEM((2, tk, 1), F32),
                pltpu.VMEM((2, tk, 1), F32),
                pltpu.VMEM((tq, 2 * HEAD_DIM), F32),
            ],
        ),
        out_shape=jax.ShapeDtypeStruct((t_len, D_MODEL), BF16),
        compiler_params=_cparams(("arbitrary", "arbitrary")),
        name="fox_prompt_attn",
    )(first_pair, q, aq, k, ak, v)


def _fox_sample_kernel(q_ref, kn_ref, vn_ref, kc_ref, vc_ref, cq_ref, ckc_ref, ckn_ref,
                       o_ref, s_ref, sn_ref, m_ref, acc_ref, *, n_new, tk):
    ki = pl.program_id(1)
    nk = pl.num_programs(1)
    dn = (((1,), (1,)), ((), ()))

    @pl.when(ki == 0)
    def _():
        m_ref[...] = jnp.full_like(m_ref, -jnp.inf)
        acc_ref[...] = jnp.zeros_like(acc_ref)

    def head_rows(h):
        return slice(h * n_new, (h + 1) * n_new)

    def head_cols(h):
        return slice(h * HEAD_DIM, (h + 1) * HEAD_DIM)

    def softmax_update(s):
        m_prev = m_ref[...]
        m_new = jnp.maximum(m_prev, jnp.max(s, axis=-1, keepdims=True))
        m_ref[...] = m_new
        return jnp.exp(m_prev - m_new), jnp.exp(s - m_new).astype(BF16)

    def accumulate(alpha, p, values):
        n_keys = p.shape[1]
        ones_col = (lax.broadcasted_iota(jnp.int32, (n_keys, HEAD_DIM), 1) == 0).astype(BF16)
        for h in range(N_HEADS):
            r = head_rows(h)
            pv = jnp.dot(p[r], jnp.concatenate([values(h), ones_col], axis=1),
                         preferred_element_type=F32)
            acc_ref[r, :] = alpha[r] * acc_ref[r, :] + pv

    for h in range(N_HEADS):
        kc = kc_ref[0, pl.ds(h, tk, stride=N_HEADS), :].astype(BF16)
        s = lax.dot_general(q_ref[:, head_cols(h)], kc, dn, preferred_element_type=F32)
        s_ref[head_rows(h), :] = (s + cq_ref[0, :, h:h + 1]) - ckc_ref[0, h:h + 1, :]
    alpha, p = softmax_update(s_ref[...])
    accumulate(alpha, p, lambda h: vc_ref[0, pl.ds(h, tk, stride=N_HEADS), :].astype(BF16))

    @pl.when(ki == nk - 1)
    def _():
        row = lax.broadcasted_iota(jnp.int32, (n_new, n_new), 0)
        col = lax.broadcasted_iota(jnp.int32, (n_new, n_new), 1)
        for h in range(N_HEADS):
            s = lax.dot_general(q_ref[:, head_cols(h)], kn_ref[:, head_cols(h)], dn,
                                preferred_element_type=F32)
            s = (s + cq_ref[0, :, h:h + 1]) - ckn_ref[0, h:h + 1, 0:n_new]
            sn_ref[head_rows(h), :] = jnp.where(col <= row, s, -jnp.inf)
        alpha, p = softmax_update(sn_ref[...])
        accumulate(alpha, p, lambda h: vn_ref[:, head_cols(h)])
        for h in range(N_HEADS):
            acc = acc_ref[head_rows(h), :]
            o_ref[:, head_cols(h)] = (acc[:, :HEAD_DIM] / acc[:, HEAD_DIM:HEAD_DIM + 1]).astype(BF16)


def _fox_sample_attn(q, kn, vn, cache_k, cache_v, layer, cq_bth, cum_bht, n_streams, n_new):
    past = cache_k.shape[1] // N_HEADS
    tk = TK_FOX_S
    first = layer * n_streams
    rows = N_HEADS * n_new
    return pl.pallas_call(
        functools.partial(_fox_sample_kernel, n_new=n_new, tk=tk),
        grid=(n_streams, past // tk),
        in_specs=[
            pl.BlockSpec((n_new, D_MODEL), lambda b, i: (b, 0)),
            pl.BlockSpec((n_new, D_MODEL), lambda b, i: (b, 0)),
            pl.BlockSpec((n_new, D_MODEL), lambda b, i: (b, 0)),
            pl.BlockSpec((1, tk * N_HEADS, HEAD_DIM), lambda b, i: (first + b, i, 0)),
            pl.BlockSpec((1, tk * N_HEADS, HEAD_DIM), lambda b, i: (first + b, i, 0)),
            pl.BlockSpec((1, n_new, N_HEADS), lambda b, i: (b, 0, 0)),
            pl.BlockSpec((1, N_HEADS, tk), lambda b, i: (b, 0, i)),
            pl.BlockSpec((1, N_HEADS, LANES), lambda b, i: (b, 0, past // LANES)),
        ],
        out_specs=pl.BlockSpec((n_new, D_MODEL), lambda b, i: (b, 0)),
        out_shape=jax.ShapeDtypeStruct((n_streams * n_new, D_MODEL), BF16),
        scratch_shapes=[
            pltpu.VMEM((rows, tk), F32),
            pltpu.VMEM((rows, n_new), F32),
            pltpu.VMEM((rows, 1), F32),
            pltpu.VMEM((rows, 2 * HEAD_DIM), F32),
        ],
        compiler_params=_cparams(("arbitrary", "arbitrary")),
        name="fox_sample_attn",
    )(q, kn, vn, cache_k, cache_v, cq_bth, cum_bht, cum_bht)


def _band_profile(rel_table, d0, n_rows, width):
    dist = d0 + (n_rows - 1) - np.arange(width)
    idx = np.clip(dist, -MAX_REL, MAX_REL) + MAX_REL
    return rel_table.astype(F32)[:, idx].reshape(N_HEADS, 1, width)


def _toeplitz_rows(g, n_rows, n_cols):
    width = g.shape[-1]
    assert n_cols + n_rows - 1 <= width and width % LANES == 0
    r = pltpu.roll(jnp.broadcast_to(g, (n_rows, width)), width - (n_rows - 1), axis=1,
                   stride=1, stride_axis=0)
    return r[:, :n_cols]


def _softmax_rows(s):
    m = jnp.max(s, axis=-1, keepdims=True)
    p = jnp.exp(s - m)
    return p, jnp.sum(p, axis=-1, keepdims=True)


def _band_prompt_kernel(q_ref, kp_ref, kc_ref, vp_ref, vc_ref, g_ref, o_ref, b_ref):
    qi = pl.program_id(1)
    t = TQ_BAND

    @pl.when(qi == 0)
    def _():
        qc = lax.broadcasted_iota(jnp.int32, (t, 3 * t), 0) // CHUNK
        kc = lax.broadcasted_iota(jnp.int32, (t, 3 * t), 1) // CHUNK
        ok = (kc >= qc) & (kc <= qc + BAND_CHUNKS)
        b_ref[...] = jnp.where(ok, _toeplitz_rows(g_ref[0], t, 3 * t), -jnp.inf)

    k = jnp.concatenate([kp_ref[...], kc_ref[...]], axis=0)
    v = jnp.concatenate([vp_ref[...], vc_ref[...]], axis=0)
    col = lax.broadcasted_iota(jnp.int32, (t, 3 * t), 1)
    for sub in range(TB_BAND // t):
        lo = sub * t
        q = q_ref[lo:lo + t, :]
        s = lax.dot_general(q, k[lo:lo + 3 * t], (((1,), (1,)), ((), ())),
                            preferred_element_type=F32) + b_ref[...]
        if lo < BAND_PAST:
            first_valid = jnp.where(qi == 0, BAND_PAST - lo, 0)
            s = jnp.where(col >= first_valid, s, -jnp.inf)
        p, l = _softmax_rows(s)
        o = jnp.dot(p.astype(BF16), v[lo:lo + 3 * t], preferred_element_type=F32)
        o_ref[lo:lo + t, :] = (o / l).astype(BF16)


def _band_prompt_attn(q, k, v, profile):
    t_len = q.shape[0]
    t = TQ_BAND
    tb = TB_BAND
    per = tb // BAND_PAST
    assert BAND_PAST == 2 * t and t % CHUNK == 0 and tb % BAND_PAST == 0 and t_len % tb == 0
    cur = pl.BlockSpec((tb, HEAD_DIM), lambda h, i: (i, h))
    prev = pl.BlockSpec((BAND_PAST, HEAD_DIM), lambda h, i: (jnp.maximum(i * per - 1, 0), h))
    return pl.pallas_call(
        _band_prompt_kernel,
        grid=(N_HEADS, t_len // tb),
        in_specs=[cur, prev, cur, prev, cur,
                  pl.BlockSpec((1, 1, 4 * t), lambda h, i: (h, 0, 0))],
        out_specs=cur,
        out_shape=jax.ShapeDtypeStruct((t_len, D_MODEL), BF16),
        scratch_shapes=[pltpu.VMEM((t, 3 * t), F32)],
        compiler_params=_cparams(("arbitrary", "arbitrary")),
        name="band_prompt_attn",
    )(q, k, k, v, v, profile)


def _band_sample_kernel(q_ref, kn_ref, vn_ref, kc_ref, vc_ref, g_ref, mask_ref, o_ref, b_ref, *,
                        n_cache):
    n_new = q_ref.shape[0]

    @pl.when(pl.program_id(0) == 0)
    def _():
        for h in range(N_HEADS):
            b_ref[h] = _toeplitz_rows(g_ref[h], n_new, n_cache + n_new) + mask_ref[...]

    for h in range(N_HEADS):
        hs = slice(h * HEAD_DIM, (h + 1) * HEAD_DIM)
        q = q_ref[:, hs]
        kc = kc_ref[0, pl.ds(h, n_cache, stride=N_HEADS), :].astype(BF16)
        vc = vc_ref[0, pl.ds(h, n_cache, stride=N_HEADS), :].astype(BF16)
        dn = (((1,), (1,)), ((), ()))
        s1 = lax.dot_general(q, kc, dn, preferred_element_type=F32) + b_ref[h, :, 0:n_cache]
        s2 = lax.dot_general(q, kn_ref[:, hs], dn, preferred_element_type=F32) + b_ref[h, :, n_cache:]
        m = jnp.maximum(jnp.max(s1, axis=-1, keepdims=True), jnp.max(s2, axis=-1, keepdims=True))
        p1 = jnp.exp(s1 - m)
        p2 = jnp.exp(s2 - m)
        l = jnp.sum(p1, axis=-1, keepdims=True) + jnp.sum(p2, axis=-1, keepdims=True)
        o = jnp.dot(p1.astype(BF16), vc, preferred_element_type=F32)
        o = o + jnp.dot(p2.astype(BF16), vn_ref[:, hs], preferred_element_type=F32)
        o_ref[:, hs] = (o / l).astype(BF16)


def _band_sample_attn(q, kn, vn, cache_k, cache_v, layer, rel_table, past, n_streams, n_new):
    n_cache = cache_k.shape[1] // N_HEADS
    n_keys = n_cache + n_new
    first = layer * n_streams
    q_pos = past + np.arange(n_new)
    k_pos = past - n_cache + np.arange(n_keys)
    qc, kc = q_pos[:, None] // CHUNK, k_pos[None, :] // CHUNK
    ok = (k_pos[None, :] >= 0) & (kc <= qc) & (kc >= qc - BAND_CHUNKS)
    mask = jnp.asarray(np.where(ok, 0.0, -np.inf), F32)
    width = -(-(n_keys + n_new - 1) // LANES) * LANES
    profile = _band_profile(rel_table, n_cache, n_new, width)
    return pl.pallas_call(
        functools.partial(_band_sample_kernel, n_cache=n_cache),
        grid=(n_streams,),
        in_specs=[
            pl.BlockSpec((n_new, D_MODEL), lambda b: (b, 0)),
            pl.BlockSpec((n_new, D_MODEL), lambda b: (b, 0)),
            pl.BlockSpec((n_new, D_MODEL), lambda b: (b, 0)),
            pl.BlockSpec((1, n_cache * N_HEADS, HEAD_DIM), lambda b: (first + b, 0, 0)),
            pl.BlockSpec((1, n_cache * N_HEADS, HEAD_DIM), lambda b: (first + b, 0, 0)),
            pl.BlockSpec((N_HEADS, 1, width), lambda b: (0, 0, 0)),
            pl.BlockSpec((n_new, n_keys), lambda b: (0, 0)),
        ],
        out_specs=pl.BlockSpec((n_new, D_MODEL), lambda b: (b, 0)),
        out_shape=jax.ShapeDtypeStruct((n_streams * n_new, D_MODEL), BF16),
        scratch_shapes=[pltpu.VMEM((N_HEADS, n_new, n_keys), F32)],
        compiler_params=_cparams(("arbitrary",)),
        name="band_sample_attn",
    )(q, kn, vn, cache_k, cache_v, profile, mask)


def _oproj_kernel(x_ref, a_ref, w_ref, o_ref):
    o_ref[...] = x_ref[...] + jnp.dot(a_ref[...], w_ref[...], preferred_element_type=F32)


def _out_proj(x, a, w_o_b):
    m = x.shape[0]
    tm = min(TM, m)
    return pl.pallas_call(
        _oproj_kernel,
        grid=(m // tm,),
        in_specs=[
            pl.BlockSpec((tm, D_MODEL), lambda i: (i, 0)),
            pl.BlockSpec((tm, D_MODEL), lambda i: (i, 0)),
            pl.BlockSpec((D_MODEL, D_MODEL), lambda i: (0, 0)),
        ],
        out_specs=pl.BlockSpec((tm, D_MODEL), lambda i: (i, 0)),
        out_shape=jax.ShapeDtypeStruct((m, D_MODEL), F32),
        compiler_params=_cparams(("arbitrary",)),
        name="out_proj",
    )(x, a, w_o_b)


def _ffn_kernel(*refs, stream_len, tm, n_f, final_norm):
    if final_norm:
        (x_ref, g_ref, wg_ref, wv_ref, hg_ref, hv_ref, cwg_ref, cwv_ref, cbg_ref, cbv_ref,
         wo_ref, gf_ref, o_ref, tg_ref, tv_ref, h_ref, pg_ref, pv_ref, u_ref) = refs
    else:
        (x_ref, g_ref, wg_ref, wv_ref, hg_ref, hv_ref, cwg_ref, cwv_ref, cbg_ref, cbv_ref,
         wo_ref, o_ref, tg_ref, tv_ref, h_ref, pg_ref, pv_ref, u_ref) = refs
        gf_ref = None
    i = pl.program_id(0)
    f = pl.program_id(1)
    streams_per_tile = max(tm // stream_len, 1)
    tiles_per_stream = max(stream_len // tm, 1)

    @pl.when(f == 0)
    def _():
        h_ref[...] = _rmsnorm_f32(x_ref[...], g_ref[...]).astype(BF16)
        o_ref[...] = x_ref[...]

    n_groups = FFN_ROW_GROUPS
    rg = tm // n_groups
    assert rg % SUBLANES == 0 and (streams_per_tile == 1 or streams_per_tile % n_groups == 0)
    row = lax.broadcasted_iota(jnp.int32, (rg, TF), 0)
    if streams_per_tile > 1:
        row = row % stream_len
    halves = ((wg_ref, hg_ref, cwg_ref, cbg_ref, pg_ref, tg_ref),
              (wv_ref, hv_ref, cwv_ref, cbv_ref, pv_ref, tv_ref))

    class _Up:
        def __init__(self, g, half):
            self.ref = u_ref.at[half * n_groups + g]
            self.ref[...] = jnp.dot(h_ref[g * rg:(g + 1) * rg, :], halves[half][0][...],
                                    preferred_element_type=F32)

        def __getitem__(self, idx):
            return self.ref[idx]

        def reshape(self, *shape):
            return self.ref[...].reshape(*shape)

    if streams_per_tile == 1:
        @pl.when((i % tiles_per_stream) == 0)
        def _():
            for (_, hist_ref, _, _, prev_ref, _) in halves:
                prev_ref[f, SUBLANES - (CONV_W - 1):SUBLANES, :] = hist_ref[0]

    ups = []

    def conv(g, half):
        _, hist_ref, cw_ref, cb_ref, prev_ref, _ = halves[half]
        u = ups[g][half][...]
        if streams_per_tile > 1:
            spg = streams_per_tile // n_groups
            def per_row(r):
                hr = hist_ref[g * spg:(g + 1) * spg, r:r + 1, :]
                return jnp.broadcast_to(hr, (spg, stream_len, TF)).reshape(rg, TF)
            p0, p1 = per_row(0), per_row(1)
        elif g == 0:
            p0 = prev_ref[f, SUBLANES - 2:SUBLANES - 1, :]
            p1 = prev_ref[f, SUBLANES - 1:SUBLANES, :]
        else:
            p0 = ups[g - 1][half][rg - 2:rg - 1, :]
            p1 = ups[g - 1][half][rg - 1:rg, :]
        u1 = jnp.where(row == 0, p1, pltpu.roll(u, 1, axis=0))
        u2 = jnp.where(row == 0, p0, jnp.where(row == 1, p1, pltpu.roll(u, 2, axis=0)))
        c = cb_ref[...] + cw_ref[0:1, :] * u2
        c = c + cw_ref[1:2, :] * u1
        return c + cw_ref[2:3, :] * u

    ups.append([_Up(0, half) for half in range(2)])
    for g in range(n_groups):
        cg = conv(g, 0)
        cv = conv(g, 1)
        a = ((cg * jax.nn.sigmoid(cg)) * cv).astype(BF16)
        if g + 1 < n_groups:
            ups.append([_Up(g + 1, half) for half in range(2)])
        o_ref[g * rg:(g + 1) * rg, :] += jnp.dot(a, wo_ref[...], preferred_element_type=F32)

    for half, (_, _, _, _, prev_ref, tail_ref) in enumerate(halves):
        if streams_per_tile > 1:
            spg = streams_per_tile // n_groups
            for g in range(n_groups):
                u3 = ups[g][half].reshape(spg, stream_len, TF)
                tail_ref[g * spg:(g + 1) * spg] = u3[:, stream_len - SUBLANES:, :]
        else:
            last = ups[n_groups - 1][half][rg - SUBLANES:, :]
            prev_ref[f] = last
            tail_ref[0] = last

    if final_norm:
        @pl.when(f == n_f - 1)
        def _():
            o_ref[...] = _rmsnorm_f32(o_ref[...], gf_ref[...])


def _conv_ffn(x, g, w_in_g, w_in_v, hist_g, hist_v, cw_g, cw_v, cb_g, cb_v, w_out_b, stream_len,
              g_final=None):
    m = x.shape[0]
    tm = min(TM_FFN, m)
    nf = D_FF_PAD // TF
    n_tiles = m // tm
    final_norm = g_final is not None
    if stream_len >= tm:
        assert stream_len % tm == 0
        tiles_per_stream = stream_len // tm
        hist_spec = pl.BlockSpec((1, CONV_W - 1, TF), lambda i, f: (i // tiles_per_stream, 0, f))
        tail_rows = n_tiles
        tail_spec = pl.BlockSpec((1, SUBLANES, TF), lambda i, f: (i, 0, f))
    else:
        assert tm % stream_len == 0 and stream_len % SUBLANES == 0 and n_tiles == 1
        spt = tm // stream_len
        hist_spec = pl.BlockSpec((spt, CONV_W - 1, TF), lambda i, f: (i, 0, f))
        tail_rows = m // stream_len
        tail_spec = pl.BlockSpec((spt, SUBLANES, TF), lambda i, f: (i, 0, f))
    row_tile = pl.BlockSpec((tm, D_MODEL), lambda i, f: (i, 0))
    gain = pl.BlockSpec((1, D_MODEL), lambda i, f: (0, 0))
    w_in_spec = pl.BlockSpec((D_MODEL, TF), lambda i, f: (0, f))
    cw_spec = pl.BlockSpec((CONV_W, TF), lambda i, f: (0, f))
    cb_spec = pl.BlockSpec((1, TF), lambda i, f: (0, f))
    in_specs = [row_tile, gain, w_in_spec, w_in_spec, hist_spec, hist_spec, cw_spec, cw_spec,
                cb_spec, cb_spec, pl.BlockSpec((TF, D_MODEL), lambda i, f: (f, 0))]
    args = [x, g.reshape(1, D_MODEL), w_in_g, w_in_v, hist_g, hist_v, cw_g, cw_v, cb_g, cb_v, w_out_b]
    if final_norm:
        in_specs.append(gain)
        args.append(g_final.reshape(1, D_MODEL))
    tail_shape = jax.ShapeDtypeStruct((tail_rows, SUBLANES, D_FF_PAD), F32)
    return pl.pallas_call(
        functools.partial(_ffn_kernel, stream_len=stream_len, tm=tm, n_f=nf, final_norm=final_norm),
        grid=(n_tiles, nf),
        in_specs=in_specs,
        out_specs=[row_tile, tail_spec, tail_spec],
        out_shape=[jax.ShapeDtypeStruct((m, D_MODEL), F32), tail_shape, tail_shape],
        scratch_shapes=[
            pltpu.VMEM((tm, D_MODEL), BF16),
            pltpu.VMEM((nf, SUBLANES, TF), F32),
            pltpu.VMEM((nf, SUBLANES, TF), F32),
            pltpu.VMEM((2 * FFN_ROW_GROUPS, tm // FFN_ROW_GROUPS, TF), F32),
        ],
        compiler_params=_cparams(("arbitrary", "arbitrary")),
        name="conv_ffn",
    )(*args)


def _pad_ff(a):
    pad = [(0, 0)] * (a.ndim - 1) + [(0, D_FF_PAD - D_FF)]
    return jnp.pad(a, pad)


def _split_ff(a):
    return _pad_ff(a[..., :D_FF]), _pad_ff(a[..., D_FF:])


def kernel(x_prompt, x_sample, cache_fox_k, cache_fox_v, cache_fox_logf, cache_band_k, cache_band_v,
           state_conv, g_attn, w_qkv, w_fgate, b_fgate, rel_bias, w_o, g_ffn, w_in, conv_w, conv_b,
           w_out, g_final):
    depth = g_attn.shape[0]
    bp, sp, _ = x_prompt.shape
    bs, ss, _ = x_sample.shape
    past = cache_fox_k.shape[2]
    n_band_cache = cache_band_k.shape[2]
    n_keep = min(BAND_PAST, sp)
    assert bp == 1

    xp = x_prompt.reshape(bp * sp, D_MODEL)
    xs = x_sample.reshape(bs * ss, D_MODEL)

    n_fox = cache_fox_k.shape[0]
    n_band = cache_band_k.shape[0]
    cfk = cache_fox_k.reshape(n_fox * bs, past * N_HEADS, HEAD_DIM)
    cfv = cache_fox_v.reshape(n_fox * bs, past * N_HEADS, HEAD_DIM)
    cbk = cache_band_k.reshape(n_band * bs, n_band_cache * N_HEADS, HEAD_DIM)
    cbv = cache_band_v.reshape(n_band * bs, n_band_cache * N_HEADS, HEAD_DIM)

    hist0 = jnp.zeros((bp, CONV_W - 1, D_FF_PAD), F32)

    fox_lp, fox_ks, fox_vs, fox_ls = [], [], [], []
    band_kp, band_vp, band_ks, band_vs = [], [], [], []
    conv_p, conv_s = [], []
    fox_kv_p = None

    for i in range(depth):
        j = i // N_MIXERS
        is_fox = (i % N_MIXERS == 0)
        wqkv_b = w_qkv[i].astype(BF16)
        wo_b = w_o[i].astype(BF16)
        w_in_g, w_in_v = _split_ff(w_in[i].astype(BF16))
        cw_g, cw_v = _split_ff(conv_w[i])
        cb_g, cb_v = _split_ff(conv_b[i].reshape(1, 2 * D_FF))
        w_out_b = jnp.pad(w_out[i].astype(BF16), ((0, D_FF_PAD - D_FF), (0, 0)))
        hs_g, hs_v = _split_ff(state_conv[i])

        if is_fox:
            wf_b = jnp.pad(w_fgate[j].astype(BF16), ((0, 0), (0, LANES - N_HEADS)))
            bf = jnp.pad(b_fgate[j].reshape(1, N_HEADS), ((0, 0), (0, LANES - N_HEADS)))
            qp, kst, vst, kpb, vpb, lfp_pad = _qkv_proj(xp, g_attn[i], wqkv_b, wf_b, bf,
                                                        slot=j, n_slots=n_fox, kv_stack=fox_kv_p)
            fox_kv_p = (kst, vst)
            qs, ks32, vs32, ksb, vsb, lfs = _qkv_proj(xs, g_attn[i], wqkv_b, wf_b, bf)
            lfp = lfp_pad[:, :N_HEADS]
            lfs = lfs[:, :N_HEADS]

            aq, ak, cum_pad, qn2, kn2 = _fox_aug(lfp_pad, qp, kpb)
            op = _fox_prompt_attn(qp, aq, kpb, ak, vpb, _fox_first_live_pair(cum_pad, qn2, kn2))

            lf_all = jnp.concatenate([
                jnp.transpose(cache_fox_logf[j], (0, 2, 1)),
                jnp.transpose(lfs.reshape(bs, ss, N_HEADS), (0, 2, 1)),
                jnp.zeros((bs, N_HEADS, LANES - ss), F32),
            ], axis=-1).reshape(bs * N_HEADS, past + LANES)
            cum_s = _cumsum_lanes(lf_all).reshape(bs, N_HEADS, past + LANES)
            cq_s = jnp.transpose(cum_s[:, :, past:past + ss], (0, 2, 1))
            osm = _fox_sample_attn(qs, ksb, vsb, cfk, cfv, j, cq_s, cum_s, bs, ss)

            fox_lp.append(lfp.reshape(bp, sp, N_HEADS))
            fox_ks.append(ks32.reshape(bs, ss, N_HEADS, HEAD_DIM))
            fox_vs.append(vs32.reshape(bs, ss, N_HEADS, HEAD_DIM))
            fox_ls.append(lfs.reshape(bs, ss, N_HEADS))
        else:
            qp, kp32, vp32, kpb, vpb = _qkv_proj(xp, g_attn[i], wqkv_b)
            qs, ks32, vs32, ksb, vsb = _qkv_proj(xs, g_attn[i], wqkv_b)
            op = _band_prompt_attn(qp, kpb, vpb,
                                   _band_profile(rel_bias[j], BAND_PAST, TQ_BAND, 4 * TQ_BAND))
            osm = _band_sample_attn(qs, ksb, vsb, cbk, cbv, j, rel_bias[j], past, bs, ss)
            last_rows = slice(sp - n_keep, sp)
            band_kp.append(kp32[0, last_rows].reshape(bp, n_keep, N_HEADS, HEAD_DIM))
            band_vp.append(vp32[0, last_rows].reshape(bp, n_keep, N_HEADS, HEAD_DIM))
            band_ks.append(ks32.reshape(bs, ss, N_HEADS, HEAD_DIM))
            band_vs.append(vs32.reshape(bs, ss, N_HEADS, HEAD_DIM))

        xp = _out_proj(xp, op, wo_b)
        xs = _out_proj(xs, osm, wo_b)

        gf = g_final if i == depth - 1 else None
        xp, tgp, tvp = _conv_ffn(xp, g_ffn[i], w_in_g, w_in_v, hist0, hist0, cw_g, cw_v, cb_g, cb_v,
                                 w_out_b, sp, gf)
        xs, tgs, tvs = _conv_ffn(xs, g_ffn[i], w_in_g, w_in_v, hs_g, hs_v, cw_g, cw_v, cb_g, cb_v,
                                 w_out_b, ss, gf)
        keep = slice(SUBLANES - (CONV_W - 1), SUBLANES)
        conv_p.append(jnp.concatenate([tgp[-1:, keep, :D_FF], tvp[-1:, keep, :D_FF]], axis=-1))
        conv_s.append(jnp.concatenate([tgs[:, keep, :D_FF], tvs[:, keep, :D_FF]], axis=-1))

    y_prompt = xp.reshape(bp, sp, D_MODEL)
    y_sample = xs.reshape(bs, ss, D_MODEL)
    fox_kp, fox_vp = (a.reshape(n_fox, bp, sp, N_HEADS, HEAD_DIM) for a in fox_kv_p)
    return (y_prompt, y_sample,
            fox_kp, fox_vp, jnp.stack(fox_lp),
            jnp.stack(fox_ks), jnp.stack(fox_vs), jnp.stack(fox_ls),
            jnp.stack(band_kp), jnp.stack(band_vp), jnp.stack(band_ks), jnp.stack(band_vs),
            jnp.stack(conv_p), jnp.stack(conv_s))
```

```python
import functools
import math

import numpy as np
import jax
import jax.numpy as jnp
from jax import lax
from jax.experimental import pallas as pl
from jax.experimental.pallas import tpu as pltpu

F32 = jnp.float32
BF16 = jnp.bfloat16

D_MODEL = 2048
N_HEADS = 16
HEAD_DIM = D_MODEL // N_HEADS
SCALE = 1.0 / math.sqrt(HEAD_DIM)
FOX_DEAD_LOGIT = 128.0
CHUNK = 64
BAND_CHUNKS = 8
BAND_PAST = BAND_CHUNKS * CHUNK
MAX_REL = 2 * CHUNK
D_FF = 5504
CONV_W = 3
EPS = 1e-6
N_MIXERS = 2

LANES = 128
SUBLANES = 8
VMEM_LIMIT = 56 * 1024 * 1024

TM = 512
TM_QKV = 256
TM_FFN = 512
TN_QKV = 512
TF = 512
FFN_ROW_GROUPS = 2
D_FF_PAD = ((D_FF + TF - 1) // TF) * TF
TK_FOX = 512
TQ_FOX = 2 * TK_FOX
TK_FOX_S = 512
TQ_BAND = 256
TB_BAND = 2048
CUM_BLK = 128


def _cparams(sem):
    return pltpu.CompilerParams(dimension_semantics=sem, vmem_limit_bytes=VMEM_LIMIT)


def _rmsnorm_f32(x, g):
    ms = jnp.mean(x * x, axis=-1, keepdims=True)
    return (x * lax.rsqrt(ms + EPS)) * g


def _qkv_kernel(*refs, with_fgate, n_aliased, q_scale):
    refs = refs[:3 + 2 * with_fgate] + refs[3 + 2 * with_fgate + n_aliased:]
    if with_fgate:
        (x_ref, g_ref, w_ref, wf_ref, bf_ref,
         q_ref, k32_ref, v32_ref, kb_ref, vb_ref, lf_ref) = refs
    else:
        x_ref, g_ref, w_ref, q_ref, k32_ref, v32_ref, kb_ref, vb_ref = refs

    h = _rmsnorm_f32(x_ref[...], g_ref[...]).astype(BF16)
    if with_fgate:
        z = jnp.dot(h, wf_ref[...], preferred_element_type=F32) + bf_ref[...]
        lf_ref[...] = jnp.minimum(z, 0.0) - jnp.log1p(jnp.exp(-jnp.abs(z)))

    def project(part, c):
        lo = part * D_MODEL + c * TN_QKV
        return jnp.dot(h, w_ref[:, lo:lo + TN_QKV], preferred_element_type=F32)

    for c in range(D_MODEL // TN_QKV):
        cs = slice(c * TN_QKV, (c + 1) * TN_QKV)
        q_ref[:, cs] = (project(0, c) * q_scale).astype(BF16)
        k = project(1, c)
        k32_ref[:, cs] = k
        kb_ref[:, cs] = k.astype(BF16)
        v = project(2, c)
        v32_ref[:, cs] = v
        vb_ref[:, cs] = v.astype(BF16)


def _qkv_proj(x, g, w_qkv_b, wf_b=None, bf=None, slot=0, n_slots=1, kv_stack=None, q_scale=SCALE):
    m = x.shape[0]
    tm = min(TM_QKV, m)
    with_fgate = wf_b is not None
    stack_tile = pl.BlockSpec((None, tm, D_MODEL), lambda i: (slot, i, 0))
    stack_shape = jax.ShapeDtypeStruct((n_slots, m, D_MODEL), F32)
    resident = lambda shape: pl.BlockSpec(shape, lambda i: (0, 0), pipeline_mode=pl.Buffered(1))
    in_specs = [
        pl.BlockSpec((tm, D_MODEL), lambda i: (i, 0)),
        resident((1, D_MODEL)),
        resident((D_MODEL, 3 * D_MODEL)),
    ]
    args = [x, g.reshape(1, D_MODEL), w_qkv_b]
    tile = pl.BlockSpec((tm, D_MODEL), lambda i: (i, 0))
    out_specs = [tile, stack_tile, stack_tile, tile, tile]
    out_shape = [
        jax.ShapeDtypeStruct((m, D_MODEL), BF16),
        stack_shape,
        stack_shape,
        jax.ShapeDtypeStruct((m, D_MODEL), BF16),
        jax.ShapeDtypeStruct((m, D_MODEL), BF16),
    ]
    if with_fgate:
        in_specs += [resident((D_MODEL, LANES)), resident((1, LANES))]
        args += [wf_b, bf]
        out_specs.append(pl.BlockSpec((tm, LANES), lambda i: (i, 0)))
        out_shape.append(jax.ShapeDtypeStruct((m, LANES), F32))
    aliases = {}
    if kv_stack is not None:
        aliases = {len(args): 1, len(args) + 1: 2}
        in_specs += [pl.BlockSpec(memory_space=pl.ANY)] * 2
        args += list(kv_stack)
    return pl.pallas_call(
        functools.partial(_qkv_kernel, with_fgate=with_fgate, n_aliased=len(aliases),
                          q_scale=q_scale),
        grid=(m // tm,),
        in_specs=in_specs,
        out_specs=out_specs,
        out_shape=out_shape,
        input_output_aliases=aliases,
        compiler_params=_cparams(("arbitrary",)),
        name="qkv_proj",
    )(*args)


def _cumsum_kernel(x_ref, o_ref, carry_ref):
    @pl.when(pl.program_id(0) == 0)
    def _():
        carry_ref[...] = jnp.zeros_like(carry_ref)

    x = x_ref[...]
    hi = x.astype(BF16)
    r1 = x - hi.astype(F32)
    mid = r1.astype(BF16)
    lo = (r1 - mid.astype(F32)).astype(BF16)
    row = lax.broadcasted_iota(jnp.int32, (CUM_BLK, CUM_BLK), 0)
    col = lax.broadcasted_iota(jnp.int32, (CUM_BLK, CUM_BLK), 1)
    tri = (row <= col).astype(BF16)
    s = jnp.dot(lo, tri, preferred_element_type=F32)
    s = s + jnp.dot(mid, tri, preferred_element_type=F32)
    s = s + jnp.dot(hi, tri, preferred_element_type=F32)
    s = s + carry_ref[...]
    o_ref[...] = s
    carry_ref[...] = jnp.broadcast_to(s[:, CUM_BLK - 1:CUM_BLK], s.shape)


def _cumsum_lanes(x):
    r, t = x.shape
    return pl.pallas_call(
        _cumsum_kernel,
        grid=(t // CUM_BLK,),
        in_specs=[pl.BlockSpec((r, CUM_BLK), lambda i: (0, i))],
        out_specs=pl.BlockSpec((r, CUM_BLK), lambda i: (0, i)),
        out_shape=jax.ShapeDtypeStruct((r, t), F32),
        scratch_shapes=[pltpu.VMEM((r, CUM_BLK), F32)],
        compiler_params=_cparams(("arbitrary",)),
        name="cumsum_time",
    )(x)


def _split3_bf16(x):
    hi = x.astype(BF16)
    r1 = x - hi.astype(F32)
    mid = r1.astype(BF16)
    lo = (r1 - mid.astype(F32)).astype(BF16)
    return jnp.concatenate([hi, mid, lo], axis=-1)


def _fox_aug_placement():
    pq = np.zeros((3 * LANES, D_MODEL), np.float32)
    pk = np.zeros((3 * LANES, D_MODEL), np.float32)
    cq = np.zeros((1, D_MODEL), np.float32)
    ck = np.zeros((1, D_MODEL), np.float32)
    for h in range(N_HEADS):
        for piece in range(3):
            pq[piece * LANES + h, h * HEAD_DIM + piece] = 1.0
            pk[piece * LANES + h, h * HEAD_DIM + 3 + piece] = -1.0
            cq[0, h * HEAD_DIM + 3 + piece] = 1.0
            ck[0, h * HEAD_DIM + piece] = 1.0
    return (jnp.asarray(pq, BF16), jnp.asarray(pk, BF16), jnp.asarray(cq), jnp.asarray(ck))


def _fox_aug_kernel(x_ref, pq_ref, pk_ref, cq_ref, ck_ref, q_ref, k_ref, e_ref,
                    aq_ref, ak_ref, cum_ref, qn_ref, kn_ref, carry_ref):
    @pl.when(pl.program_id(0) == 0)
    def _():
        carry_ref[...] = jnp.zeros_like(carry_ref)
        qn_ref[...] = jnp.zeros_like(qn_ref)
        kn_ref[...] = jnp.zeros_like(kn_ref)

    for a_ref, n_ref in ((q_ref, qn_ref), (k_ref, kn_ref)):
        a = a_ref[...].astype(F32)
        n2 = jnp.dot((a * a).astype(BF16), e_ref[...], preferred_element_type=F32)
        n_ref[...] = jnp.maximum(n_ref[...], jnp.max(n2, axis=0, keepdims=True))

    tb = x_ref.shape[0]
    row = lax.broadcasted_iota(jnp.int32, (tb, tb), 0)
    col = lax.broadcasted_iota(jnp.int32, (tb, tb), 1)
    tri = (col <= row).astype(BF16)
    x3 = _split3_bf16(x_ref[...])
    c3 = jnp.dot(tri, x3, preferred_element_type=F32)
    cum = (c3[:, 2 * LANES:] + c3[:, LANES:2 * LANES]) + c3[:, :LANES] + carry_ref[...]
    carry_ref[...] = cum[tb - 1:tb, :]
    cum_ref[...] = cum
    cum3 = _split3_bf16(cum)
    aq = jnp.dot(cum3, pq_ref[...], preferred_element_type=F32) + cq_ref[...]
    ak = jnp.dot(cum3, pk_ref[...], preferred_element_type=F32) + ck_ref[...]
    aq_ref[...] = aq.astype(BF16)
    ak_ref[...] = ak.astype(BF16)


def _fox_aug(logf_pad, q, k):
    t_len = logf_pad.shape[0]
    tb = 2 * LANES
    pq, pk, cq, ck = _fox_aug_placement()
    head_of_col = np.arange(D_MODEL)[:, None] // HEAD_DIM == np.arange(LANES)[None, :]
    const = lambda shape: pl.BlockSpec(shape, lambda i: (0, 0))
    rows = lambda width: pl.BlockSpec((tb, width), lambda i: (i, 0))
    return pl.pallas_call(
        _fox_aug_kernel,
        grid=(t_len // tb,),
        in_specs=[
            rows(LANES),
            const((3 * LANES, D_MODEL)), const((3 * LANES, D_MODEL)),
            const((1, D_MODEL)), const((1, D_MODEL)),
            rows(D_MODEL), rows(D_MODEL), const((D_MODEL, LANES)),
        ],
        out_specs=[rows(D_MODEL), rows(D_MODEL), rows(LANES), const((1, LANES)), const((1, LANES))],
        out_shape=[jax.ShapeDtypeStruct((t_len, D_MODEL), BF16)] * 2
        + [jax.ShapeDtypeStruct((t_len, LANES), F32)]
        + [jax.ShapeDtypeStruct((1, LANES), F32)] * 2,
        scratch_shapes=[pltpu.VMEM((1, LANES), F32)],
        compiler_params=_cparams(("arbitrary",)),
        name="fox_aug",
    )(logf_pad, pq, pk, cq, ck, q, k, jnp.asarray(head_of_col, BF16))


def _fox_prompt_kernel(start_ref, q_ref, aq_ref, k_ref, ak_ref, v_ref, o_ref,
                       s0_ref, s1_ref, p0_ref, p1_ref, a0_ref, a1_ref, m_ref, acc_ref):
    qi = pl.program_id(1)
    j0 = start_ref[pl.program_id(0), qi]
    tq, tk = TQ_FOX, TK_FOX
    dn = (((1,), (1,)), ((), ()))
    q = jnp.concatenate([q_ref[...], aq_ref[...]], axis=1)
    ones_col = (lax.broadcasted_iota(jnp.int32, (tk, HEAD_DIM), 1) == 0).astype(BF16)

    everything, lower = slice(0, tq), slice(tk, tq)

    def stat(ref, rows):
        return ref[...].reshape(tq, 1) if rows is everything else ref[1]

    def set_stat(ref, rows, value):
        if rows is everything:
            ref[...] = value.reshape(2, tk, 1)
        else:
            ref[1] = value

    def scores(ki, s_ref, rows=everything):
        ks = pl.multiple_of(ki * tk, tk)
        k = jnp.concatenate([k_ref[pl.ds(ks, tk), :], ak_ref[pl.ds(ks, tk), :]], axis=1)
        s_ref[rows, :] = lax.dot_general(q[rows], k, dn, preferred_element_type=F32)

    def accumulate(ki, p_ref, a_ref, rows=everything):
        ks = pl.multiple_of(ki * tk, tk)
        v = jnp.concatenate([v_ref[pl.ds(ks, tk), :], ones_col], axis=1)
        pv = jnp.dot(p_ref[rows, :], v, preferred_element_type=F32)
        acc_ref[rows, :] = stat(a_ref, rows) * acc_ref[rows, :] + pv

    def softmax(s, p_ref, a_ref, rows=everything):
        m_prev = stat(m_ref, rows)
        m_new = jnp.maximum(m_prev, jnp.max(s, axis=-1, keepdims=True))
        set_stat(a_ref, rows, jnp.exp(m_prev - m_new))
        p_ref[rows, :] = jnp.exp(s - m_new).astype(BF16)
        set_stat(m_ref, rows, m_new)

    m_ref[...] = jnp.full_like(m_ref, -jnp.inf)
    acc_ref[...] = jnp.zeros_like(acc_ref)
    p1_ref[...] = jnp.zeros_like(p1_ref)
    a1_ref[...] = jnp.ones_like(a1_ref)
    scores(2 * j0, s0_ref)

    def body(j, carry):
        scores(2 * j + 1, s1_ref)
        accumulate(jnp.maximum(2 * j - 1, 0), p1_ref, a1_ref)
        softmax(s0_ref[...], p0_ref, a0_ref)
        scores(2 * j + 2, s0_ref)
        accumulate(2 * j, p0_ref, a0_ref)
        softmax(s1_ref[...], p1_ref, a1_ref)
        return carry

    lax.fori_loop(j0, qi, body, 0)

    row = lax.broadcasted_iota(jnp.int32, (tq, tk), 0)
    col = lax.broadcasted_iota(jnp.int32, (tq, tk), 1)
    scores(2 * qi + 1, s1_ref, lower)
    accumulate(jnp.maximum(2 * qi - 1, 0), p1_ref, a1_ref)
    softmax(jnp.where(col <= row, s0_ref[...], -jnp.inf), p0_ref, a0_ref)
    accumulate(2 * qi, p0_ref, a0_ref)
    row_l = lax.broadcasted_iota(jnp.int32, (tq - tk, tk), 0)
    col_l = lax.broadcasted_iota(jnp.int32, (tq - tk, tk), 1)
    softmax(jnp.where(col_l <= row_l, s1_ref[lower, :], -jnp.inf), p1_ref, a1_ref, lower)
    accumulate(2 * qi + 1, p1_ref, a1_ref, lower)
    acc = acc_ref[...]
    o_ref[...] = (acc[:, :HEAD_DIM] / acc[:, HEAD_DIM:HEAD_DIM + 1]).astype(BF16)


def _fox_first_live_pair(cum_pad, qn2, kn2):
    nq = cum_pad.shape[0] // TQ_FOX
    slack = 2.0 * 1.01 * jnp.sqrt(qn2[0, :N_HEADS] * kn2[0, :N_HEADS])
    cum = cum_pad[:, :N_HEADS]
    c_tile_start = cum[0::TQ_FOX]
    c_pair_end = cum[TQ_FOX - 1::TQ_FOX]
    bound = c_tile_start[:, None, :] - c_pair_end[None, :, :] + slack[None, None, :]
    before = np.arange(nq)[None, :, None] < np.arange(nq)[:, None, None]
    dead = jnp.logical_and(bound < -FOX_DEAD_LOGIT, before)
    lead = jnp.cumprod(dead.astype(jnp.int32), axis=1)
    return jnp.transpose(jnp.sum(lead, axis=1)).astype(jnp.int32)


def _fox_prompt_attn(q, aq, k, ak, v, first_pair):
    t_len = q.shape[0]
    tq, tk = TQ_FOX, TK_FOX
    assert tq == 2 * tk and t_len % tq == 0
    tile = pl.BlockSpec((tq, HEAD_DIM), lambda h, i, start: (i, h))
    whole = pl.BlockSpec((t_len, HEAD_DIM), lambda h, i, start: (0, h))
    return pl.pallas_call(
        _fox_prompt_kernel,
        grid_spec=pltpu.PrefetchScalarGridSpec(
            num_scalar_prefetch=1,
            grid=(N_HEADS, t_len // tq),
            in_specs=[tile, tile, whole, whole, whole],
            out_specs=tile,
            scratch_shapes=[
                pltpu.VMEM((tq, tk), F32), pltpu.VMEM((tq, tk), F32),
                pltpu.VMEM((tq, tk), BF16), pltpu.VMEM((tq, tk), BF16),
                pltpu.VMEM((2, tk, 1), F32), pltpu.VMEM((2, tk, 1), F32),
                pltpu.VMEM((2, tk, 1), F32),
                pltpu.VMEM((tq, 2 * HEAD_DIM), F32),
            ],
        ),
        out_shape=jax.ShapeDtypeStruct((t_len, D_MODEL), BF16),
        compiler_params=_cparams(("arbitrary", "arbitrary")),
        name="fox_prompt_attn",
    )(first_pair, q, aq, k, ak, v)


def _fox_sample_kernel(q_ref, kn_ref, vn_ref, kc_ref, vc_ref, cq_ref, ckc_ref, ckn_ref,
                       o_ref, s_ref, sn_ref, m_ref, acc_ref, *, n_new, tk):
    ki = pl.program_id(1)
    nk = pl.num_programs(1)
    dn = (((1,), (1,)), ((), ()))

    @pl.when(ki == 0)
    def _():
        m_ref[...] = jnp.full_like(m_ref, -jnp.inf)
        acc_ref[...] = jnp.zeros_like(acc_ref)

    def head_rows(h):
        return slice(h * n_new, (h + 1) * n_new)

    def head_cols(h):
        return slice(h * HEAD_DIM, (h + 1) * HEAD_DIM)

    def softmax_update(s):
        m_prev = m_ref[...]
        m_new = jnp.maximum(m_prev, jnp.max(s, axis=-1, keepdims=True))
        m_ref[...] = m_new
        return jnp.exp(m_prev - m_new), jnp.exp(s - m_new).astype(BF16)

    def accumulate(alpha, p, values):
        n_keys = p.shape[1]
        ones_col = (lax.broadcasted_iota(jnp.int32, (n_keys, HEAD_DIM), 1) == 0).astype(BF16)
        for h in range(N_HEADS):
            r = head_rows(h)
            pv = jnp.dot(p[r], jnp.concatenate([values(h), ones_col], axis=1),
                         preferred_element_type=F32)
            acc_ref[r, :] = alpha[r] * acc_ref[r, :] + pv

    for h in range(N_HEADS):
        kc = kc_ref[0, pl.ds(h, tk, stride=N_HEADS), :].astype(BF16)
        s = lax.dot_general(q_ref[:, head_cols(h)], kc, dn, preferred_element_type=F32)
        s_ref[head_rows(h), :] = (s + cq_ref[0, :, h:h + 1]) - ckc_ref[0, h:h + 1, :]
    alpha, p = softmax_update(s_ref[...])
    accumulate(alpha, p, lambda h: vc_ref[0, pl.ds(h, tk, stride=N_HEADS), :].astype(BF16))

    @pl.when(ki == nk - 1)
    def _():
        row = lax.broadcasted_iota(jnp.int32, (n_new, n_new), 0)
        col = lax.broadcasted_iota(jnp.int32, (n_new, n_new), 1)
        for h in range(N_HEADS):
            s = lax.dot_general(q_ref[:, head_cols(h)], kn_ref[:, head_cols(h)], dn,
                                preferred_element_type=F32)
            s = (s + cq_ref[0, :, h:h + 1]) - ckn_ref[0, h:h + 1, 0:n_new]
            sn_ref[head_rows(h), :] = jnp.where(col <= row, s, -jnp.inf)
        alpha, p = softmax_update(sn_ref[...])
        accumulate(alpha, p, lambda h: vn_ref[:, head_cols(h)])
        for h in range(N_HEADS):
            acc = acc_ref[head_rows(h), :]
            o_ref[:, head_cols(h)] = (acc[:, :HEAD_DIM] / acc[:, HEAD_DIM:HEAD_DIM + 1]).astype(BF16)


def _fox_sample_attn(q, kn, vn, cache_k, cache_v, layer, cq_bth, cum_bht, n_streams, n_new):
    past = cache_k.shape[1] // N_HEADS
    tk = TK_FOX_S
    first = layer * n_streams
    rows = N_HEADS * n_new
    return pl.pallas_call(
        functools.partial(_fox_sample_kernel, n_new=n_new, tk=tk),
        grid=(n_streams, past // tk),
        in_specs=[
            pl.BlockSpec((n_new, D_MODEL), lambda b, i: (b, 0)),
            pl.BlockSpec((n_new, D_MODEL), lambda b, i: (b, 0)),
            pl.BlockSpec((n_new, D_MODEL), lambda b, i: (b, 0)),
            pl.BlockSpec((1, tk * N_HEADS, HEAD_DIM), lambda b, i: (first + b, i, 0)),
            pl.BlockSpec((1, tk * N_HEADS, HEAD_DIM), lambda b, i: (first + b, i, 0)),
            pl.BlockSpec((1, n_new, N_HEADS), lambda b, i: (b, 0, 0)),
            pl.BlockSpec((1, N_HEADS, tk), lambda b, i: (b, 0, i)),
            pl.BlockSpec((1, N_HEADS, LANES), lambda b, i: (b, 0, past // LANES)),
        ],
        out_specs=pl.BlockSpec((n_new, D_MODEL), lambda b, i: (b, 0)),
        out_shape=jax.ShapeDtypeStruct((n_streams * n_new, D_MODEL), BF16),
        scratch_shapes=[
            pltpu.VMEM((rows, tk), F32),
            pltpu.VMEM((rows, n_new), F32),
            pltpu.VMEM((rows, 1), F32),
            pltpu.VMEM((rows, 2 * HEAD_DIM), F32),
        ],
        compiler_params=_cparams(("arbitrary", "arbitrary")),
        name="fox_sample_attn",
    )(q, kn, vn, cache_k, cache_v, cq_bth, cum_bht, cum_bht)


def _band_profile(rel_table, d0, n_rows, width):
    dist = d0 + (n_rows - 1) - np.arange(width)
    idx = np.clip(dist, -MAX_REL, MAX_REL) + MAX_REL
    return rel_table.astype(F32)[:, idx].reshape(N_HEADS, 1, width)


def _toeplitz_rows(g, n_rows, n_cols):
    width = g.shape[-1]
    assert n_cols + n_rows - 1 <= width and width % LANES == 0
    r = pltpu.roll(jnp.broadcast_to(g, (n_rows, width)), width - (n_rows - 1), axis=1,
                   stride=1, stride_axis=0)
    return r[:, :n_cols]


def _softmax_rows(s):
    m = jnp.max(s, axis=-1, keepdims=True)
    p = jnp.exp(s - m)
    return p, jnp.sum(p, axis=-1, keepdims=True)


def _band_prompt_kernel(q_ref, kp_ref, kc_ref, vp_ref, vc_ref, g_ref, o_ref, b_ref):
    qi = pl.program_id(1)
    t = TQ_BAND

    @pl.when(qi == 0)
    def _():
        qc = lax.broadcasted_iota(jnp.int32, (t, 3 * t), 0) // CHUNK
        kc = lax.broadcasted_iota(jnp.int32, (t, 3 * t), 1) // CHUNK
        ok = (kc >= qc) & (kc <= qc + BAND_CHUNKS)
        b_ref[...] = jnp.where(ok, _toeplitz_rows(g_ref[0], t, 3 * t), -jnp.inf)

    k = jnp.concatenate([kp_ref[...], kc_ref[...]], axis=0)
    v = jnp.concatenate([vp_ref[...], vc_ref[...]], axis=0)
    col = lax.broadcasted_iota(jnp.int32, (t, 3 * t), 1)
    for sub in range(TB_BAND // t):
        lo = sub * t
        q = q_ref[lo:lo + t, :]
        s = lax.dot_general(q, k[lo:lo + 3 * t], (((1,), (1,)), ((), ())),
                            preferred_element_type=F32) + b_ref[...]
        if lo < BAND_PAST:
            first_valid = jnp.where(qi == 0, BAND_PAST - lo, 0)
            s = jnp.where(col >= first_valid, s, -jnp.inf)
        p, l = _softmax_rows(s)
        o = jnp.dot(p.astype(BF16), v[lo:lo + 3 * t], preferred_element_type=F32)
        o_ref[lo:lo + t, :] = (o / l).astype(BF16)


def _band_prompt_attn(q, k, v, profile):
    t_len = q.shape[0]
    t = TQ_BAND
    tb = TB_BAND
    per = tb // BAND_PAST
    assert BAND_PAST == 2 * t and t % CHUNK == 0 and tb % BAND_PAST == 0 and t_len % tb == 0
    cur = pl.BlockSpec((tb, HEAD_DIM), lambda h, i: (i, h))
    prev = pl.BlockSpec((BAND_PAST, HEAD_DIM), lambda h, i: (jnp.maximum(i * per - 1, 0), h))
    return pl.pallas_call(
        _band_prompt_kernel,
        grid=(N_HEADS, t_len // tb),
        in_specs=[cur, prev, cur, prev, cur,
                  pl.BlockSpec((1, 1, 4 * t), lambda h, i: (h, 0, 0))],
        out_specs=cur,
        out_shape=jax.ShapeDtypeStruct((t_len, D_MODEL), BF16),
        scratch_shapes=[pltpu.VMEM((t, 3 * t), F32)],
        compiler_params=_cparams(("arbitrary", "arbitrary")),
        name="band_prompt_attn",
    )(q, k, k, v, v, profile)


def _band_sample_kernel(q_ref, kn_ref, vn_ref, kc_ref, vc_ref, g_ref, mask_ref, o_ref, b_ref, *,
                        n_cache):
    n_new = q_ref.shape[0]

    @pl.when(pl.program_id(0) == 0)
    def _():
        for h in range(N_HEADS):
            b_ref[h] = _toeplitz_rows(g_ref[h], n_new, n_cache + n_new) + mask_ref[...]

    for h in range(N_HEADS):
        hs = slice(h * HEAD_DIM, (h + 1) * HEAD_DIM)
        q = q_ref[:, hs]
        kc = kc_ref[0, pl.ds(h, n_cache, stride=N_HEADS), :].astype(BF16)
        vc = vc_ref[0, pl.ds(h, n_cache, stride=N_HEADS), :].astype(BF16)
        dn = (((1,), (1,)), ((), ()))
        s1 = lax.dot_general(q, kc, dn, preferred_element_type=F32) + b_ref[h, :, 0:n_cache]
        s2 = lax.dot_general(q, kn_ref[:, hs], dn, preferred_element_type=F32) + b_ref[h, :, n_cache:]
        m = jnp.maximum(jnp.max(s1, axis=-1, keepdims=True), jnp.max(s2, axis=-1, keepdims=True))
        p1 = jnp.exp(s1 - m)
        p2 = jnp.exp(s2 - m)
        l = jnp.sum(p1, axis=-1, keepdims=True) + jnp.sum(p2, axis=-1, keepdims=True)
        o = jnp.dot(p1.astype(BF16), vc, preferred_element_type=F32)
        o = o + jnp.dot(p2.astype(BF16), vn_ref[:, hs], preferred_element_type=F32)
        o_ref[:, hs] = (o / l).astype(BF16)


def _band_sample_attn(q, kn, vn, cache_k, cache_v, layer, rel_table, past, n_streams, n_new):
    n_cache = cache_k.shape[1] // N_HEADS
    n_keys = n_cache + n_new
    first = layer * n_streams
    q_pos = past + np.arange(n_new)
    k_pos = past - n_cache + np.arange(n_keys)
    qc, kc = q_pos[:, None] // CHUNK, k_pos[None, :] // CHUNK
    ok = (k_pos[None, :] >= 0) & (kc <= qc) & (kc >= qc - BAND_CHUNKS)
    mask = jnp.asarray(np.where(ok, 0.0, -np.inf), F32)
    width = -(-(n_keys + n_new - 1) // LANES) * LANES
    profile = _band_profile(rel_table, n_cache, n_new, width)
    return pl.pallas_call(
        functools.partial(_band_sample_kernel, n_cache=n_cache),
        grid=(n_streams,),
        in_specs=[
            pl.BlockSpec((n_new, D_MODEL), lambda b: (b, 0)),
            pl.BlockSpec((n_new, D_MODEL), lambda b: (b, 0)),
            pl.BlockSpec((n_new, D_MODEL), lambda b: (b, 0)),
            pl.BlockSpec((1, n_cache * N_HEADS, HEAD_DIM), lambda b: (first + b, 0, 0)),
            pl.BlockSpec((1, n_cache * N_HEADS, HEAD_DIM), lambda b: (first + b, 0, 0)),
            pl.BlockSpec((N_HEADS, 1, width), lambda b: (0, 0, 0)),
            pl.BlockSpec((n_new, n_keys), lambda b: (0, 0)),
        ],
        out_specs=pl.BlockSpec((n_new, D_MODEL), lambda b: (b, 0)),
        out_shape=jax.ShapeDtypeStruct((n_streams * n_new, D_MODEL), BF16),
        scratch_shapes=[pltpu.VMEM((N_HEADS, n_new, n_keys), F32)],
        compiler_params=_cparams(("arbitrary",)),
        name="band_sample_attn",
    )(q, kn, vn, cache_k, cache_v, profile, mask)


def _oproj_kernel(x_ref, a_ref, w_ref, o_ref):
    o_ref[...] = x_ref[...] + jnp.dot(a_ref[...], w_ref[...], preferred_element_type=F32)


def _out_proj(x, a, w_o_b):
    m = x.shape[0]
    tm = min(TM, m)
    return pl.pallas_call(
        _oproj_kernel,
        grid=(m // tm,),
        in_specs=[
            pl.BlockSpec((tm, D_MODEL), lambda i: (i, 0)),
            pl.BlockSpec((tm, D_MODEL), lambda i: (i, 0)),
            pl.BlockSpec((D_MODEL, D_MODEL), lambda i: (0, 0)),
        ],
        out_specs=pl.BlockSpec((tm, D_MODEL), lambda i: (i, 0)),
        out_shape=jax.ShapeDtypeStruct((m, D_MODEL), F32),
        compiler_params=_cparams(("arbitrary",)),
        name="out_proj",
    )(x, a, w_o_b)


def _ffn_kernel(*refs, stream_len, tm, n_f, final_norm):
    if final_norm:
        (x_ref, g_ref, wg_ref, wv_ref, hg_ref, hv_ref, cwg_ref, cwv_ref, cbg_ref, cbv_ref,
         wo_ref, gf_ref, o_ref, tg_ref, tv_ref, h_ref, pg_ref, pv_ref, u_ref) = refs
    else:
        (x_ref, g_ref, wg_ref, wv_ref, hg_ref, hv_ref, cwg_ref, cwv_ref, cbg_ref, cbv_ref,
         wo_ref, o_ref, tg_ref, tv_ref, h_ref, pg_ref, pv_ref, u_ref) = refs
        gf_ref = None
    i = pl.program_id(0)
    f = pl.program_id(1)
    streams_per_tile = max(tm // stream_len, 1)
    tiles_per_stream = max(stream_len // tm, 1)

    @pl.when(f == 0)
    def _():
        h_ref[...] = _rmsnorm_f32(x_ref[...], g_ref[...]).astype(BF16)
        o_ref[...] = x_ref[...]

    n_groups = FFN_ROW_GROUPS
    rg = tm // n_groups
    assert rg % SUBLANES == 0 and (streams_per_tile == 1 or streams_per_tile % n_groups == 0)
    row = lax.broadcasted_iota(jnp.int32, (rg, TF), 0)
    if streams_per_tile > 1:
        row = row % stream_len
    halves = ((wg_ref, hg_ref, cwg_ref, cbg_ref, pg_ref, tg_ref),
              (wv_ref, hv_ref, cwv_ref, cbv_ref, pv_ref, tv_ref))

    class _Up:
        def __init__(self, g, half):
            self.ref = u_ref.at[half * n_groups + g]
            self.ref[...] = jnp.dot(h_ref[g * rg:(g + 1) * rg, :], halves[half][0][...],
                                    preferred_element_type=F32)

        def __getitem__(self, idx):
            return self.ref[idx]

        def reshape(self, *shape):
            return self.ref[...].reshape(*shape)

    if streams_per_tile == 1:
        @pl.when((i % tiles_per_stream) == 0)
        def _():
            for (_, hist_ref, _, _, prev_ref, _) in halves:
                prev_ref[f, SUBLANES - (CONV_W - 1):SUBLANES, :] = hist_ref[0]

    ups = []

    def conv(g, half):
        _, hist_ref, cw_ref, cb_ref, prev_ref, _ = halves[half]
        u = ups[g][half][...]
        if streams_per_tile > 1:
            spg = streams_per_tile // n_groups
            def per_row(r):
                hr = hist_ref[g * spg:(g + 1) * spg, r:r + 1, :]
                return jnp.broadcast_to(hr, (spg, stream_len, TF)).reshape(rg, TF)
            p0, p1 = per_row(0), per_row(1)
        elif g == 0:
            p0 = prev_ref[f, SUBLANES - 2:SUBLANES - 1, :]
            p1 = prev_ref[f, SUBLANES - 1:SUBLANES, :]
        else:
            p0 = ups[g - 1][half][rg - 2:rg - 1, :]
            p1 = ups[g - 1][half][rg - 1:rg, :]
        u1 = jnp.where(row == 0, p1, pltpu.roll(u, 1, axis=0))
        u2 = jnp.where(row == 0, p0, jnp.where(row == 1, p1, pltpu.roll(u, 2, axis=0)))
        c = cb_ref[...] + cw_ref[0:1, :] * u2
        c = c + cw_ref[1:2, :] * u1
        return c + cw_ref[2:3, :] * u

    ups.append([_Up(0, half) for half in range(2)])
    for g in range(n_groups):
        cg = conv(g, 0)
        cv = conv(g, 1)
        a = ((cg * jax.nn.sigmoid(cg)) * cv).astype(BF16)
        if g + 1 < n_groups:
            ups.append([_Up(g + 1, half) for half in range(2)])
        o_ref[g * rg:(g + 1) * rg, :] += jnp.dot(a, wo_ref[...], preferred_element_type=F32)

    for half, (_, _, _, _, prev_ref, tail_ref) in enumerate(halves):
        if streams_per_tile > 1:
            spg = streams_per_tile // n_groups
            for g in range(n_groups):
                u3 = ups[g][half].reshape(spg, stream_len, TF)
                tail_ref[g * spg:(g + 1) * spg] = u3[:, stream_len - SUBLANES:, :]
        else:
            last = ups[n_groups - 1][half][rg - SUBLANES:, :]
            prev_ref[f] = last
            tail_ref[0] = last

    if final_norm:
        @pl.when(f == n_f - 1)
        def _():
            o_ref[...] = _rmsnorm_f32(o_ref[...], gf_ref[...])


def _conv_ffn(x, g, w_in_g, w_in_v, hist_g, hist_v, cw_g, cw_v, cb_g, cb_v, w_out_b, stream_len,
              g_final=None):
    m = x.shape[0]
    tm = min(TM_FFN, m)
    nf = D_FF_PAD // TF
    n_tiles = m // tm
    final_norm = g_final is not None
    if stream_len >= tm:
        assert stream_len % tm == 0
        tiles_per_stream = stream_len // tm
        hist_spec = pl.BlockSpec((1, CONV_W - 1, TF), lambda i, f: (i // tiles_per_stream, 0, f))
        tail_rows = n_tiles
        tail_spec = pl.BlockSpec((1, SUBLANES, TF), lambda i, f: (i, 0, f))
    else:
        assert tm % stream_len == 0 and stream_len % SUBLANES == 0 and n_tiles == 1
        spt = tm // stream_len
        hist_spec = pl.BlockSpec((spt, CONV_W - 1, TF), lambda i, f: (i, 0, f))
        tail_rows = m // stream_len
        tail_spec = pl.BlockSpec((spt, SUBLANES, TF), lambda i, f: (i, 0, f))
    row_tile = pl.BlockSpec((tm, D_MODEL), lambda i, f: (i, 0))
    gain = pl.BlockSpec((1, D_MODEL), lambda i, f: (0, 0))
    w_in_spec = pl.BlockSpec((D_MODEL, TF), lambda i, f: (0, f))
    cw_spec = pl.BlockSpec((CONV_W, TF), lambda i, f: (0, f))
    cb_spec = pl.BlockSpec((1, TF), lambda i, f: (0, f))
    in_specs = [row_tile, gain, w_in_spec, w_in_spec, hist_spec, hist_spec, cw_spec, cw_spec,
                cb_spec, cb_spec, pl.BlockSpec((TF, D_MODEL), lambda i, f: (f, 0))]
    args = [x, g.reshape(1, D_MODEL), w_in_g, w_in_v, hist_g, hist_v, cw_g, cw_v, cb_g, cb_v, w_out_b]
    if final_norm:
        in_specs.append(gain)
        args.append(g_final.reshape(1, D_MODEL))
    tail_shape = jax.ShapeDtypeStruct((tail_rows, SUBLANES, D_FF_PAD), F32)
    return pl.pallas_call(
        functools.partial(_ffn_kernel, stream_len=stream_len, tm=tm, n_f=nf, final_norm=final_norm),
        grid=(n_tiles, nf),
        in_specs=in_specs,
        out_specs=[row_tile, tail_spec, tail_spec],
        out_shape=[jax.ShapeDtypeStruct((m, D_MODEL), F32), tail_shape, tail_shape],
        scratch_shapes=[
            pltpu.VMEM((tm, D_MODEL), BF16),
            pltpu.VMEM((nf, SUBLANES, TF), F32),
            pltpu.VMEM((nf, SUBLANES, TF), F32),
            pltpu.VMEM((2 * FFN_ROW_GROUPS, tm // FFN_ROW_GROUPS, TF), F32),
        ],
        compiler_params=_cparams(("arbitrary", "arbitrary")),
        name="conv_ffn",
    )(*args)


def _pad_ff(a):
    pad = [(0, 0)] * (a.ndim - 1) + [(0, D_FF_PAD - D_FF)]
    return jnp.pad(a, pad)


def _split_ff(a):
    return _pad_ff(a[..., :D_FF]), _pad_ff(a[..., D_FF:])


def kernel(x_prompt, x_sample, cache_fox_k, cache_fox_v, cache_fox_logf, cache_band_k, cache_band_v,
           state_conv, g_attn, w_qkv, w_fgate, b_fgate, rel_bias, w_o, g_ffn, w_in, conv_w, conv_b,
           w_out, g_final):
    depth = g_attn.shape[0]
    bp, sp, _ = x_prompt.shape
    bs, ss, _ = x_sample.shape
    past = cache_fox_k.shape[2]
    n_band_cache = cache_band_k.shape[2]
    n_keep = min(BAND_PAST, sp)
    assert bp == 1

    xp = x_prompt.reshape(bp * sp, D_MODEL)
    xs = x_sample.reshape(bs * ss, D_MODEL)

    n_fox = cache_fox_k.shape[0]
    n_band = cache_band_k.shape[0]
    cfk = cache_fox_k.reshape(n_fox * bs, past * N_HEADS, HEAD_DIM)
    cfv = cache_fox_v.reshape(n_fox * bs, past * N_HEADS, HEAD_DIM)
    cbk = cache_band_k.reshape(n_band * bs, n_band_cache * N_HEADS, HEAD_DIM)
    cbv = cache_band_v.reshape(n_band * bs, n_band_cache * N_HEADS, HEAD_DIM)

    hist0 = jnp.zeros((bp, CONV_W - 1, D_FF_PAD), F32)

    fox_lp, fox_ks, fox_vs, fox_ls = [], [], [], []
    band_kp, band_vp, band_ks, band_vs = [], [], [], []
    conv_p, conv_s = [], []
    fox_kv_p = None

    for i in range(depth):
        j = i // N_MIXERS
        is_fox = (i % N_MIXERS == 0)
        wqkv_b = w_qkv[i].astype(BF16)
        wo_b = w_o[i].astype(BF16)
        w_in_g, w_in_v = _split_ff(w_in[i].astype(BF16))
        cw_g, cw_v = _split_ff(conv_w[i])
        cb_g, cb_v = _split_ff(conv_b[i].reshape(1, 2 * D_FF))
        w_out_b = jnp.pad(w_out[i].astype(BF16), ((0, D_FF_PAD - D_FF), (0, 0)))
        hs_g, hs_v = _split_ff(state_conv[i])

        if is_fox:
            wf_b = jnp.pad(w_fgate[j].astype(BF16), ((0, 0), (0, LANES - N_HEADS)))
            bf = jnp.pad(b_fgate[j].reshape(1, N_HEADS), ((0, 0), (0, LANES - N_HEADS)))
            qp, kst, vst, kpb, vpb, lfp_pad = _qkv_proj(xp, g_attn[i], wqkv_b, wf_b, bf,
                                                        slot=j, n_slots=n_fox, kv_stack=fox_kv_p)
            fox_kv_p = (kst, vst)
            qs, ks32, vs32, ksb, vsb, lfs = _qkv_proj(xs, g_attn[i], wqkv_b, wf_b, bf)
            lfp = lfp_pad[:, :N_HEADS]
            lfs = lfs[:, :N_HEADS]

            aq, ak, cum_pad, qn2, kn2 = _fox_aug(lfp_pad, qp, kpb)
            op = _fox_prompt_attn(qp, aq, kpb, ak, vpb, _fox_first_live_pair(cum_pad, qn2, kn2))

            lf_all = jnp.concatenate([
                jnp.transpose(cache_fox_logf[j], (0, 2, 1)),
                jnp.transpose(lfs.reshape(bs, ss, N_HEADS), (0, 2, 1)),
                jnp.zeros((bs, N_HEADS, LANES - ss), F32),
            ], axis=-1).reshape(bs * N_HEADS, past + LANES)
            cum_s = _cumsum_lanes(lf_all).reshape(bs, N_HEADS, past + LANES)
            cq_s = jnp.transpose(cum_s[:, :, past:past + ss], (0, 2, 1))
            osm = _fox_sample_attn(qs, ksb, vsb, cfk, cfv, j, cq_s, cum_s, bs, ss)

            fox_lp.append(lfp.reshape(bp, sp, N_HEADS))
            fox_ks.append(ks32.reshape(bs, ss, N_HEADS, HEAD_DIM))
            fox_vs.append(vs32.reshape(bs, ss, N_HEADS, HEAD_DIM))
            fox_ls.append(lfs.reshape(bs, ss, N_HEADS))
        else:
            qp, kp32, vp32, kpb, vpb = _qkv_proj(xp, g_attn[i], wqkv_b)
            qs, ks32, vs32, ksb, vsb = _qkv_proj(xs, g_attn[i], wqkv_b)
            op = _band_prompt_attn(qp, kpb, vpb,
                                   _band_profile(rel_bias[j], BAND_PAST, TQ_BAND, 4 * TQ_BAND))
            osm = _band_sample_attn(qs, ksb, vsb, cbk, cbv, j, rel_bias[j], past, bs, ss)
            last_rows = slice(sp - n_keep, sp)
            band_kp.append(kp32[0, last_rows].reshape(bp, n_keep, N_HEADS, HEAD_DIM))
            band_vp.append(vp32[0, last_rows].reshape(bp, n_keep, N_HEADS, HEAD_DIM))
            band_ks.append(ks32.reshape(bs, ss, N_HEADS, HEAD_DIM))
            band_vs.append(vs32.reshape(bs, ss, N_HEADS, HEAD_DIM))

        xp = _out_proj(xp, op, wo_b)
        xs = _out_proj(xs, osm, wo_b)

        gf = g_final if i == depth - 1 else None
        xp, tgp, tvp = _conv_ffn(xp, g_ffn[i], w_in_g, w_in_v, hist0, hist0, cw_g, cw_v, cb_g, cb_v,
                                 w_out_b, sp, gf)
        xs, tgs, tvs = _conv_ffn(xs, g_ffn[i], w_in_g, w_in_v, hs_g, hs_v, cw_g, cw_v, cb_g, cb_v,
                                 w_out_b, ss, gf)
        keep = slice(SUBLANES - (CONV_W - 1), SUBLANES)
        conv_p.append(jnp.concatenate([tgp[-1:, keep, :D_FF], tvp[-1:, keep, :D_FF]], axis=-1))
        conv_s.append(jnp.concatenate([tgs[:, keep, :D_FF], tvs[:, keep, :D_FF]], axis=-1))

    y_prompt = xp.reshape(bp, sp, D_MODEL)
    y_sample = xs.reshape(bs, ss, D_MODEL)
    fox_kp, fox_vp = (a.reshape(n_fox, bp, sp, N_HEADS, HEAD_DIM) for a in fox_kv_p)
    return (y_prompt, y_sample,
            fox_kp, fox_vp, jnp.stack(fox_lp),
            jnp.stack(fox_ks), jnp.stack(fox_vs), jnp.stack(fox_ls),
            jnp.stack(band_kp), jnp.stack(band_vp), jnp.stack(band_ks), jnp.stack(band_vs),
            jnp.stack(conv_p), jnp.stack(conv_s))
```

```python
import functools
import math

import numpy as np
import jax
import jax.numpy as jnp
from jax import lax
from jax.experimental import pallas as pl
from jax.experimental.pallas import tpu as pltpu

F32 = jnp.float32
BF16 = jnp.bfloat16

D_MODEL = 2048
N_HEADS = 16
HEAD_DIM = D_MODEL // N_HEADS
SCALE = 1.0 / math.sqrt(HEAD_DIM)
FOX_DEAD_LOGIT = 128.0
CHUNK = 64
BAND_CHUNKS = 8
BAND_PAST = BAND_CHUNKS * CHUNK
MAX_REL = 2 * CHUNK
D_FF = 5504
CONV_W = 3
EPS = 1e-6
N_MIXERS = 2

LANES = 128
SUBLANES = 8
VMEM_LIMIT = 56 * 1024 * 1024

TM = 512
TM_QKV = 256
TM_FFN = 512
TN_QKV = 512
TF = 512
FFN_ROW_GROUPS = 2
D_FF_PAD = ((D_FF + TF - 1) // TF) * TF
TK_FOX = 512
TQ_FOX = 2 * TK_FOX
TK_FOX_S = 1024
TQ_BAND = 256
TB_BAND = 2048
CUM_BLK = 128


def _cparams(sem):
    return pltpu.CompilerParams(dimension_semantics=sem, vmem_limit_bytes=VMEM_LIMIT)


def _rmsnorm_f32(x, g):
    ms = jnp.mean(x * x, axis=-1, keepdims=True)
    return (x * lax.rsqrt(ms + EPS)) * g


def _qkv_kernel(*refs, with_fgate, n_aliased, q_scale):
    refs = refs[:3 + 2 * with_fgate] + refs[3 + 2 * with_fgate + n_aliased:]
    if with_fgate:
        (x_ref, g_ref, w_ref, wf_ref, bf_ref,
         q_ref, k32_ref, v32_ref, kb_ref, vb_ref, lf_ref) = refs
    else:
        x_ref, g_ref, w_ref, q_ref, k32_ref, v32_ref, kb_ref, vb_ref = refs

    h = _rmsnorm_f32(x_ref[...], g_ref[...]).astype(BF16)
    if with_fgate:
        z = jnp.dot(h, wf_ref[...], preferred_element_type=F32) + bf_ref[...]
        lf_ref[...] = jnp.minimum(z, 0.0) - jnp.log1p(jnp.exp(-jnp.abs(z)))

    def project(part, c):
        lo = part * D_MODEL + c * TN_QKV
        return jnp.dot(h, w_ref[:, lo:lo + TN_QKV], preferred_element_type=F32)

    for c in range(D_MODEL // TN_QKV):
        cs = slice(c * TN_QKV, (c + 1) * TN_QKV)
        q_ref[:, cs] = (project(0, c) * q_scale).astype(BF16)
        k = project(1, c)
        k32_ref[:, cs] = k
        kb_ref[:, cs] = k.astype(BF16)
        v = project(2, c)
        v32_ref[:, cs] = v
        vb_ref[:, cs] = v.astype(BF16)


def _qkv_proj(x, g, w_qkv_b, wf_b=None, bf=None, slot=0, n_slots=1, kv_stack=None, q_scale=SCALE):
    m = x.shape[0]
    tm = min(TM_QKV, m)
    with_fgate = wf_b is not None
    stack_tile = pl.BlockSpec((None, tm, D_MODEL), lambda i: (slot, i, 0))
    stack_shape = jax.ShapeDtypeStruct((n_slots, m, D_MODEL), F32)
    resident = lambda shape: pl.BlockSpec(shape, lambda i: (0, 0), pipeline_mode=pl.Buffered(1))
    in_specs = [
        pl.BlockSpec((tm, D_MODEL), lambda i: (i, 0)),
        resident((1, D_MODEL)),
        resident((D_MODEL, 3 * D_MODEL)),
    ]
    args = [x, g.reshape(1, D_MODEL), w_qkv_b]
    tile = pl.BlockSpec((tm, D_MODEL), lambda i: (i, 0))
    out_specs = [tile, stack_tile, stack_tile, tile, tile]
    out_shape = [
        jax.ShapeDtypeStruct((m, D_MODEL), BF16),
        stack_shape,
        stack_shape,
        jax.ShapeDtypeStruct((m, D_MODEL), BF16),
        jax.ShapeDtypeStruct((m, D_MODEL), BF16),
    ]
    if with_fgate:
        in_specs += [resident((D_MODEL, LANES)), resident((1, LANES))]
        args += [wf_b, bf]
        out_specs.append(pl.BlockSpec((tm, LANES), lambda i: (i, 0)))
        out_shape.append(jax.ShapeDtypeStruct((m, LANES), F32))
    aliases = {}
    if kv_stack is not None:
        aliases = {len(args): 1, len(args) + 1: 2}
        in_specs += [pl.BlockSpec(memory_space=pl.ANY)] * 2
        args += list(kv_stack)
    return pl.pallas_call(
        functools.partial(_qkv_kernel, with_fgate=with_fgate, n_aliased=len(aliases),
                          q_scale=q_scale),
        grid=(m // tm,),
        in_specs=in_specs,
        out_specs=out_specs,
        out_shape=out_shape,
        input_output_aliases=aliases,
        compiler_params=_cparams(("arbitrary",)),
        name="qkv_proj",
    )(*args)


def _cumsum_kernel(x_ref, o_ref, carry_ref):
    @pl.when(pl.program_id(0) == 0)
    def _():
        carry_ref[...] = jnp.zeros_like(carry_ref)

    x = x_ref[...]
    hi = x.astype(BF16)
    r1 = x - hi.astype(F32)
    mid = r1.astype(BF16)
    lo = (r1 - mid.astype(F32)).astype(BF16)
    row = lax.broadcasted_iota(jnp.int32, (CUM_BLK, CUM_BLK), 0)
    col = lax.broadcasted_iota(jnp.int32, (CUM_BLK, CUM_BLK), 1)
    tri = (row <= col).astype(BF16)
    s = jnp.dot(lo, tri, preferred_element_type=F32)
    s = s + jnp.dot(mid, tri, preferred_element_type=F32)
    s = s + jnp.dot(hi, tri, preferred_element_type=F32)
    s = s + carry_ref[...]
    o_ref[...] = s
    carry_ref[...] = jnp.broadcast_to(s[:, CUM_BLK - 1:CUM_BLK], s.shape)


def _cumsum_lanes(x):
    r, t = x.shape
    return pl.pallas_call(
        _cumsum_kernel,
        grid=(t // CUM_BLK,),
        in_specs=[pl.BlockSpec((r, CUM_BLK), lambda i: (0, i))],
        out_specs=pl.BlockSpec((r, CUM_BLK), lambda i: (0, i)),
        out_shape=jax.ShapeDtypeStruct((r, t), F32),
        scratch_shapes=[pltpu.VMEM((r, CUM_BLK), F32)],
        compiler_params=_cparams(("arbitrary",)),
        name="cumsum_time",
    )(x)


def _split3_bf16(x):
    hi = x.astype(BF16)
    r1 = x - hi.astype(F32)
    mid = r1.astype(BF16)
    lo = (r1 - mid.astype(F32)).astype(BF16)
    return jnp.concatenate([hi, mid, lo], axis=-1)


def _fox_aug_placement():
    pq = np.zeros((3 * LANES, D_MODEL), np.float32)
    pk = np.zeros((3 * LANES, D_MODEL), np.float32)
    cq = np.zeros((1, D_MODEL), np.float32)
    ck = np.zeros((1, D_MODEL), np.float32)
    for h in range(N_HEADS):
        for piece in range(3):
            pq[piece * LANES + h, h * HEAD_DIM + piece] = 1.0
            pk[piece * LANES + h, h * HEAD_DIM + 3 + piece] = -1.0
            cq[0, h * HEAD_DIM + 3 + piece] = 1.0
            ck[0, h * HEAD_DIM + piece] = 1.0
    return (jnp.asarray(pq, BF16), jnp.asarray(pk, BF16), jnp.asarray(cq), jnp.asarray(ck))


def _fox_aug_kernel(x_ref, pq_ref, pk_ref, cq_ref, ck_ref, q_ref, k_ref, e_ref,
                    aq_ref, ak_ref, cum_ref, qn_ref, kn_ref, carry_ref):
    @pl.when(pl.program_id(0) == 0)
    def _():
        carry_ref[...] = jnp.zeros_like(carry_ref)
        qn_ref[...] = jnp.zeros_like(qn_ref)
        kn_ref[...] = jnp.zeros_like(kn_ref)

    for a_ref, n_ref in ((q_ref, qn_ref), (k_ref, kn_ref)):
        a = a_ref[...].astype(F32)
        n2 = jnp.dot((a * a).astype(BF16), e_ref[...], preferred_element_type=F32)
        n_ref[...] = jnp.maximum(n_ref[...], jnp.max(n2, axis=0, keepdims=True))

    tb = x_ref.shape[0]
    row = lax.broadcasted_iota(jnp.int32, (tb, tb), 0)
    col = lax.broadcasted_iota(jnp.int32, (tb, tb), 1)
    tri = (col <= row).astype(BF16)
    x3 = _split3_bf16(x_ref[...])
    c3 = jnp.dot(tri, x3, preferred_element_type=F32)
    cum = (c3[:, 2 * LANES:] + c3[:, LANES:2 * LANES]) + c3[:, :LANES] + carry_ref[...]
    carry_ref[...] = cum[tb - 1:tb, :]
    cum_ref[...] = cum
    cum3 = _split3_bf16(cum)
    aq = jnp.dot(cum3, pq_ref[...], preferred_element_type=F32) + cq_ref[...]
    ak = jnp.dot(cum3, pk_ref[...], preferred_element_type=F32) + ck_ref[...]
    aq_ref[...] = aq.astype(BF16)
    ak_ref[...] = ak.astype(BF16)


def _fox_aug(logf_pad, q, k):
    t_len = logf_pad.shape[0]
    tb = 2 * LANES
    pq, pk, cq, ck = _fox_aug_placement()
    head_of_col = np.arange(D_MODEL)[:, None] // HEAD_DIM == np.arange(LANES)[None, :]
    const = lambda shape: pl.BlockSpec(shape, lambda i: (0, 0))
    rows = lambda width: pl.BlockSpec((tb, width), lambda i: (i, 0))
    return pl.pallas_call(
        _fox_aug_kernel,
        grid=(t_len // tb,),
        in_specs=[
            rows(LANES),
            const((3 * LANES, D_MODEL)), const((3 * LANES, D_MODEL)),
            const((1, D_MODEL)), const((1, D_MODEL)),
            rows(D_MODEL), rows(D_MODEL), const((D_MODEL, LANES)),
        ],
        out_specs=[rows(D_MODEL), rows(D_MODEL), rows(LANES), const((1, LANES)), const((1, LANES))],
        out_shape=[jax.ShapeDtypeStruct((t_len, D_MODEL), BF16)] * 2
        + [jax.ShapeDtypeStruct((t_len, LANES), F32)]
        + [jax.ShapeDtypeStruct((1, LANES), F32)] * 2,
        scratch_shapes=[pltpu.VMEM((1, LANES), F32)],
        compiler_params=_cparams(("arbitrary",)),
        name="fox_aug",
    )(logf_pad, pq, pk, cq, ck, q, k, jnp.asarray(head_of_col, BF16))


def _fox_prompt_kernel(start_ref, q_ref, aq_ref, k_ref, ak_ref, v_ref, o_ref,
                       s0_ref, s1_ref, p0_ref, p1_ref, a0_ref, a1_ref, m_ref, acc_ref):
    qi = pl.program_id(1)
    j0 = start_ref[pl.program_id(0), qi]
    tq, tk = TQ_FOX, TK_FOX
    dn = (((1,), (1,)), ((), ()))
    q = jnp.concatenate([q_ref[...], aq_ref[...]], axis=1)
    ones_col = (lax.broadcasted_iota(jnp.int32, (tk, HEAD_DIM), 1) == 0).astype(BF16)

    everything, lower = slice(0, tq), slice(tk, tq)

    def stat(ref, rows):
        return ref[...].reshape(tq, 1) if rows is everything else ref[1]

    def set_stat(ref, rows, value):
        if rows is everything:
            ref[...] = value.reshape(2, tk, 1)
        else:
            ref[1] = value

    def scores(ki, s_ref, rows=everything):
        ks = pl.multiple_of(ki * tk, tk)
        k = jnp.concatenate([k_ref[pl.ds(ks, tk), :], ak_ref[pl.ds(ks, tk), :]], axis=1)
        s_ref[rows, :] = lax.dot_general(q[rows], k, dn, preferred_element_type=F32)

    def accumulate(ki, p_ref, a_ref, rows=everything):
        ks = pl.multiple_of(ki * tk, tk)
        v = jnp.concatenate([v_ref[pl.ds(ks, tk), :], ones_col], axis=1)
        pv = jnp.dot(p_ref[rows, :], v, preferred_element_type=F32)
        acc_ref[rows, :] = stat(a_ref, rows) * acc_ref[rows, :] + pv

    def softmax(s, p_ref, a_ref, rows=everything):
        m_prev = stat(m_ref, rows)
        m_new = jnp.maximum(m_prev, jnp.max(s, axis=-1, keepdims=True))
        set_stat(a_ref, rows, jnp.exp(m_prev - m_new))
        p_ref[rows, :] = jnp.exp(s - m_new).astype(BF16)
        set_stat(m_ref, rows, m_new)

    m_ref[...] = jnp.full_like(m_ref, -jnp.inf)
    acc_ref[...] = jnp.zeros_like(acc_ref)
    p1_ref[...] = jnp.zeros_like(p1_ref)
    a1_ref[...] = jnp.ones_like(a1_ref)
    scores(2 * j0, s0_ref)

    def body(j, carry):
        scores(2 * j + 1, s1_ref)
        accumulate(jnp.maximum(2 * j - 1, 0), p1_ref, a1_ref)
        softmax(s0_ref[...], p0_ref, a0_ref)
        scores(2 * j + 2, s0_ref)
        accumulate(2 * j, p0_ref, a0_ref)
        softmax(s1_ref[...], p1_ref, a1_ref)
        return carry

    lax.fori_loop(j0, qi, body, 0)

    row = lax.broadcasted_iota(jnp.int32, (tq, tk), 0)
    col = lax.broadcasted_iota(jnp.int32, (tq, tk), 1)
    scores(2 * qi + 1, s1_ref, lower)
    accumulate(jnp.maximum(2 * qi - 1, 0), p1_ref, a1_ref)
    softmax(jnp.where(col <= row, s0_ref[...], -jnp.inf), p0_ref, a0_ref)
    accumulate(2 * qi, p0_ref, a0_ref)
    row_l = lax.broadcasted_iota(jnp.int32, (tq - tk, tk), 0)
    col_l = lax.broadcasted_iota(jnp.int32, (tq - tk, tk), 1)
    softmax(jnp.where(col_l <= row_l, s1_ref[lower, :], -jnp.inf), p1_ref, a1_ref, lower)
    accumulate(2 * qi + 1, p1_ref, a1_ref, lower)
    acc = acc_ref[...]
    o_ref[...] = (acc[:, :HEAD_DIM] / acc[:, HEAD_DIM:HEAD_DIM + 1]).astype(BF16)


def _fox_first_live_pair(cum_pad, qn2, kn2):
    nq = cum_pad.shape[0] // TQ_FOX
    slack = 2.0 * 1.01 * jnp.sqrt(qn2[0, :N_HEADS] * kn2[0, :N_HEADS])
    cum = cum_pad[:, :N_HEADS]
    c_tile_start = cum[0::TQ_FOX]
    c_pair_end = cum[TQ_FOX - 1::TQ_FOX]
    bound = c_tile_start[:, None, :] - c_pair_end[None, :, :] + slack[None, None, :]
    before = np.arange(nq)[None, :, None] < np.arange(nq)[:, None, None]
    dead = jnp.logical_and(bound < -FOX_DEAD_LOGIT, before)
    lead = jnp.cumprod(dead.astype(jnp.int32), axis=1)
    return jnp.transpose(jnp.sum(lead, axis=1)).astype(jnp.int32)


def _fox_prompt_attn(q, aq, k, ak, v, first_pair):
    t_len = q.shape[0]
    tq, tk = TQ_FOX, TK_FOX
    assert tq == 2 * tk and t_len % tq == 0
    tile = pl.BlockSpec((tq, HEAD_DIM), lambda h, i, start: (i, h))
    whole = pl.BlockSpec((t_len, HEAD_DIM), lambda h, i, start: (0, h))
    return pl.pallas_call(
        _fox_prompt_kernel,
        grid_spec=pltpu.PrefetchScalarGridSpec(
            num_scalar_prefetch=1,
            grid=(N_HEADS, t_len // tq),
            in_specs=[tile, tile, whole, whole, whole],
            out_specs=tile,
            scratch_shapes=[
                pltpu.VMEM((tq, tk), F32), pltpu.VMEM((tq, tk), F32),
                pltpu.VMEM((tq, tk), BF16), pltpu.VMEM((tq, tk), BF16),
                pltpu.VMEM((2, tk, 1), F32), pltpu.VMEM((2, tk, 1), F32),
                pltpu.VMEM((2, tk, 1), F32),
                pltpu.VMEM((tq, 2 * HEAD_DIM), F32),
            ],
        ),
        out_shape=jax.ShapeDtypeStruct((t_len, D_MODEL), BF16),
        compiler_params=_cparams(("arbitrary", "arbitrary")),
        name="fox_prompt_attn",
    )(first_pair, q, aq, k, ak, v)


def _fox_sample_kernel(q_ref, kn_ref, vn_ref, kc_ref, vc_ref, cq_ref, ckc_ref, ckn_ref,
                       o_ref, s_ref, sn_ref, m_ref, acc_ref, *, n_new, tk):
    ki = pl.program_id(1)
    nk = pl.num_programs(1)
    dn = (((1,), (1,)), ((), ()))

    @pl.when(ki == 0)
    def _():
        m_ref[...] = jnp.full_like(m_ref, -jnp.inf)
        acc_ref[...] = jnp.zeros_like(acc_ref)

    def head_rows(h):
        return slice(h * n_new, (h + 1) * n_new)

    def head_cols(h):
        return slice(h * HEAD_DIM, (h + 1) * HEAD_DIM)

    def softmax_update(s):
        m_prev = m_ref[...]
        m_new = jnp.maximum(m_prev, jnp.max(s, axis=-1, keepdims=True))
        m_ref[...] = m_new
        return jnp.exp(m_prev - m_new), jnp.exp(s - m_new).astype(BF16)

    def accumulate(alpha, p, values):
        n_keys = p.shape[1]
        ones_col = (lax.broadcasted_iota(jnp.int32, (n_keys, HEAD_DIM), 1) == 0).astype(BF16)
        for h in range(N_HEADS):
            r = head_rows(h)
            pv = jnp.dot(p[r], jnp.concatenate([values(h), ones_col], axis=1),
                         preferred_element_type=F32)
            acc_ref[r, :] = alpha[r] * acc_ref[r, :] + pv

    for h in range(N_HEADS):
        kc = kc_ref[0, pl.ds(h, tk, stride=N_HEADS), :].astype(BF16)
        s = lax.dot_general(q_ref[:, head_cols(h)], kc, dn, preferred_element_type=F32)
        s_ref[head_rows(h), :] = (s + cq_ref[0, :, h:h + 1]) - ckc_ref[0, h:h + 1, :]
    alpha, p = softmax_update(s_ref[...])
    accumulate(alpha, p, lambda h: vc_ref[0, pl.ds(h, tk, stride=N_HEADS), :].astype(BF16))

    @pl.when(ki == nk - 1)
    def _():
        row = lax.broadcasted_iota(jnp.int32, (n_new, n_new), 0)
        col = lax.broadcasted_iota(jnp.int32, (n_new, n_new), 1)
        for h in range(N_HEADS):
            s = lax.dot_general(q_ref[:, head_cols(h)], kn_ref[:, head_cols(h)], dn,
                                preferred_element_type=F32)
            s = (s + cq_ref[0, :, h:h + 1]) - ckn_ref[0, h:h + 1, 0:n_new]
            sn_ref[head_rows(h), :] = jnp.where(col <= row, s, -jnp.inf)
        alpha, p = softmax_update(sn_ref[...])
        accumulate(alpha, p, lambda h: vn_ref[:, head_cols(h)])
        for h in range(N_HEADS):
            acc = acc_ref[head_rows(h), :]
            o_ref[:, head_cols(h)] = (acc[:, :HEAD_DIM] / acc[:, HEAD_DIM:HEAD_DIM + 1]).astype(BF16)


def _fox_sample_attn(q, kn, vn, cache_k, cache_v, layer, cq_bth, cum_bht, n_streams, n_new):
    past = cache_k.shape[1] // N_HEADS
    tk = TK_FOX_S
    first = layer * n_streams
    rows = N_HEADS * n_new
    return pl.pallas_call(
        functools.partial(_fox_sample_kernel, n_new=n_new, tk=tk),
        grid=(n_streams, past // tk),
        in_specs=[
            pl.BlockSpec((n_new, D_MODEL), lambda b, i: (b, 0)),
            pl.BlockSpec((n_new, D_MODEL), lambda b, i: (b, 0)),
            pl.BlockSpec((n_new, D_MODEL), lambda b, i: (b, 0)),
            pl.BlockSpec((1, tk * N_HEADS, HEAD_DIM), lambda b, i: (first + b, i, 0)),
            pl.BlockSpec((1, tk * N_HEADS, HEAD_DIM), lambda b, i: (first + b, i, 0)),
            pl.BlockSpec((1, n_new, N_HEADS), lambda b, i: (b, 0, 0)),
            pl.BlockSpec((1, N_HEADS, tk), lambda b, i: (b, 0, i)),
            pl.BlockSpec((1, N_HEADS, LANES), lambda b, i: (b, 0, past // LANES)),
        ],
        out_specs=pl.BlockSpec((n_new, D_MODEL), lambda b, i: (b, 0)),
        out_shape=jax.ShapeDtypeStruct((n_streams * n_new, D_MODEL), BF16),
        scratch_shapes=[
            pltpu.VMEM((rows, tk), F32),
            pltpu.VMEM((rows, n_new), F32),
            pltpu.VMEM((rows, 1), F32),
            pltpu.VMEM((rows, 2 * HEAD_DIM), F32),
        ],
        compiler_params=_cparams(("arbitrary", "arbitrary")),
        name="fox_sample_attn",
    )(q, kn, vn, cache_k, cache_v, cq_bth, cum_bht, cum_bht)


def _band_profile(rel_table, d0, n_rows, width):
    dist = d0 + (n_rows - 1) - np.arange(width)
    idx = np.clip(dist, -MAX_REL, MAX_REL) + MAX_REL
    return rel_table.astype(F32)[:, idx].reshape(N_HEADS, 1, width)


def _toeplitz_rows(g, n_rows, n_cols):
    width = g.shape[-1]
    assert n_cols + n_rows - 1 <= width and width % LANES == 0
    r = pltpu.roll(jnp.broadcast_to(g, (n_rows, width)), width - (n_rows - 1), axis=1,
                   stride=1, stride_axis=0)
    return r[:, :n_cols]


def _softmax_rows(s):
    m = jnp.max(s, axis=-1, keepdims=True)
    p = jnp.exp(s - m)
    return p, jnp.sum(p, axis=-1, keepdims=True)


def _band_prompt_kernel(q_ref, kp_ref, kc_ref, vp_ref, vc_ref, g_ref, o_ref, b_ref):
    qi = pl.program_id(1)
    t = TQ_BAND

    @pl.when(qi == 0)
    def _():
        qc = lax.broadcasted_iota(jnp.int32, (t, 3 * t), 0) // CHUNK
        kc = lax.broadcasted_iota(jnp.int32, (t, 3 * t), 1) // CHUNK
        ok = (kc >= qc) & (kc <= qc + BAND_CHUNKS)
        b_ref[...] = jnp.where(ok, _toeplitz_rows(g_ref[0], t, 3 * t), -jnp.inf)

    k = jnp.concatenate([kp_ref[...], kc_ref[...]], axis=0)
    v = jnp.concatenate([vp_ref[...], vc_ref[...]], axis=0)
    col = lax.broadcasted_iota(jnp.int32, (t, 3 * t), 1)
    for sub in range(TB_BAND // t):
        lo = sub * t
        q = q_ref[lo:lo + t, :]
        s = lax.dot_general(q, k[lo:lo + 3 * t], (((1,), (1,)), ((), ())),
                            preferred_element_type=F32) + b_ref[...]
        if lo < BAND_PAST:
            first_valid = jnp.where(qi == 0, BAND_PAST - lo, 0)
            s = jnp.where(col >= first_valid, s, -jnp.inf)
        p, l = _softmax_rows(s)
        o = jnp.dot(p.astype(BF16), v[lo:lo + 3 * t], preferred_element_type=F32)
        o_ref[lo:lo + t, :] = (o / l).astype(BF16)


def _band_prompt_attn(q, k, v, profile):
    t_len = q.shape[0]
    t = TQ_BAND
    tb = TB_BAND
    per = tb // BAND_PAST
    assert BAND_PAST == 2 * t and t % CHUNK == 0 and tb % BAND_PAST == 0 and t_len % tb == 0
    cur = pl.BlockSpec((tb, HEAD_DIM), lambda h, i: (i, h))
    prev = pl.BlockSpec((BAND_PAST, HEAD_DIM), lambda h, i: (jnp.maximum(i * per - 1, 0), h))
    return pl.pallas_call(
        _band_prompt_kernel,
        grid=(N_HEADS, t_len // tb),
        in_specs=[cur, prev, cur, prev, cur,
                  pl.BlockSpec((1, 1, 4 * t), lambda h, i: (h, 0, 0))],
        out_specs=cur,
        out_shape=jax.ShapeDtypeStruct((t_len, D_MODEL), BF16),
        scratch_shapes=[pltpu.VMEM((t, 3 * t), F32)],
        compiler_params=_cparams(("arbitrary", "arbitrary")),
        name="band_prompt_attn",
    )(q, k, k, v, v, profile)


def _band_sample_kernel(q_ref, kn_ref, vn_ref, kc_ref, vc_ref, g_ref, mask_ref, o_ref, b_ref, *,
                        n_cache):
    n_new = q_ref.shape[0]

    @pl.when(pl.program_id(0) == 0)
    def _():
        for h in range(N_HEADS):
            b_ref[h] = _toeplitz_rows(g_ref[h], n_new, n_cache + n_new) + mask_ref[...]

    for h in range(N_HEADS):
        hs = slice(h * HEAD_DIM, (h + 1) * HEAD_DIM)
        q = q_ref[:, hs]
        kc = kc_ref[0, pl.ds(h, n_cache, stride=N_HEADS), :].astype(BF16)
        vc = vc_ref[0, pl.ds(h, n_cache, stride=N_HEADS), :].astype(BF16)
        dn = (((1,), (1,)), ((), ()))
        s1 = lax.dot_general(q, kc, dn, preferred_element_type=F32) + b_ref[h, :, 0:n_cache]
        s2 = lax.dot_general(q, kn_ref[:, hs], dn, preferred_element_type=F32) + b_ref[h, :, n_cache:]
        m = jnp.maximum(jnp.max(s1, axis=-1, keepdims=True), jnp.max(s2, axis=-1, keepdims=True))
        p1 = jnp.exp(s1 - m)
        p2 = jnp.exp(s2 - m)
        l = jnp.sum(p1, axis=-1, keepdims=True) + jnp.sum(p2, axis=-1, keepdims=True)
        o = jnp.dot(p1.astype(BF16), vc, preferred_element_type=F32)
        o = o + jnp.dot(p2.astype(BF16), vn_ref[:, hs], preferred_element_type=F32)
        o_ref[:, hs] = (o / l).astype(BF16)


def _band_sample_attn(q, kn, vn, cache_k, cache_v, layer, rel_table, past, n_streams, n_new):
    n_cache = cache_k.shape[1] // N_HEADS
    n_keys = n_cache + n_new
    first = layer * n_streams
    q_pos = past + np.arange(n_new)
    k_pos = past - n_cache + np.arange(n_keys)
    qc, kc = q_pos[:, None] // CHUNK, k_pos[None, :] // CHUNK
    ok = (k_pos[None, :] >= 0) & (kc <= qc) & (kc >= qc - BAND_CHUNKS)
    mask = jnp.asarray(np.where(ok, 0.0, -np.inf), F32)
    width = -(-(n_keys + n_new - 1) // LANES) * LANES
    profile = _band_profile(rel_table, n_cache, n_new, width)
    return pl.pallas_call(
        functools.partial(_band_sample_kernel, n_cache=n_cache),
        grid=(n_streams,),
        in_specs=[
            pl.BlockSpec((n_new, D_MODEL), lambda b: (b, 0)),
            pl.BlockSpec((n_new, D_MODEL), lambda b: (b, 0)),
            pl.BlockSpec((n_new, D_MODEL), lambda b: (b, 0)),
            pl.BlockSpec((1, n_cache * N_HEADS, HEAD_DIM), lambda b: (first + b, 0, 0)),
            pl.BlockSpec((1, n_cache * N_HEADS, HEAD_DIM), lambda b: (first + b, 0, 0)),
            pl.BlockSpec((N_HEADS, 1, width), lambda b: (0, 0, 0)),
            pl.BlockSpec((n_new, n_keys), lambda b: (0, 0)),
        ],
        out_specs=pl.BlockSpec((n_new, D_MODEL), lambda b: (b, 0)),
        out_shape=jax.ShapeDtypeStruct((n_streams * n_new, D_MODEL), BF16),
        scratch_shapes=[pltpu.VMEM((N_HEADS, n_new, n_keys), F32)],
        compiler_params=_cparams(("arbitrary",)),
        name="band_sample_attn",
    )(q, kn, vn, cache_k, cache_v, profile, mask)


def _oproj_kernel(x_ref, a_ref, w_ref, o_ref):
    o_ref[...] = x_ref[...] + jnp.dot(a_ref[...], w_ref[...], preferred_element_type=F32)


def _out_proj(x, a, w_o_b):
    m = x.shape[0]
    tm = min(TM, m)
    return pl.pallas_call(
        _oproj_kernel,
        grid=(m // tm,),
        in_specs=[
            pl.BlockSpec((tm, D_MODEL), lambda i: (i, 0)),
            pl.BlockSpec((tm, D_MODEL), lambda i: (i, 0)),
            pl.BlockSpec((D_MODEL, D_MODEL), lambda i: (0, 0)),
        ],
        out_specs=pl.BlockSpec((tm, D_MODEL), lambda i: (i, 0)),
        out_shape=jax.ShapeDtypeStruct((m, D_MODEL), F32),
        compiler_params=_cparams(("arbitrary",)),
        name="out_proj",
    )(x, a, w_o_b)


def _ffn_kernel(*refs, stream_len, tm, n_f, final_norm):
    if final_norm:
        (x_ref, g_ref, wg_ref, wv_ref, hg_ref, hv_ref, cwg_ref, cwv_ref, cbg_ref, cbv_ref,
         wo_ref, gf_ref, o_ref, tg_ref, tv_ref, h_ref, pg_ref, pv_ref, u_ref) = refs
    else:
        (x_ref, g_ref, wg_ref, wv_ref, hg_ref, hv_ref, cwg_ref, cwv_ref, cbg_ref, cbv_ref,
         wo_ref, o_ref, tg_ref, tv_ref, h_ref, pg_ref, pv_ref, u_ref) = refs
        gf_ref = None
    i = pl.program_id(0)
    f = pl.program_id(1)
    streams_per_tile = max(tm // stream_len, 1)
    tiles_per_stream = max(stream_len // tm, 1)

    @pl.when(f == 0)
    def _():
        h_ref[...] = _rmsnorm_f32(x_ref[...], g_ref[...]).astype(BF16)
        o_ref[...] = x_ref[...]

    n_groups = FFN_ROW_GROUPS
    rg = tm // n_groups
    assert rg % SUBLANES == 0 and (streams_per_tile == 1 or streams_per_tile % n_groups == 0)
    row = lax.broadcasted_iota(jnp.int32, (rg, TF), 0)
    if streams_per_tile > 1:
        row = row % stream_len
    halves = ((wg_ref, hg_ref, cwg_ref, cbg_ref, pg_ref, tg_ref),
              (wv_ref, hv_ref, cwv_ref, cbv_ref, pv_ref, tv_ref))

    class _Up:
        def __init__(self, g, half):
            self.ref = u_ref.at[half * n_groups + g]
            self.ref[...] = jnp.dot(h_ref[g * rg:(g + 1) * rg, :], halves[half][0][...],
                                    preferred_element_type=F32)

        def __getitem__(self, idx):
            return self.ref[idx]

        def reshape(self, *shape):
            return self.ref[...].reshape(*shape)

    if streams_per_tile == 1:
        @pl.when((i % tiles_per_stream) == 0)
        def _():
            for (_, hist_ref, _, _, prev_ref, _) in halves:
                prev_ref[f, SUBLANES - (CONV_W - 1):SUBLANES, :] = hist_ref[0]

    ups = []

    def conv(g, half):
        _, hist_ref, cw_ref, cb_ref, prev_ref, _ = halves[half]
        u = ups[g][half][...]
        if streams_per_tile > 1:
            spg = streams_per_tile // n_groups
            def per_row(r):
                hr = hist_ref[g * spg:(g + 1) * spg, r:r + 1, :]
                return jnp.broadcast_to(hr, (spg, stream_len, TF)).reshape(rg, TF)
            p0, p1 = per_row(0), per_row(1)
        elif g == 0:
            p0 = prev_ref[f, SUBLANES - 2:SUBLANES - 1, :]
            p1 = prev_ref[f, SUBLANES - 1:SUBLANES, :]
        else:
            p0 = ups[g - 1][half][rg - 2:rg - 1, :]
            p1 = ups[g - 1][half][rg - 1:rg, :]
        u1 = jnp.where(row == 0, p1, pltpu.roll(u, 1, axis=0))
        u2 = jnp.where(row == 0, p0, jnp.where(row == 1, p1, pltpu.roll(u, 2, axis=0)))
        c = cb_ref[...] + cw_ref[0:1, :] * u2
        c = c + cw_ref[1:2, :] * u1
        return c + cw_ref[2:3, :] * u

    ups.append([_Up(0, half) for half in range(2)])
    for g in range(n_groups):
        cg = conv(g, 0)
        cv = conv(g, 1)
        a = ((cg * jax.nn.sigmoid(cg)) * cv).astype(BF16)
        if g + 1 < n_groups:
            ups.append([_Up(g + 1, half) for half in range(2)])
        o_ref[g * rg:(g + 1) * rg, :] += jnp.dot(a, wo_ref[...], preferred_element_type=F32)

    for half, (_, _, _, _, prev_ref, tail_ref) in enumerate(halves):
        if streams_per_tile > 1:
            spg = streams_per_tile // n_groups
            for g in range(n_groups):
                u3 = ups[g][half].reshape(spg, stream_len, TF)
                tail_ref[g * spg:(g + 1) * spg] = u3[:, stream_len - SUBLANES:, :]
        else:
            last = ups[n_groups - 1][half][rg - SUBLANES:, :]
            prev_ref[f] = last
            tail_ref[0] = last

    if final_norm:
        @pl.when(f == n_f - 1)
        def _():
            o_ref[...] = _rmsnorm_f32(o_ref[...], gf_ref[...])


def _conv_ffn(x, g, w_in_g, w_in_v, hist_g, hist_v, cw_g, cw_v, cb_g, cb_v, w_out_b, stream_len,
              g_final=None):
    m = x.shape[0]
    tm = min(TM_FFN, m)
    nf = D_FF_PAD // TF
    n_tiles = m // tm
    final_norm = g_final is not None
    if stream_len >= tm:
        assert stream_len % tm == 0
        tiles_per_stream = stream_len // tm
        hist_spec = pl.BlockSpec((1, CONV_W - 1, TF), lambda i, f: (i // tiles_per_stream, 0, f))
        tail_rows = n_tiles
        tail_spec = pl.BlockSpec((1, SUBLANES, TF), lambda i, f: (i, 0, f))
    else:
        assert tm % stream_len == 0 and stream_len % SUBLANES == 0 and n_tiles == 1
        spt = tm // stream_len
        hist_spec = pl.BlockSpec((spt, CONV_W - 1, TF), lambda i, f: (i, 0, f))
        tail_rows = m // stream_len
        tail_spec = pl.BlockSpec((spt, SUBLANES, TF), lambda i, f: (i, 0, f))
    row_tile = pl.BlockSpec((tm, D_MODEL), lambda i, f: (i, 0))
    gain = pl.BlockSpec((1, D_MODEL), lambda i, f: (0, 0))
    w_in_spec = pl.BlockSpec((D_MODEL, TF), lambda i, f: (0, f))
    cw_spec = pl.BlockSpec((CONV_W, TF), lambda i, f: (0, f))
    cb_spec = pl.BlockSpec((1, TF), lambda i, f: (0, f))
    in_specs = [row_tile, gain, w_in_spec, w_in_spec, hist_spec, hist_spec, cw_spec, cw_spec,
                cb_spec, cb_spec, pl.BlockSpec((TF, D_MODEL), lambda i, f: (f, 0))]
    args = [x, g.reshape(1, D_MODEL), w_in_g, w_in_v, hist_g, hist_v, cw_g, cw_v, cb_g, cb_v, w_out_b]
    if final_norm:
        in_specs.append(gain)
        args.append(g_final.reshape(1, D_MODEL))
    tail_shape = jax.ShapeDtypeStruct((tail_rows, SUBLANES, D_FF_PAD), F32)
    return pl.pallas_call(
        functools.partial(_ffn_kernel, stream_len=stream_len, tm=tm, n_f=nf, final_norm=final_norm),
        grid=(n_tiles, nf),
        in_specs=in_specs,
        out_specs=[row_tile, tail_spec, tail_spec],
        out_shape=[jax.ShapeDtypeStruct((m, D_MODEL), F32), tail_shape, tail_shape],
        scratch_shapes=[
            pltpu.VMEM((tm, D_MODEL), BF16),
            pltpu.VMEM((nf, SUBLANES, TF), F32),
            pltpu.VMEM((nf, SUBLANES, TF), F32),
            pltpu.VMEM((2 * FFN_ROW_GROUPS, tm // FFN_ROW_GROUPS, TF), F32),
        ],
        compiler_params=_cparams(("arbitrary", "arbitrary")),
        name="conv_ffn",
    )(*args)


def _pad_ff(a):
    pad = [(0, 0)] * (a.ndim - 1) + [(0, D_FF_PAD - D_FF)]
    return jnp.pad(a, pad)


def _split_ff(a):
    return _pad_ff(a[..., :D_FF]), _pad_ff(a[..., D_FF:])


def kernel(x_prompt, x_sample, cache_fox_k, cache_fox_v, cache_fox_logf, cache_band_k, cache_band_v,
           state_conv, g_attn, w_qkv, w_fgate, b_fgate, rel_bias, w_o, g_ffn, w_in, conv_w, conv_b,
           w_out, g_final):
    depth = g_attn.shape[0]
    bp, sp, _ = x_prompt.shape
    bs, ss, _ = x_sample.shape
    past = cache_fox_k.shape[2]
    n_band_cache = cache_band_k.shape[2]
    n_keep = min(BAND_PAST, sp)
    assert bp == 1

    xp = x_prompt.reshape(bp * sp, D_MODEL)
    xs = x_sample.reshape(bs * ss, D_MODEL)

    n_fox = cache_fox_k.shape[0]
    n_band = cache_band_k.shape[0]
    cfk = cache_fox_k.reshape(n_fox * bs, past * N_HEADS, HEAD_DIM)
    cfv = cache_fox_v.reshape(n_fox * bs, past * N_HEADS, HEAD_DIM)
    cbk = cache_band_k.reshape(n_band * bs, n_band_cache * N_HEADS, HEAD_DIM)
    cbv = cache_band_v.reshape(n_band * bs, n_band_cache * N_HEADS, HEAD_DIM)

    hist0 = jnp.zeros((bp, CONV_W - 1, D_FF_PAD), F32)

    fox_lp, fox_ks, fox_vs, fox_ls = [], [], [], []
    band_kp, band_vp, band_ks, band_vs = [], [], [], []
    conv_p, conv_s = [], []
    fox_kv_p = None

    for i in range(depth):
        j = i // N_MIXERS
        is_fox = (i % N_MIXERS == 0)
        wqkv_b = w_qkv[i].astype(BF16)
        wo_b = w_o[i].astype(BF16)
        w_in_g, w_in_v = _split_ff(w_in[i].astype(BF16))
        cw_g, cw_v = _split_ff(conv_w[i])
        cb_g, cb_v = _split_ff(conv_b[i].reshape(1, 2 * D_FF))
        w_out_b = jnp.pad(w_out[i].astype(BF16), ((0, D_FF_PAD - D_FF), (0, 0)))
        hs_g, hs_v = _split_ff(state_conv[i])

        if is_fox:
            wf_b = jnp.pad(w_fgate[j].astype(BF16), ((0, 0), (0, LANES - N_HEADS)))
            bf = jnp.pad(b_fgate[j].reshape(1, N_HEADS), ((0, 0), (0, LANES - N_HEADS)))
            qp, kst, vst, kpb, vpb, lfp_pad = _qkv_proj(xp, g_attn[i], wqkv_b, wf_b, bf,
                                                        slot=j, n_slots=n_fox, kv_stack=fox_kv_p)
            fox_kv_p = (kst, vst)
            qs, ks32, vs32, ksb, vsb, lfs = _qkv_proj(xs, g_attn[i], wqkv_b, wf_b, bf)
            lfp = lfp_pad[:, :N_HEADS]
            lfs = lfs[:, :N_HEADS]

            aq, ak, cum_pad, qn2, kn2 = _fox_aug(lfp_pad, qp, kpb)
            op = _fox_prompt_attn(qp, aq, kpb, ak, vpb, _fox_first_live_pair(cum_pad, qn2, kn2))

            lf_all = jnp.concatenate([
                jnp.transpose(cache_fox_logf[j], (0, 2, 1)),
                jnp.transpose(lfs.reshape(bs, ss, N_HEADS), (0, 2, 1)),
                jnp.zeros((bs, N_HEADS, LANES - ss), F32),
            ], axis=-1).reshape(bs * N_HEADS, past + LANES)
            cum_s = _cumsum_lanes(lf_all).reshape(bs, N_HEADS, past + LANES)
            cq_s = jnp.transpose(cum_s[:, :, past:past + ss], (0, 2, 1))
            osm = _fox_sample_attn(qs, ksb, vsb, cfk, cfv, j, cq_s, cum_s, bs, ss)

            fox_lp.append(lfp.reshape(bp, sp, N_HEADS))
            fox_ks.append(ks32.reshape(bs, ss, N_HEADS, HEAD_DIM))
            fox_vs.append(vs32.reshape(bs, ss, N_HEADS, HEAD_DIM))
            fox_ls.append(lfs.reshape(bs, ss, N_HEADS))
        else:
            qp, kp32, vp32, kpb, vpb = _qkv_proj(xp, g_attn[i], wqkv_b)
            qs, ks32, vs32, ksb, vsb = _qkv_proj(xs, g_attn[i], wqkv_b)
            op = _band_prompt_attn(qp, kpb, vpb,
                                   _band_profile(rel_bias[j], BAND_PAST, TQ_BAND, 4 * TQ_BAND))
            osm = _band_sample_attn(qs, ksb, vsb, cbk, cbv, j, rel_bias[j], past, bs, ss)
            last_rows = slice(sp - n_keep, sp)
            band_kp.append(kp32[0, last_rows].reshape(bp, n_keep, N_HEADS, HEAD_DIM))
            band_vp.append(vp32[0, last_rows].reshape(bp, n_keep, N_HEADS, HEAD_DIM))
            band_ks.append(ks32.reshape(bs, ss, N_HEADS, HEAD_DIM))
            band_vs.append(vs32.reshape(bs, ss, N_HEADS, HEAD_DIM))

        xp = _out_proj(xp, op, wo_b)
        xs = _out_proj(xs, osm, wo_b)

        gf = g_final if i == depth - 1 else None
        xp, tgp, tvp = _conv_ffn(xp, g_ffn[i], w_in_g, w_in_v, hist0, hist0, cw_g, cw_v, cb_g, cb_v,
                                 w_out_b, sp, gf)
        xs, tgs, tvs = _conv_ffn(xs, g_ffn[i], w_in_g, w_in_v, hs_g, hs_v, cw_g, cw_v, cb_g, cb_v,
                                 w_out_b, ss, gf)
        keep = slice(SUBLANES - (CONV_W - 1), SUBLANES)
        conv_p.append(jnp.concatenate([tgp[-1:, keep, :D_FF], tvp[-1:, keep, :D_FF]], axis=-1))
        conv_s.append(jnp.concatenate([tgs[:, keep, :D_FF], tvs[:, keep, :D_FF]], axis=-1))

    y_prompt = xp.reshape(bp, sp, D_MODEL)
    y_sample = xs.reshape(bs, ss, D_MODEL)
    fox_kp, fox_vp = (a.reshape(n_fox, bp, sp, N_HEADS, HEAD_DIM) for a in fox_kv_p)
    return (y_prompt, y_sample,
            fox_kp, fox_vp, jnp.stack(fox_lp),
            jnp.stack(fox_ks), jnp.stack(fox_vs), jnp.stack(fox_ls),
            jnp.stack(band_kp), jnp.stack(band_vp), jnp.stack(band_ks), jnp.stack(band_vs),
            jnp.stack(conv_p), jnp.stack(conv_s))
```
